```python
import math
import jax
import jax.numpy as jnp
from jax import lax
import numpy as np

D_MODEL = 1024
BATCH = 8
SEQ = 8192
DEPTH = 2

N_BRANCH = 3
SSM_GROUP_WIDTH = 16
SSM_GROUPS = 16
D_SSM = SSM_GROUP_WIDTH * SSM_GROUPS
SSM_STATE = 64
SSM_DT_MIN = 1e-3
SSM_DT_MAX = 1e-1
HGRN_HEADS = 4
HGRN_KDIM = 64
HGRN_VDIM = 64
D_HGRN_K = HGRN_HEADS * HGRN_KDIM
D_HGRN_V = HGRN_HEADS * HGRN_VDIM
HGRN_CHUNK = 64
MOBA_HEADS = 4
MOBA_HEAD_DIM = 64
D_MOBA = MOBA_HEADS * MOBA_HEAD_DIM
MOBA_BLOCK = 256
MOBA_TOPK = 3
MOBA_QUERY_BLOCK = 32
IN_WIDTHS = (D_SSM, D_HGRN_K, D_HGRN_K, D_HGRN_V, D_HGRN_V, D_MOBA, D_MOBA, D_MOBA, N_BRANCH * D_MODEL)
D_IN = sum(IN_WIDTHS)
N_EXPERTS = 64
TOP_K = 6
D_EXPERT = 256
D_SHARED = 256
ROUTED_SCALE = 2.5
MOE_ROW_BLOCK = 256
N_MOD = 6
EPS = 1e-6

kernel_name = 'hybrid_s5_hgrn2_moba_moe_adaln'


def rms_norm(x, gain):
    xf = x.astype(jnp.float32)
    y = xf * lax.rsqrt(jnp.mean(xf * xf, axis=-1, keepdims=True) + EPS)
    return y * gain.astype(jnp.float32)


def _complex_affine_combine(earlier, later):
    a1r, a1i, b1r, b1i = earlier
    a2r, a2i, b2r, b2i = later
    return (a2r * a1r - a2i * a1i,
            a2r * a1i + a2i * a1r,
            a2r * b1r - a2i * b1i + b2r,
            a2r * b1i + a2i * b1r + b2i)


def s5_mixer(u, log_dt, lam_re, lam_im, b_re, b_im, c_re, c_im, d_skip, w_glu):
    bsz, seq, _ = u.shape
    f32 = jnp.float32
    ug = u.astype(f32).reshape(bsz, seq, SSM_GROUPS, SSM_GROUP_WIDTH)
    dt = jnp.exp(log_dt.astype(f32))[:, None]
    lr, li = lam_re.astype(f32), lam_im.astype(f32)
    mag = jnp.exp(lr * dt)
    abar_r, abar_i = mag * jnp.cos(li * dt), mag * jnp.sin(li * dt)
    inv = 1.0 / (lr * lr + li * li)
    coef_r = ((abar_r - 1.0) * lr + abar_i * li) * inv
    coef_i = (abar_i * lr - (abar_r - 1.0) * li) * inv
    br, bi = b_re.astype(f32), b_im.astype(f32)
    bbar_r = coef_r[..., None] * br - coef_i[..., None] * bi
    bbar_i = coef_r[..., None] * bi + coef_i[..., None] * br
    bu_r = jnp.einsum('blgh,gph->blgp', ug, bbar_r)
    bu_i = jnp.einsum('blgh,gph->blgp', ug, bbar_i)
    shp = (1, seq, SSM_GROUPS, SSM_STATE)
    a_r = jnp.broadcast_to(abar_r, shp)
    a_i = jnp.broadcast_to(abar_i, shp)
    _, _, s_r, s_i = lax.associative_scan(_complex_affine_combine, (a_r, a_i, bu_r, bu_i), axis=1)
    y = (jnp.einsum('blgp,ghp->blgh', s_r, c_re.astype(f32))
         - jnp.einsum('blgp,ghp->blgh', s_i, c_im.astype(f32)))
    y = (y + d_skip.astype(f32).reshape(SSM_GROUPS, SSM_GROUP_WIDTH) * ug).reshape(bsz, seq, D_SSM)
    z = (jax.nn.gelu(y).astype(u.dtype) @ w_glu).astype(f32)
    return z[..., :D_MODEL] * jax.nn.sigmoid(z[..., D_MODEL:])


def hgrn2_mixer(q, f_raw, inp, g, lower_bound, norm_g, w_o):
    bsz, seq, _ = q.shape
    f32 = jnp.float32
    n_chunks = seq // HGRN_CHUNK
    lb = lower_bound.astype(f32)
    forget = lb + (1.0 - lb) * jax.nn.sigmoid(f_raw.astype(f32))
    log_f = jnp.log(forget)
    key = 1.0 - forget

    def chunks(t, dim):
        return t.astype(f32).reshape(bsz, n_chunks, HGRN_CHUNK, HGRN_HEADS, dim).transpose(1, 0, 3, 2, 4)

    xs = (chunks(q, HGRN_KDIM), chunks(log_f, HGRN_KDIM), chunks(key, HGRN_KDIM), chunks(inp, HGRN_VDIM))
    causal = jnp.tril(jnp.ones((HGRN_CHUNK, HGRN_CHUNK), dtype=bool))[:, :, None]

    def step(state, xc):
        qc, lfc, kc, ic = xc
        cum = jnp.cumsum(lfc, axis=2)
        rel = cum[:, :, :, None, :] - cum[:, :, None, :, :]
        decay = jnp.exp(jnp.where(causal, rel, -jnp.inf))
        scores = jnp.einsum('bhtk,bhtsk,bhsk->bhts', qc, decay, kc)
        out = (jnp.einsum('bhts,bhsv->bhtv', scores, ic)
               + jnp.einsum('bhtk,bhkv->bhtv', qc * jnp.exp(cum), state))
        last = cum[:, :, -1, :]
        state = (jnp.exp(last)[..., None] * state
                 + jnp.einsum('bhsk,bhsv->bhkv', kc * jnp.exp(last[:, :, None, :] - cum), ic))
        return state, out

    s0 = jnp.zeros((bsz, HGRN_HEADS, HGRN_KDIM, HGRN_VDIM), f32)
    _, o = lax.scan(step, s0, xs)
    o = o.transpose(1, 0, 3, 2, 4).reshape(bsz, seq, HGRN_HEADS, HGRN_VDIM)
    o = rms_norm(o, norm_g) * jax.nn.silu(g.astype(f32).reshape(bsz, seq, HGRN_HEADS, HGRN_VDIM))
    return o.reshape(bsz, seq, D_HGRN_V).astype(q.dtype) @ w_o


def moba_mixer(q, k, v, q_norm_g, k_norm_g, w_o):
    bsz, seq, _ = q.shape
    f32 = jnp.float32
    hs = (bsz, seq, MOBA_HEADS, MOBA_HEAD_DIM)
    qh = rms_norm(q.reshape(hs), q_norm_g).transpose(0, 2, 1, 3) * (MOBA_HEAD_DIM ** -0.5)
    kh = rms_norm(k.reshape(hs), k_norm_g).transpose(0, 2, 1, 3)
    vh = v.astype(f32).reshape(hs).transpose(0, 2, 1, 3)
    n_blocks = -(-seq // MOBA_BLOCK)
    pad = ((0, 0), (0, 0), (0, n_blocks * MOBA_BLOCK - seq), (0, 0))
    kb = jnp.pad(kh, pad).reshape(bsz, MOBA_HEADS, n_blocks, MOBA_BLOCK, MOBA_HEAD_DIM)
    vb = jnp.pad(vh, pad).reshape(bsz, MOBA_HEADS, n_blocks, MOBA_BLOCK, MOBA_HEAD_DIM)
    kmean = jnp.mean(kb, axis=3)
    n_sel = min(MOBA_TOPK, n_blocks)
    b_ix = jnp.arange(bsz)[:, None, None, None]
    h_ix = jnp.arange(MOBA_HEADS)[None, :, None, None]
    block_ids = jnp.arange(n_blocks)
    key_offs = jnp.arange(MOBA_BLOCK)
    q_offs = jnp.arange(MOBA_QUERY_BLOCK)

    def query_block(qi):
        t0 = qi * MOBA_QUERY_BLOCK
        qc = lax.dynamic_slice_in_dim(qh, t0, MOBA_QUERY_BLOCK, axis=2)
        qpos = t0 + q_offs
        own = t0 // MOBA_BLOCK
        k_own = lax.dynamic_index_in_dim(kb, own, axis=2, keepdims=False)
        v_own = lax.dynamic_index_in_dim(vb, own, axis=2, keepdims=False)
        s_own = jnp.einsum('bhqd,bhnd->bhqn', qc, k_own)
        s_own = jnp.where((own * MOBA_BLOCK + key_offs)[None, :] <= qpos[:, None], s_own, -jnp.inf)
        gate = jnp.einsum('bhqd,bhjd->bhqj', qc, kmean)
        gate = jnp.where(block_ids < own, gate, -jnp.inf)
        _, sel = lax.top_k(gate, n_sel)
        valid = sel < own
        k_sel = kb[b_ix, h_ix, sel]
        v_sel = vb[b_ix, h_ix, sel]
        s_sel = jnp.einsum('bhqd,bhqjnd->bhqjn', qc, k_sel)
        s_sel = jnp.where(valid[..., None], s_sel, -jnp.inf)
        s = jnp.concatenate(
            [s_own, s_sel.reshape(bsz, MOBA_HEADS, MOBA_QUERY_BLOCK, n_sel * MOBA_BLOCK)], axis=-1)
        p = jax.nn.softmax(s, axis=-1)
        p_own = p[..., :MOBA_BLOCK]
        p_sel = p[..., MOBA_BLOCK:].reshape(bsz, MOBA_HEADS, MOBA_QUERY_BLOCK, n_sel, MOBA_BLOCK)
        return (jnp.einsum('bhqn,bhnd->bhqd', p_own, v_own)
                + jnp.einsum('bhqjn,bhqjnd->bhqd', p_sel, v_sel))

    o = lax.map(query_block, jnp.arange(seq // MOBA_QUERY_BLOCK))
    o = o.transpose(1, 0, 3, 2, 4).reshape(bsz, seq, D_MOBA)
    return o.astype(q.dtype) @ w_o


def moe_ffn(h, router_w, router_bias, w_gate, w_up, w_down, s_gate, s_up, s_down):
    bsz, seq, dm = h.shape
    f32 = jnp.float32
    n_tok = bsz * seq
    hf = h.reshape(n_tok, dm)
    scores = jax.nn.sigmoid((hf @ router_w).astype(f32))
    _, top_idx = lax.top_k(scores + router_bias.astype(f32), TOP_K)
    top_s = jnp.take_along_axis(scores, top_idx, axis=-1)
    top_w = top_s / jnp.sum(top_s, axis=-1, keepdims=True) * ROUTED_SCALE
    n_assign = n_tok * TOP_K
    e_flat = top_idx.reshape(n_assign)
    t_flat = jnp.repeat(jnp.arange(n_tok, dtype=jnp.int32), TOP_K)
    w_flat = top_w.reshape(n_assign)
    order = jnp.argsort(e_flat)
    e_sorted = e_flat[order]
    counts = jnp.bincount(e_flat, length=N_EXPERTS)
    starts = jnp.cumsum(counts) - counts
    padded = (counts + MOE_ROW_BLOCK - 1) // MOE_ROW_BLOCK * MOE_ROW_BLOCK
    padded_end = jnp.cumsum(padded)
    dest = (padded_end - padded)[e_sorted] + jnp.arange(n_assign) - starts[e_sorted]
    n_blocks = -(-n_assign // MOE_ROW_BLOCK) + N_EXPERTS
    n_rows = n_blocks * MOE_ROW_BLOCK
    row_tok = jnp.full((n_rows,), n_tok, jnp.int32).at[dest].set(t_flat[order])
    row_w = jnp.zeros((n_rows,), f32).at[dest].set(w_flat[order])
    block_expert = jnp.minimum(
        jnp.searchsorted(padded_end, jnp.arange(n_blocks) * MOE_ROW_BLOCK, side='right'), N_EXPERTS - 1)

    def block(acc, bi):
        rows = lax.dynamic_slice_in_dim(row_tok, bi * MOE_ROW_BLOCK, MOE_ROW_BLOCK)
        wts = lax.dynamic_slice_in_dim(row_w, bi * MOE_ROW_BLOCK, MOE_ROW_BLOCK)
        e = block_expert[bi]
        xb = jnp.take(hf, rows, axis=0, mode='fill', fill_value=0)
        hid = jax.nn.silu(xb @ w_gate[e]) * (xb @ w_up[e])
        yb = (hid @ w_down[e]).astype(f32) * wts[:, None]
        return acc.at[rows].add(yb, mode='drop'), None

    routed, _ = lax.scan(block, jnp.zeros((n_tok, dm), f32), jnp.arange(n_blocks))
    shared = (jax.nn.silu(hf @ s_gate) * (hf @ s_up)) @ s_down
    return (routed + shared.astype(f32)).reshape(bsz, seq, dm)


def setup_inputs(seed: int = 0) -> dict:
    key = jax.random.key(seed)
    k = jax.random.split(key, 32)
    f32 = jnp.float32

    def normal(kk, shape, scale):
        return jax.random.normal(kk, shape, f32) * scale

    def gain(kk, shape):
        return 1.0 + normal(kk, shape, 0.02)

    L, D, G, P, GW = DEPTH, D_MODEL, SSM_GROUPS, SSM_STATE, SSM_GROUP_WIDTH
    lam_im_base = jnp.pi * jnp.arange(P, dtype=f32)
    return {
        'x': normal(k[0], (BATCH, SEQ, D), 1.0),
        'c': normal(k[1], (BATCH, D), 1.0),
        'ada_w': normal(k[2], (L, D, N_MOD * D), 0.5 * D ** -0.5),
        'ada_b': normal(k[3], (L, N_MOD * D), 0.02),
        'norm1_g': gain(k[4], (L, D)),
        'w_in': normal(k[5], (L, D, D_IN), D ** -0.5),
        'ssm_log_dt': jax.random.uniform(k[6], (L, G), f32, math.log(SSM_DT_MIN), math.log(SSM_DT_MAX)),
        'ssm_lambda_re': -0.5 + normal(k[7], (L, G, P), 0.01),
        'ssm_lambda_im': lam_im_base + normal(k[8], (L, G, P), 0.01),
        'ssm_b_re': normal(k[9], (L, G, P, GW), (2 * GW) ** -0.5),
        'ssm_b_im': normal(k[10], (L, G, P, GW), (2 * GW) ** -0.5),
        'ssm_c_re': normal(k[11], (L, G, GW, P), (2 * P) ** -0.5),
        'ssm_c_im': normal(k[12], (L, G, GW, P), (2 * P) ** -0.5),
        'ssm_d': normal(k[13], (L, D_SSM), 1.0),
        'ssm_w_glu': normal(k[14], (L, D_SSM, 2 * D), D_SSM ** -0.5),
        'hgrn_lb_logits': normal(k[15], (L, D_HGRN_K), 1.0),
        'hgrn_norm_g': gain(k[16], (L, HGRN_VDIM)),
        'hgrn_w_o': normal(k[17], (L, D_HGRN_V, D), D_HGRN_V ** -0.5),
        'moba_q_norm_g': gain(k[18], (L, MOBA_HEAD_DIM)),
        'moba_k_norm_g': gain(k[19], (L, MOBA_HEAD_DIM)),
        'moba_w_o': normal(k[20], (L, D_MOBA, D), D_MOBA ** -0.5),
        'w_out': normal(k[21], (L, D, D), D ** -0.5),
        'norm2_g': gain(k[22], (L, D)),
        'router_w': normal(k[23], (L, D, N_EXPERTS), D ** -0.5),
        'router_bias': normal(k[24], (L, N_EXPERTS), 0.01),
        'exp_w_gate': normal(k[25], (L, N_EXPERTS, D, D_EXPERT), D ** -0.5),
        'exp_w_up': normal(k[26], (L, N_EXPERTS, D, D_EXPERT), D ** -0.5),
        'exp_w_down': normal(k[27], (L, N_EXPERTS, D_EXPERT, D), D_EXPERT ** -0.5),
        'shared_w_gate': normal(k[28], (L, D, D_SHARED), D ** -0.5),
        'shared_w_up': normal(k[29], (L, D, D_SHARED), D ** -0.5),
        'shared_w_down': normal(k[30], (L, D_SHARED, D), D_SHARED ** -0.5),
    }


def reference(x, c, ada_w, ada_b, norm1_g, w_in, ssm_log_dt, ssm_lambda_re, ssm_lambda_im,
              ssm_b_re, ssm_b_im, ssm_c_re, ssm_c_im, ssm_d, ssm_w_glu, hgrn_lb_logits,
              hgrn_norm_g, hgrn_w_o, moba_q_norm_g, moba_k_norm_g, moba_w_o, w_out, norm2_g,
              router_w, router_bias, exp_w_gate, exp_w_up, exp_w_down, shared_w_gate,
              shared_w_up, shared_w_down):
    f32 = jnp.float32
    bsz, seq, _ = x.shape
    lb_cum = jnp.cumsum(jax.nn.softmax(hgrn_lb_logits.astype(f32), axis=0), axis=0)
    lower_bounds = lb_cum - lb_cum[0:1]
    cond = jax.nn.silu(c.astype(f32))
    splits = np.cumsum(IN_WIDTHS)[:-1].tolist()
    for l in range(DEPTH):
        mod = cond @ ada_w[l].astype(f32) + ada_b[l].astype(f32)
        shift1, scale1, gate1, shift2, scale2, gate2 = [m[:, None, :] for m in jnp.split(mod, N_MOD, axis=-1)]
        h = (rms_norm(x, norm1_g[l]) * (1.0 + scale1) + shift1).astype(x.dtype)
        proj = h @ w_in[l]
        u, hq, hf, hi, hg, mq, mk, mv, gl = jnp.split(proj, splits, axis=-1)
        y_a = s5_mixer(u, ssm_log_dt[l], ssm_lambda_re[l], ssm_lambda_im[l], ssm_b_re[l], ssm_b_im[l],
                       ssm_c_re[l], ssm_c_im[l], ssm_d[l], ssm_w_glu[l])
        y_b = hgrn2_mixer(hq, hf, hi, hg, lower_bounds[l], hgrn_norm_g[l], hgrn_w_o[l])
        y_c = moba_mixer(mq, mk, mv, moba_q_norm_g[l], moba_k_norm_g[l], moba_w_o[l])
        gates = jax.nn.sigmoid(gl.astype(f32)).reshape(bsz, seq, N_BRANCH, D_MODEL)
        merged = gates[:, :, 0] * y_a + gates[:, :, 1] * y_b + gates[:, :, 2] * y_c
        x = x + (gate1 * (merged.astype(x.dtype) @ w_out[l])).astype(x.dtype)
        h2 = (rms_norm(x, norm2_g[l]) * (1.0 + scale2) + shift2).astype(x.dtype)
        y_ffn = moe_ffn(h2, router_w[l], router_bias[l], exp_w_gate[l], exp_w_up[l], exp_w_down[l],
                        shared_w_gate[l], shared_w_up[l], shared_w_down[l])
        x = x + (gate2 * y_ffn).astype(x.dtype)
    return x
```

```python
import functools

import jax
import jax.numpy as jnp
from jax import lax
from jax.experimental import pallas as pl
from jax.experimental.pallas import tpu as pltpu

F32 = jnp.float32
BF16 = jnp.bfloat16
I32 = jnp.int32
HIGHEST = lax.Precision.HIGHEST

N_MOD = 6
EPS = 1e-6
MOBA_BLOCK = 256
MOBA_TOPK = 3
TOP_K = 6
ROUTED_SCALE = 2.5

LANES = 128
HGRN_CHUNK = 64
HGRN_SUB = 16
MASK_BIG = 30000.0
VMEM_LIMIT = 56 * 1024 * 1024


def _nt_dot(a, b, precision=None):
    return lax.dot_general(a, b, (((1,), (1,)), ((), ())), preferred_element_type=F32,
                           precision=precision)


def _tn_dot(a, b, precision=None):
    return lax.dot_general(a, b, (((0,), (0,)), ((), ())), preferred_element_type=F32,
                           precision=precision)


def _dot(a, b, precision=None):
    return jnp.dot(a, b, preferred_element_type=F32, precision=precision)


def _sigmoid(x):
    return 1.0 / (1.0 + jnp.exp(-x))


def _silu(x):
    return x * _sigmoid(x)


def _params(*sem):
    return pltpu.CompilerParams(dimension_semantics=sem, vmem_limit_bytes=VMEM_LIMIT)


def _const_spec(shape):
    nd = len(shape)
    return pl.BlockSpec(shape, lambda *_: (0,) * nd, pipeline_mode=pl.Buffered(1))


def _ada_kernel(c_ref, w_ref, b_ref, o_ref):
    c = c_ref[...]
    o_ref[...] = _dot(_silu(c), w_ref[...], HIGHEST) + b_ref[...]


def _ada_mod(c, ada_w, ada_b):
    depth, d, n = ada_w.shape
    bsz = c.shape[0]
    tn = n // 4
    return pl.pallas_call(
        _ada_kernel,
        grid=(depth, n // tn),
        in_specs=[pl.BlockSpec((bsz, d), lambda l, j: (0, 0)),
                  pl.BlockSpec((None, d, tn), lambda l, j: (l, 0, j)),
                  pl.BlockSpec((None, 1, tn), lambda l, j: (l, 0, j))],
        out_specs=pl.BlockSpec((None, bsz, tn), lambda l, j: (l, 0, j)),
        out_shape=jax.ShapeDtypeStruct((depth, bsz, n), F32),
        compiler_params=_params("arbitrary", "arbitrary"),
        name="ada_mod",
    )(c, ada_w, ada_b.reshape(depth, 1, n))


def _prep_kernel(ldt_ref, lr_ref, li_ref, btr_ref, bti_ref, lbl_ref,
                 abar_ref, bbr_ref, bbi_ref, lb_ref):
    depth = ldt_ref.shape[0]
    for l in range(depth):
        dt = jnp.exp(ldt_ref[l])
        lr = lr_ref[l]
        li = li_ref[l]
        mag = jnp.exp(lr * dt)
        ar = mag * jnp.cos(li * dt)
        ai = mag * jnp.sin(li * dt)
        inv = 1.0 / (lr * lr + li * li)
        cr = ((ar - 1.0) * lr + ai * li) * inv
        ci = (ai * lr - (ar - 1.0) * li) * inv
        abar_ref[l, 0:1, :] = ar
        abar_ref[l, 1:2, :] = ai
        btr = btr_ref[l]
        bti = bti_ref[l]
        bbr_ref[l] = cr * btr - ci * bti
        bbi_ref[l] = cr * bti + ci * btr
    rows = [lbl_ref[l:l + 1, :] for l in range(depth)]
    mx = functools.reduce(jnp.maximum, rows)
    ex = [jnp.exp(r - mx) for r in rows]
    den = functools.reduce(lambda a, b: a + b, ex)
    run = ex[0] / den
    first = run
    lb_ref[0:1, :] = run - first
    for l in range(1, depth):
        run = run + ex[l] / den
        lb_ref[l:l + 1, :] = run - first


def _prep_params(log_dt, lam_re, lam_im, b_re, b_im, lb_logits):
    depth, g, p = lam_re.shape
    gw = b_re.shape[-1]
    n = g * p
    hk = lb_logits.shape[-1]
    ldt = jnp.repeat(log_dt, p, axis=1).reshape(depth, 1, n)
    btr = b_re.transpose(0, 3, 1, 2).reshape(depth, gw, n)
    bti = b_im.transpose(0, 3, 1, 2).reshape(depth, gw, n)
    return pl.pallas_call(
        _prep_kernel,
        out_shape=(jax.ShapeDtypeStruct((depth, 2, n), F32),
                   jax.ShapeDtypeStruct((depth, gw, n), F32),
                   jax.ShapeDtypeStruct((depth, gw, n), F32),
                   jax.ShapeDtypeStruct((depth, hk), F32)),
        name="param_prep",
    )(ldt, lam_re.reshape(depth, 1, n), lam_im.reshape(depth, 1, n), btr, bti, lb_logits)


def _inproj_kernel(x_ref, mod_ref, g_ref, w_ref, u_ref, hg_ref, mo_ref, gl_ref, *, widths):
    x = x_ref[...]
    ms = jnp.mean(x * x, axis=-1, keepdims=True)
    h = x * lax.rsqrt(ms + EPS) * g_ref[...]
    h = h * (1.0 + mod_ref[1:2, :]) + mod_ref[0:1, :]
    hb = h.astype(BF16)
    d_ssm, d_hgrn, d_moba, d_gate = widths
    o0 = 0
    u_ref[...] = _dot(hb, w_ref[:, o0:o0 + d_ssm])
    o0 += d_ssm
    hg_ref[...] = _dot(hb, w_ref[:, o0:o0 + d_hgrn])
    o0 += d_hgrn
    mo_ref[...] = _dot(hb, w_ref[:, o0:o0 + d_moba])
    o0 += d_moba
    step = 1024
    for j in range(0, d_gate, step):
        gl_ref[:, j:j + step] = _dot(hb, w_ref[:, o0 + j:o0 + j + step]).astype(BF16)


def _inproj(x, mod, norm_g, w_in_bf, widths, tm):
    bsz, seq, d = x.shape
    d_ssm, d_hgrn, d_moba, d_gate = widths
    d_in = w_in_bf.shape[1]
    row = lambda n: pl.BlockSpec((None, tm, n), lambda b, i: (b, i, 0))
    return pl.pallas_call(
        functools.partial(_inproj_kernel, widths=widths),
        grid=(bsz, seq // tm),
        in_specs=[row(d),
                  pl.BlockSpec((None, N_MOD, d), lambda b, i: (b, 0, 0)),
                  _const_spec((1, d)),
                  _const_spec((d, d_in))],
        out_specs=[row(d_ssm), row(d_hgrn), row(d_moba), row(d_gate)],
        out_shape=(jax.ShapeDtypeStruct((bsz, seq, d_ssm), F32),
                   jax.ShapeDtypeStruct((bsz, seq, d_hgrn), F32),
                   jax.ShapeDtypeStruct((bsz, seq, d_moba), F32),
                   jax.ShapeDtypeStruct((bsz, seq, d_gate), BF16)),
        compiler_params=_params("arbitrary", "arbitrary"),
        name="in_proj",
    )(x, mod, norm_g.reshape(1, d), w_in_bf)


def _s5_kernel(u_ref, wb_ref, wc_ref, abar_ref, d_ref, wglu_ref, o_ref, bu_ref, st_ref,
               *, nb, tl, ns, unroll):
    nc = ns // LANES

    @pl.when(pl.program_id(0) == 0)
    def _():
        st_ref[...] = jnp.zeros_like(st_ref)

    for b in range(nb):
        bu = _dot(u_ref[b].astype(BF16), wb_ref[...])
        for c in range(2 * nc):
            bu_ref[c, b * tl:(b + 1) * tl, :] = bu[:, c * LANES:(c + 1) * LANES]

    ar = [jnp.broadcast_to(abar_ref[0:1, c * LANES:(c + 1) * LANES], (nb, LANES)) for c in range(nc)]
    ai = [jnp.broadcast_to(abar_ref[1:2, c * LANES:(c + 1) * LANES], (nb, LANES)) for c in range(nc)]

    def outer(j, carry):
        sr, si = list(carry[:nc]), list(carry[nc:])
        for k in range(unroll):
            t = j * unroll + k
            for c in range(nc):
                br = bu_ref[c, pl.ds(t, nb, stride=tl), :]
                bi = bu_ref[nc + c, pl.ds(t, nb, stride=tl), :]
                nr = ar[c] * sr[c] - ai[c] * si[c] + br
                ni = ar[c] * si[c] + ai[c] * sr[c] + bi
                bu_ref[c, pl.ds(t, nb, stride=tl), :] = nr
                bu_ref[nc + c, pl.ds(t, nb, stride=tl), :] = ni
                sr[c], si[c] = nr, ni
        return tuple(sr) + tuple(si)

    init = tuple(st_ref[c] for c in range(2 * nc))
    fin = lax.fori_loop(0, tl // unroll, outer, init)
    for c in range(2 * nc):
        st_ref[c] = fin[c]

    dm = o_ref.shape[-1]
    for b in range(nb):
        s = jnp.concatenate([bu_ref[c, b * tl:(b + 1) * tl, :] for c in range(2 * nc)], axis=1)
        y = _dot(s.astype(BF16), wc_ref[...]) + d_ref[...] * u_ref[b]
        z = _dot(jax.nn.gelu(y).astype(BF16), wglu_ref[...])
        o_ref[b] = (z[:, :dm] * _sigmoid(z[:, dm:])).astype(o_ref.dtype)


def _s5_mixer(u, wb, wc, abar, d_skip, wglu, tl):
    bsz, seq, d_ssm = u.shape
    ns = abar.shape[-1]
    dm = wglu.shape[1] // 2
    return pl.pallas_call(
        functools.partial(_s5_kernel, nb=bsz, tl=tl, ns=ns, unroll=8),
        grid=(seq // tl,),
        in_specs=[pl.BlockSpec((bsz, tl, d_ssm), lambda i: (0, i, 0)),
                  _const_spec(wb.shape), _const_spec(wc.shape), _const_spec(abar.shape),
                  _const_spec((1, d_ssm)), _const_spec(wglu.shape)],
        out_specs=pl.BlockSpec((bsz, tl, dm), lambda i: (0, i, 0)),
        out_shape=jax.ShapeDtypeStruct((bsz, seq, dm), BF16),
        scratch_shapes=[pltpu.VMEM((2 * ns // LANES, bsz * tl, LANES), F32),
                        pltpu.VMEM((2 * ns // LANES, bsz, LANES), F32)],
        compiler_params=_params("arbitrary"),
        name="s5_mixer",
    )(u, wb, wc, abar, d_skip.reshape(1, d_ssm), wglu)


def _hgrn_kernel(x_ref, lb_ref, ng_ref, o_ref, st_ref, *, tl, hd, dk):
    cs, sb = HGRN_CHUNK, HGRN_SUB

    @pl.when(pl.program_id(1) == 0)
    def _():
        st_ref[...] = jnp.zeros_like(st_ref)

    def iota(shape, dim):
        return lax.broadcasted_iota(I32, shape, dim)

    tril = (iota((cs, cs), 0) >= iota((cs, cs), 1)).astype(F32)
    head_eq = (iota((hd, hd), 0) // dk) == (iota((hd, hd), 1) // dk)
    block_ones = head_eq.astype(BF16)
    block_mean = head_eq.astype(F32) * (1.0 / dk)
    sel = ((iota((sb, sb * sb), 1) // sb) == iota((sb, sb * sb), 0)).astype(BF16)
    s_iota = iota((sb, hd), 0)
    lane_head = iota((1, hd), 1) // dk
    n_heads = hd // dk
    lb = lb_ref[...]
    ng = ng_ref[...]

    def head_copies(a):
        return jnp.concatenate([jnp.where(lane_head == h, a, 0.0) for h in range(n_heads)], axis=0)

    def off_diag(q, key, iv, cum, t0, t1, s0, s1):
        r = cum[s1 - 1:s1, :]
        qt = q[t0:t1] * jnp.exp(cum[t0:t1] - r)
        ks = key[s0:s1] * jnp.exp(r - cum[s0:s1])
        sc = _nt_dot(qt.astype(BF16), head_copies(ks).astype(BF16))
        return _dot(sc.astype(BF16), head_copies(iv[s0:s1]).astype(BF16))

    def chunk(c, carry):
        r0 = pl.multiple_of(c * cs, cs)
        q = x_ref[pl.ds(r0, cs), 0:hd]
        fr = x_ref[pl.ds(r0, cs), hd:2 * hd]
        iv = x_ref[pl.ds(r0, cs), 2 * hd:3 * hd]
        g = x_ref[pl.ds(r0, cs), 3 * hd:4 * hd]
        forget = lb + (1.0 - lb) * _sigmoid(fr)
        lf = jnp.log(forget)
        key = 1.0 - forget
        cum = _dot(tril, lf, HIGHEST)
        last = cum[cs - 1:cs, :]
        st = st_ref[...]
        out = _nt_dot((q * jnp.exp(cum)).astype(BF16), st.astype(BF16))
        parts = []
        for a in range(cs // sb):
            lo = a * sb
            cum_a = cum[lo:lo + sb]
            qa = q[lo:lo + sb]
            ka = key[lo:lo + sb]
            slabs = []
            for t in range(sb):
                arg = jnp.where(s_iota <= t, cum_a[t:t + 1] - cum_a, -jnp.inf)
                slabs.append(jnp.exp(arg) * (qa[t:t + 1] * ka))
            z = jnp.concatenate(slabs, axis=0)
            p = _dot(z.astype(BF16), block_ones)
            w = p * jnp.concatenate([iv[lo:lo + sb]] * sb, axis=0)
            parts.append(_dot(sel, w.astype(BF16)))
        intra = jnp.concatenate(parts, axis=0)
        half = cs // 2
        lvl1 = off_diag(q, key, iv, cum, half, cs, 0, half)
        lvl2a = off_diag(q, key, iv, cum, sb, half, 0, sb)
        lvl2b = off_diag(q, key, iv, cum, half + sb, cs, half, half + sb)
        zeros = jnp.zeros((sb, hd), F32)
        intra = intra + jnp.concatenate([zeros, lvl2a, lvl1[0:sb], lvl1[sb:half] + lvl2b], axis=0)
        out = out + intra
        kc = key * jnp.exp(last - cum)
        upd = _tn_dot(iv.astype(BF16), kc.astype(BF16))
        st_ref[...] = st * jnp.exp(last) + jnp.where(head_eq, upd, 0.0)
        ms = _dot(out * out, block_mean, HIGHEST)
        on = out * lax.rsqrt(ms + EPS) * ng * _silu(g)
        o_ref[pl.ds(r0, cs), :] = on.astype(o_ref.dtype)
        return carry

    lax.fori_loop(0, tl // cs, chunk, 0)


def _hgrn_mixer(hg, lb, norm_g, tl):
    bsz, seq, d4 = hg.shape
    hd = d4 // 4
    dk = norm_g.shape[-1]
    ng = jnp.tile(norm_g, hd // dk).reshape(1, hd)
    return pl.pallas_call(
        functools.partial(_hgrn_kernel, tl=tl, hd=hd, dk=dk),
        grid=(bsz, seq // tl),
        in_specs=[pl.BlockSpec((None, tl, d4), lambda b, i: (b, i, 0)),
                  _const_spec((1, hd)), _const_spec((1, hd))],
        out_specs=pl.BlockSpec((None, tl, hd), lambda b, i: (b, i, 0)),
        out_shape=jax.ShapeDtypeStruct((bsz, seq, hd), BF16),
        scratch_shapes=[pltpu.VMEM((hd, hd), F32)],
        compiler_params=_params("arbitrary", "arbitrary"),
        name="hgrn_mixer",
    )(hg, lb.reshape(1, hd), ng)


def _moba_prep_kernel(x_ref, gq_ref, gk_ref, qn_ref, ka_ref, km_ref, vb_ref, *, nh, dh):
    i = pl.program_id(1)
    blk = MOBA_BLOCK
    hd = nh * dh

    @pl.when(i == 0)
    def _():
        km_ref[...] = jnp.zeros_like(km_ref)

    lane = lax.broadcasted_iota(I32, (blk, LANES - dh), 1)
    onehot = (lane == i).astype(F32)
    scale = dh ** -0.5
    for h in range(nh):
        qh = x_ref[:, h * dh:(h + 1) * dh]
        kh = x_ref[:, hd + h * dh:hd + (h + 1) * dh]
        qn = qh * lax.rsqrt(jnp.mean(qh * qh, axis=-1, keepdims=True) + EPS) * gq_ref[...] * scale
        kn = kh * lax.rsqrt(jnp.mean(kh * kh, axis=-1, keepdims=True) + EPS) * gk_ref[...]
        qn_ref[h] = jnp.concatenate([qn, jnp.zeros((blk, LANES - dh), F32)], axis=1)
        ka_ref[h] = jnp.concatenate([kn, onehot], axis=1).astype(BF16)
        kmean = jnp.mean(kn, axis=0, keepdims=True)
        km_ref[h, pl.ds(dh + i, 1), :] = jnp.concatenate(
            [kmean, jnp.zeros((1, LANES - dh), F32)], axis=1)
    vb_ref[...] = x_ref[:, 2 * hd:3 * hd].astype(BF16)


def _moba_kernel(q_ref, ka_ref, km_ref, vb_ref, o_ref, qa_scr, m_scr, l_scr, acc_scr, *, nh, dh):
    i = pl.program_id(1)
    blk = MOBA_BLOCK
    lane = lax.broadcasted_iota(I32, (blk, LANES), 1)
    row = lax.broadcasted_iota(I32, (blk, blk), 0)
    col = lax.broadcasted_iota(I32, (blk, blk), 1)
    r0 = pl.multiple_of(i * blk, blk)

    for h in range(nh):
        qn = q_ref[h]
        gate = _nt_dot(qn, km_ref[h], HIGHEST)
        valid = (lane >= dh) & (lane < dh + i)
        work = jnp.where(valid, gate, -jnp.inf)
        chosen = jnp.zeros((blk, LANES), jnp.bool_)
        for _ in range(MOBA_TOPK):
            mx = jnp.max(work, axis=-1, keepdims=True)
            first = jnp.min(jnp.where(work == mx, lane, LANES), axis=-1, keepdims=True)
            hit = (lane == first) & (mx > -jnp.inf)
            chosen = chosen | hit
            work = jnp.where(hit, -jnp.inf, work)
        qa_scr[h] = jnp.where(lane < dh, qn, jnp.where(chosen, 0.0, -MASK_BIG)).astype(BF16)
        qd = qn.astype(BF16)
        s = _nt_dot(qd, ka_ref[h, pl.ds(r0, blk), :])
        s = jnp.where(col <= row, s, -jnp.inf)
        m0 = jnp.max(s, axis=-1, keepdims=True)
        p = jnp.exp(s - m0)
        c0 = (h // 2) * LANES
        m_scr[h] = jnp.broadcast_to(m0, (blk, LANES))
        l_scr[h] = jnp.broadcast_to(jnp.sum(p, axis=-1, keepdims=True), (blk, LANES))
        acc_scr[h] = _dot(p.astype(BF16), vb_ref[pl.ds(r0, blk), c0:c0 + LANES])

    def past(j, carry):
        k0 = pl.multiple_of(j * blk, blk)
        for h in range(nh):
            s = _nt_dot(qa_scr[h], ka_ref[h, pl.ds(k0, blk), :])
            m_old = m_scr[h][:, 0:1]
            m_new = jnp.maximum(m_old, jnp.max(s, axis=-1, keepdims=True))
            alpha = jnp.exp(m_old - m_new)
            p = jnp.exp(s - m_new)
            c0 = (h // 2) * LANES
            l_scr[h] = alpha * l_scr[h] + jnp.sum(p, axis=-1, keepdims=True)
            acc_scr[h] = alpha * acc_scr[h] + _dot(p.astype(BF16), vb_ref[pl.ds(k0, blk), c0:c0 + LANES])
            m_scr[h] = jnp.broadcast_to(m_new, (blk, LANES))
        return carry

    lax.fori_loop(0, i, past, 0)

    outs = [acc_scr[h] / l_scr[h][:, 0:1] for h in range(nh)]
    per_tile = LANES // dh
    tiles = []
    for t0 in range(0, nh, per_tile):
        acc = outs[t0]
        for k in range(1, per_tile):
            acc = jnp.where((lane // dh) == k, outs[t0 + k], acc)
        tiles.append(acc)
    o_ref[...] = jnp.concatenate(tiles, axis=1).astype(o_ref.dtype)


def _moba_mixer(mo, gq, gk):
    bsz, seq, d3 = mo.shape
    hd = d3 // 3
    dh = gq.shape[-1]
    nh = hd // dh
    blk = MOBA_BLOCK
    nblk = seq // blk
    assert seq % blk == 0 and nblk <= LANES - dh
    qn, ka, km, vb = pl.pallas_call(
        functools.partial(_moba_prep_kernel, nh=nh, dh=dh),
        grid=(bsz, nblk),
        in_specs=[pl.BlockSpec((None, blk, d3), lambda b, i: (b, i, 0)),
                  _const_spec((1, dh)), _const_spec((1, dh))],
        out_specs=[pl.BlockSpec((None, nh, blk, LANES), lambda b, i: (b, 0, i, 0)),
                   pl.BlockSpec((None, nh, blk, LANES), lambda b, i: (b, 0, i, 0)),
                   pl.BlockSpec((None, nh, LANES, LANES), lambda b, i: (b, 0, 0, 0)),
                   pl.BlockSpec((None, blk, hd), lambda b, i: (b, i, 0))],
        out_shape=(jax.ShapeDtypeStruct((bsz, nh, seq, LANES), F32),
                   jax.ShapeDtypeStruct((bsz, nh, seq, LANES), BF16),
                   jax.ShapeDtypeStruct((bsz, nh, LANES, LANES), F32),
                   jax.ShapeDtypeStruct((bsz, seq, hd), BF16)),
        compiler_params=_params("arbitrary", "arbitrary"),
        name="moba_prep",
    )(mo, gq.reshape(1, dh), gk.reshape(1, dh))
    return pl.pallas_call(
        functools.partial(_moba_kernel, nh=nh, dh=dh),
        grid=(bsz, nblk),
        in_specs=[pl.BlockSpec((None, nh, blk, LANES), lambda b, i: (b, 0, i, 0)),
                  pl.BlockSpec((None, nh, seq, LANES), lambda b, i: (b, 0, 0, 0)),
                  pl.BlockSpec((None, nh, LANES, LANES), lambda b, i: (b, 0, 0, 0)),
                  pl.BlockSpec((None, seq, hd), lambda b, i: (b, 0, 0))],
        out_specs=pl.BlockSpec((None, blk, hd), lambda b, i: (b, i, 0)),
        out_shape=jax.ShapeDtypeStruct((bsz, seq, hd), BF16),
        scratch_shapes=[pltpu.VMEM((nh, blk, LANES), BF16),
                        pltpu.VMEM((nh, blk, LANES), F32),
                        pltpu.VMEM((nh, blk, LANES), F32),
                        pltpu.VMEM((nh, blk, LANES), F32)],
        compiler_params=_params("arbitrary", "arbitrary"),
        name="moba_attn",
    )(qn, ka, km, vb)


def _merge_kernel(x_ref, ya_ref, hb_ref, mc_ref, gl_ref, mod_ref, g2_ref, who_ref, wmo_ref, wout_ref,
                  rwh_ref, rwl_ref, rb_ref,
                  x1_ref, h2_ref, idx_ref, wgt_ref, pos_ref, cnt_ref, *, tm, ne):
    first = (pl.program_id(0) == 0) & (pl.program_id(1) == 0)

    @pl.when(first)
    def _():
        cnt_ref[...] = jnp.zeros_like(cnt_ref)

    d = x_ref.shape[-1]
    yb = _dot(hb_ref[...], who_ref[...])
    yc = _dot(mc_ref[...], wmo_ref[...])
    merged = (_sigmoid(gl_ref[:, 0:d].astype(F32)) * ya_ref[...].astype(F32)
              + _sigmoid(gl_ref[:, d:2 * d].astype(F32)) * yb
              + _sigmoid(gl_ref[:, 2 * d:3 * d].astype(F32)) * yc)
    x1 = x_ref[...] + mod_ref[2:3, :] * _dot(merged.astype(BF16), wout_ref[...])
    x1_ref[...] = x1
    ms = jnp.mean(x1 * x1, axis=-1, keepdims=True)
    h2 = x1 * lax.rsqrt(ms + EPS) * g2_ref[...]
    h2 = h2 * (1.0 + mod_ref[4:5, :]) + mod_ref[3:4, :]
    h2_ref[...] = h2
    hi = h2.astype(BF16)
    lo = (h2 - hi.astype(F32)).astype(BF16)
    logits = _nt_dot(rwh_ref[...], hi) + _nt_dot(rwh_ref[...], lo) + _nt_dot(rwl_ref[...], hi)
    scores = _sigmoid(logits)
    e_iota = lax.broadcasted_iota(I32, (ne, tm), 0)
    work = scores + rb_ref[...]
    picked = jnp.zeros((ne, tm), F32)
    hits, tops = [], []
    for _ in range(TOP_K):
        mx = jnp.max(work, axis=0, keepdims=True)
        first_e = jnp.min(jnp.where(work == mx, e_iota, ne), axis=0, keepdims=True)
        hit = e_iota == first_e
        hits.append((hit, first_e))
        tops.append(jnp.sum(jnp.where(hit, scores, 0.0), axis=0, keepdims=True))
        work = jnp.where(hit, -jnp.inf, work)
        picked = jnp.where(hit, 1.0, picked)
    total = functools.reduce(lambda a, b: a + b, tops)
    upper = (lax.broadcasted_iota(I32, (tm, tm), 0) < lax.broadcasted_iota(I32, (tm, tm), 1)).astype(BF16)
    rank = _dot(picked.astype(BF16), upper) + cnt_ref[:, 0:1]
    zero_row = jnp.zeros((1, tm), F32)
    idx_rows, w_rows, pos_rows = [], [], []
    for (hit, first_e), top in zip(hits, tops):
        idx_rows.append(first_e)
        w_rows.append(top / total * ROUTED_SCALE)
        pos_rows.append(jnp.sum(jnp.where(hit, rank, 0.0), axis=0, keepdims=True))
    pad = 8 - TOP_K
    idx_ref[...] = jnp.concatenate(idx_rows + [jnp.zeros((1, tm), I32)] * pad, axis=0)
    wgt_ref[...] = jnp.concatenate(w_rows + [zero_row] * pad, axis=0)
    pos_ref[...] = jnp.concatenate(pos_rows + [zero_row] * pad, axis=0).astype(I32)
    cnt_ref[...] = cnt_ref[...] + jnp.sum(picked, axis=1, keepdims=True)


def _merge_router(x, ya, hb, mc, gl, mod, norm2_g, who, wmo, wout, rwh, rwl, rbias, tm):
    bsz, seq, d = x.shape
    ne = rwh.shape[0]
    nt = seq // tm
    n_tok = bsz * seq
    row = lambda n: pl.BlockSpec((None, tm, n), lambda b, i: (b, i, 0))
    col = pl.BlockSpec((8, tm), lambda b, i: (0, b * nt + i))
    return pl.pallas_call(
        functools.partial(_merge_kernel, tm=tm, ne=ne),
        grid=(bsz, nt),
        in_specs=[row(d), row(d), row(hb.shape[-1]), row(mc.shape[-1]), row(gl.shape[-1]),
                  pl.BlockSpec((None, N_MOD, d), lambda b, i: (b, 0, 0)),
                  _const_spec((1, d)), _const_spec(who.shape), _const_spec(wmo.shape),
                  _const_spec(wout.shape), _const_spec(rwh.shape), _const_spec(rwl.shape),
                  _const_spec((ne, 1))],
        out_specs=[row(d), row(d), col, col, col,
                   pl.BlockSpec((ne, LANES), lambda b, i: (0, 0))],
        out_shape=(jax.ShapeDtypeStruct((bsz, seq, d), F32),
                   jax.ShapeDtypeStruct((bsz, seq, d), F32),
                   jax.ShapeDtypeStruct((8, n_tok), I32),
                   jax.ShapeDtypeStruct((8, n_tok), F32),
                   jax.ShapeDtypeStruct((8, n_tok), I32),
                   jax.ShapeDtypeStruct((ne, LANES), F32)),
        compiler_params=_params("arbitrary", "arbitrary"),
        name="merge_router",
    )(x, ya, hb, mc, gl, mod, norm2_g.reshape(1, d), who, wmo, wout, rwh, rwl, rbias.reshape(ne, 1))


def _dispatch_kernel(dest_ref, h_ref, xs_ref, sem, *, tm):
    def body(t, carry):
        for k in range(TOP_K):
            pltpu.make_async_copy(h_ref.at[pl.ds(t, 1), :],
                                  xs_ref.at[pl.ds(dest_ref[k, t], 1), :], sem).start()
        return carry

    lax.fori_loop(0, tm, body, 0)
    for k in range(TOP_K):
        pltpu.make_async_copy(h_ref, xs_ref.at[pl.ds(0, tm), :], sem).wait()


def _dispatch(dest, h2, n_rows, tm):
    n_tok, d = h2.shape
    return pl.pallas_call(
        functools.partial(_dispatch_kernel, tm=tm),
        grid=(n_tok // tm,),
        in_specs=[pl.BlockSpec((8, tm), lambda i: (0, i), memory_space=pltpu.SMEM),
                  pl.BlockSpec((tm, d), lambda i: (i, 0))],
        out_specs=pl.BlockSpec(memory_space=pl.ANY),
        out_shape=jax.ShapeDtypeStruct((n_rows, d), F32),
        scratch_shapes=[pltpu.SemaphoreType.DMA],
        compiler_params=_params("arbitrary"),
        name="moe_dispatch",
    )(dest, h2)


def _expert_kernel(be_ref, nv_ref, x_ref, wg_ref, wu_ref, wd_ref, y_ref, *, rb):
    nv = nv_ref[pl.program_id(0)]

    @pl.when(nv > 0)
    def _():
        rows = lax.broadcasted_iota(I32, (rb, 1), 0)
        x = jnp.where(rows < nv, x_ref[...], 0.0).astype(BF16)
        g = _dot(x, wg_ref[...])
        u = _dot(x, wu_ref[...])
        y_ref[...] = _dot((_silu(g) * u).astype(BF16), wd_ref[...])


def _experts(block_expert, block_rows, xs, wg, wu, wd, rb):
    n_rows, d = xs.shape
    n_blocks = n_rows // rb
    de = wg.shape[-1]
    live = lambda i, be, nv: jnp.where(nv[i] > 0, i, n_blocks - 1)
    grid_spec = pltpu.PrefetchScalarGridSpec(
        num_scalar_prefetch=2,
        grid=(n_blocks,),
        in_specs=[pl.BlockSpec((rb, d), lambda i, be, nv: (live(i, be, nv), 0)),
                  pl.BlockSpec((None, d, de), lambda i, be, nv: (be[i], 0, 0)),
                  pl.BlockSpec((None, d, de), lambda i, be, nv: (be[i], 0, 0)),
                  pl.BlockSpec((None, de, d), lambda i, be, nv: (be[i], 0, 0))],
        out_specs=pl.BlockSpec((rb, d), lambda i, be, nv: (live(i, be, nv), 0)),
    )
    return pl.pallas_call(
        functools.partial(_expert_kernel, rb=rb),
        grid_spec=grid_spec,
        out_shape=jax.ShapeDtypeStruct((n_rows, d), F32),
        compiler_params=_params("arbitrary"),
        name="moe_experts",
    )(block_expert, block_rows, xs, wg, wu, wd)


def _combine_kernel(dest_ref, wgt_ref, x1_ref, h2_ref, mod_ref, sg_ref, su_ref, sd_ref, ys_ref,
                    o_ref, gbuf, sem, *, tm):
    def body(t, carry):
        for k in range(TOP_K):
            pltpu.make_async_copy(ys_ref.at[pl.ds(dest_ref[k, t], 1), :],
                                  gbuf.at[k, pl.ds(t, 1), :], sem).start()
        return carry

    lax.fori_loop(0, tm, body, 0)
    hb = h2_ref[...].astype(BF16)
    hid = _silu(_dot(hb, sg_ref[...])) * _dot(hb, su_ref[...])
    y = _dot(hid.astype(BF16), sd_ref[...])
    wcol = wgt_ref[...].T
    for k in range(TOP_K):
        pltpu.make_async_copy(ys_ref.at[pl.ds(0, tm), :], gbuf.at[k], sem).wait()
    for k in range(TOP_K):
        y = y + gbuf[k] * wcol[:, k:k + 1]
    o_ref[...] = x1_ref[...] + mod_ref[5:6, :] * y


def _combine(dest, wgt, x1, h2, mod, sg, su, sd, ys, tm):
    bsz, seq, d = x1.shape
    nt = seq // tm
    row = pl.BlockSpec((None, tm, d), lambda b, i: (b, i, 0))
    return pl.pallas_call(
        functools.partial(_combine_kernel, tm=tm),
        grid=(bsz, nt),
        in_specs=[pl.BlockSpec((8, tm), lambda b, i: (0, b * nt + i), memory_space=pltpu.SMEM),
                  pl.BlockSpec((8, tm), lambda b, i: (0, b * nt + i)),
                  row, row,
                  pl.BlockSpec((None, N_MOD, d), lambda b, i: (b, 0, 0)),
                  _const_spec(sg.shape), _const_spec(su.shape), _const_spec(sd.shape),
                  pl.BlockSpec(memory_space=pl.ANY)],
        out_specs=row,
        out_shape=jax.ShapeDtypeStruct((bsz, seq, d), F32),
        scratch_shapes=[pltpu.VMEM((TOP_K, tm, d), F32), pltpu.SemaphoreType.DMA],
        compiler_params=_params("arbitrary", "arbitrary"),
        name="moe_combine",
    )(dest, wgt, x1, h2, mod, sg, su, sd, ys)


def _routing_tables(idx, pos, counts, n_tok, ne, rb):
    n_assign = n_tok * TOP_K
    n_blocks = -(-n_assign // rb) + ne
    padded = (counts + rb - 1) // rb * rb
    pend = jnp.cumsum(padded)
    pstart = pend - padded
    dest = pstart[idx[:TOP_K]] + pos[:TOP_K]
    dest = jnp.concatenate([dest, jnp.zeros((8 - TOP_K, n_tok), I32)], axis=0)
    bstart = jnp.arange(n_blocks, dtype=I32) * rb
    bexp = jnp.minimum(jnp.searchsorted(pend, bstart, side='right'), ne - 1).astype(I32)
    brows = jnp.clip(counts[bexp] - (bstart - pstart[bexp]), 0, rb).astype(I32)
    return dest, bexp, brows, n_blocks


def _block_diag_in(bbt, g):
    gw, n = bbt.shape
    p = n // g
    rows = jnp.tile(bbt, (g, 1))
    rg = jnp.arange(g * gw)[:, None] // gw
    cg = jnp.arange(n)[None, :] // p
    return jnp.where(rg == cg, rows, 0.0)


def _block_diag_out(c, sign):
    g, gw, p = c.shape
    cols = jnp.tile(c.transpose(0, 2, 1).reshape(g * p, gw), (1, g))
    rg = jnp.arange(g * p)[:, None] // p
    cg = jnp.arange(g * gw)[None, :] // gw
    return jnp.where(rg == cg, sign * cols, 0.0)


def kernel(x, c, ada_w, ada_b, norm1_g, w_in, ssm_log_dt, ssm_lambda_re, ssm_lambda_im, ssm_b_re,
           ssm_b_im, ssm_c_re, ssm_c_im, ssm_d, ssm_w_glu, hgrn_lb_logits, hgrn_norm_g, hgrn_w_o,
           moba_q_norm_g, moba_k_norm_g, moba_w_o, w_out, norm2_g, router_w, router_bias,
           exp_w_gate, exp_w_up, exp_w_down, shared_w_gate, shared_w_up, shared_w_down):
    bsz, seq, d = x.shape
    depth = w_in.shape[0]
    n_tok = bsz * seq
    g = ssm_lambda_re.shape[1]
    d_ssm = ssm_d.shape[-1]
    d_hk = hgrn_lb_logits.shape[-1]
    d_moba = moba_w_o.shape[1]
    ne = router_w.shape[-1]
    widths = (d_ssm, 4 * d_hk, 3 * d_moba, 3 * d)
    tm = min(256, seq)
    rb = 512

    mod_all = _ada_mod(c, ada_w, ada_b).reshape(depth, bsz, N_MOD, d)
    abar, bbr, bbi, lower = _prep_params(ssm_log_dt, ssm_lambda_re, ssm_lambda_im, ssm_b_re, ssm_b_im,
                                         hgrn_lb_logits)

    for l in range(depth):
        mod = mod_all[l]
        u, hg, mo, gl = _inproj(x, mod, norm1_g[l], w_in[l].astype(BF16), widths, tm)
        wb = jnp.concatenate([_block_diag_in(bbr[l], g), _block_diag_in(bbi[l], g)], axis=1).astype(BF16)
        wc = jnp.concatenate([_block_diag_out(ssm_c_re[l], 1.0), _block_diag_out(ssm_c_im[l], -1.0)],
                             axis=0).astype(BF16)
        ya = _s5_mixer(u, wb, wc, abar[l], ssm_d[l], ssm_w_glu[l].astype(BF16), min(128, seq))
        hb = _hgrn_mixer(hg, lower[l], hgrn_norm_g[l], min(256, seq))
        mc = _moba_mixer(mo, moba_q_norm_g[l], moba_k_norm_g[l])
        rwt = router_w[l].T
        rwh = rwt.astype(BF16)
        rwl = (rwt - rwh.astype(F32)).astype(BF16)
        x1, h2, idx, wgt, pos, cnt = _merge_router(
            x, ya, hb, mc, gl, mod, norm2_g[l], hgrn_w_o[l].astype(BF16), moba_w_o[l].astype(BF16),
            w_out[l].astype(BF16), rwh, rwl, router_bias[l], tm)
        counts = cnt[:, 0].astype(I32)
        dest, bexp, brows, n_blocks = _routing_tables(idx, pos, counts, n_tok, ne, rb)
        xs = _dispatch(dest, h2.reshape(n_tok, d), n_blocks * rb, tm)
        ys = _experts(bexp, brows, xs, exp_w_gate[l].astype(BF16), exp_w_up[l].astype(BF16),
                      exp_w_down[l].astype(BF16), rb)
        x = _combine(dest, wgt, x1, h2, mod, shared_w_gate[l].astype(BF16),
                     shared_w_up[l].astype(BF16), shared_w_down[l].astype(BF16), ys, tm)
    return x
```

```python
import functools

import jax
import jax.numpy as jnp
from jax import lax
from jax.experimental import pallas as pl
from jax.experimental.pallas import tpu as pltpu

F32 = jnp.float32
BF16 = jnp.bfloat16
I32 = jnp.int32
HIGHEST = lax.Precision.HIGHEST

N_MOD = 6
EPS = 1e-6
MOBA_BLOCK = 256
MOBA_TOPK = 3
TOP_K = 6
ROUTED_SCALE = 2.5

LANES = 128
HGRN_CHUNK = 64
HGRN_SUB = 16
MASK_BIG = 30000.0
VMEM_LIMIT = 56 * 1024 * 1024


def _nt_dot(a, b, precision=None):
    return lax.dot_general(a, b, (((1,), (1,)), ((), ())), preferred_element_type=F32,
                           precision=precision)


def _tn_dot(a, b, precision=None):
    return lax.dot_general(a, b, (((0,), (0,)), ((), ())), preferred_element_type=F32,
                           precision=precision)


def _dot(a, b, precision=None):
    return jnp.dot(a, b, preferred_element_type=F32, precision=precision)


def _sigmoid(x):
    return 1.0 / (1.0 + jnp.exp(-x))


def _silu(x):
    return x * _sigmoid(x)


def _params(*sem):
    return pltpu.CompilerParams(dimension_semantics=sem, vmem_limit_bytes=VMEM_LIMIT)


def _const_spec(shape):
    nd = len(shape)
    return pl.BlockSpec(shape, lambda *_: (0,) * nd, pipeline_mode=pl.Buffered(1))


def _ada_kernel(c_ref, w_ref, b_ref, o_ref):
    c = c_ref[...]
    o_ref[...] = _dot(_silu(c), w_ref[...], HIGHEST) + b_ref[...]


def _ada_mod(c, ada_w, ada_b):
    depth, d, n = ada_w.shape
    bsz = c.shape[0]
    tn = n // 4
    return pl.pallas_call(
        _ada_kernel,
        grid=(depth, n // tn),
        in_specs=[pl.BlockSpec((bsz, d), lambda l, j: (0, 0)),
                  pl.BlockSpec((None, d, tn), lambda l, j: (l, 0, j)),
                  pl.BlockSpec((None, 1, tn), lambda l, j: (l, 0, j))],
        out_specs=pl.BlockSpec((None, bsz, tn), lambda l, j: (l, 0, j)),
        out_shape=jax.ShapeDtypeStruct((depth, bsz, n), F32),
        compiler_params=_params("arbitrary", "arbitrary"),
        name="ada_mod",
    )(c, ada_w, ada_b.reshape(depth, 1, n))


def _prep_kernel(ldt_ref, lr_ref, li_ref, btr_ref, bti_ref, lbl_ref,
                 abar_ref, bbr_ref, bbi_ref, lb_ref):
    depth = ldt_ref.shape[0]
    for l in range(depth):
        dt = jnp.exp(ldt_ref[l])
        lr = lr_ref[l]
        li = li_ref[l]
        mag = jnp.exp(lr * dt)
        ar = mag * jnp.cos(li * dt)
        ai = mag * jnp.sin(li * dt)
        inv = 1.0 / (lr * lr + li * li)
        cr = ((ar - 1.0) * lr + ai * li) * inv
        ci = (ai * lr - (ar - 1.0) * li) * inv
        abar_ref[l, 0:1, :] = ar
        abar_ref[l, 1:2, :] = ai
        btr = btr_ref[l]
        bti = bti_ref[l]
        bbr_ref[l] = cr * btr - ci * bti
        bbi_ref[l] = cr * bti + ci * btr
    rows = [lbl_ref[l:l + 1, :] for l in range(depth)]
    mx = functools.reduce(jnp.maximum, rows)
    ex = [jnp.exp(r - mx) for r in rows]
    den = functools.reduce(lambda a, b: a + b, ex)
    run = ex[0] / den
    first = run
    lb_ref[0:1, :] = run - first
    for l in range(1, depth):
        run = run + ex[l] / den
        lb_ref[l:l + 1, :] = run - first


def _prep_params(log_dt, lam_re, lam_im, b_re, b_im, lb_logits):
    depth, g, p = lam_re.shape
    gw = b_re.shape[-1]
    n = g * p
    hk = lb_logits.shape[-1]
    ldt = jnp.repeat(log_dt, p, axis=1).reshape(depth, 1, n)
    btr = b_re.transpose(0, 3, 1, 2).reshape(depth, gw, n)
    bti = b_im.transpose(0, 3, 1, 2).reshape(depth, gw, n)
    return pl.pallas_call(
        _prep_kernel,
        out_shape=(jax.ShapeDtypeStruct((depth, 2, n), F32),
                   jax.ShapeDtypeStruct((depth, gw, n), F32),
                   jax.ShapeDtypeStruct((depth, gw, n), F32),
                   jax.ShapeDtypeStruct((depth, hk), F32)),
        name="param_prep",
    )(ldt, lam_re.reshape(depth, 1, n), lam_im.reshape(depth, 1, n), btr, bti, lb_logits)


def _inproj_kernel(x_ref, mod_ref, g_ref, w_ref, u_ref, hg_ref, mo_ref, gl_ref, *, widths):
    x = x_ref[...]
    ms = jnp.mean(x * x, axis=-1, keepdims=True)
    h = x * lax.rsqrt(ms + EPS) * g_ref[...]
    h = h * (1.0 + mod_ref[1:2, :]) + mod_ref[0:1, :]
    hb = h.astype(BF16)
    d_ssm, d_hgrn, d_moba, d_gate = widths
    o0 = 0
    u_ref[...] = _dot(hb, w_ref[:, o0:o0 + d_ssm])
    o0 += d_ssm
    hg_ref[...] = _dot(hb, w_ref[:, o0:o0 + d_hgrn])
    o0 += d_hgrn
    mo_ref[...] = _dot(hb, w_ref[:, o0:o0 + d_moba])
    o0 += d_moba
    step = 1024
    for j in range(0, d_gate, step):
        gl_ref[:, j:j + step] = _dot(hb, w_ref[:, o0 + j:o0 + j + step]).astype(BF16)


def _inproj(x, mod, norm_g, w_in_bf, widths, tm):
    bsz, seq, d = x.shape
    d_ssm, d_hgrn, d_moba, d_gate = widths
    d_in = w_in_bf.shape[1]
    row = lambda n: pl.BlockSpec((None, tm, n), lambda b, i: (b, i, 0))
    return pl.pallas_call(
        functools.partial(_inproj_kernel, widths=widths),
        grid=(bsz, seq // tm),
        in_specs=[row(d),
                  pl.BlockSpec((None, N_MOD, d), lambda b, i: (b, 0, 0)),
                  _const_spec((1, d)),
                  _const_spec((d, d_in))],
        out_specs=[row(d_ssm), row(d_hgrn), row(d_moba), row(d_gate)],
        out_shape=(jax.ShapeDtypeStruct((bsz, seq, d_ssm), F32),
                   jax.ShapeDtypeStruct((bsz, seq, d_hgrn), F32),
                   jax.ShapeDtypeStruct((bsz, seq, d_moba), F32),
                   jax.ShapeDtypeStruct((bsz, seq, d_gate), BF16)),
        compiler_params=_params("arbitrary", "arbitrary"),
        name="in_proj",
    )(x, mod, norm_g.reshape(1, d), w_in_bf)


def _s5_kernel(u_ref, wb_ref, wc_ref, abar_ref, d_ref, wglu_ref, o_ref, bu_ref, st_ref,
               *, nb, tl, ns, unroll):
    nc = ns // LANES

    @pl.when(pl.program_id(0) == 0)
    def _():
        st_ref[...] = jnp.zeros_like(st_ref)

    for b in range(nb):
        bu = _dot(u_ref[b].astype(BF16), wb_ref[...])
        for c in range(2 * nc):
            bu_ref[c, b * tl:(b + 1) * tl, :] = bu[:, c * LANES:(c + 1) * LANES]

    ar = [jnp.broadcast_to(abar_ref[0:1, c * LANES:(c + 1) * LANES], (nb, LANES)) for c in range(nc)]
    ai = [jnp.broadcast_to(abar_ref[1:2, c * LANES:(c + 1) * LANES], (nb, LANES)) for c in range(nc)]

    def outer(j, carry):
        sr, si = list(carry[:nc]), list(carry[nc:])
        for k in range(unroll):
            t = j * unroll + k
            for c in range(nc):
                br = bu_ref[c, pl.ds(t, nb, stride=tl), :]
                bi = bu_ref[nc + c, pl.ds(t, nb, stride=tl), :]
                nr = ar[c] * sr[c] - ai[c] * si[c] + br
                ni = ar[c] * si[c] + ai[c] * sr[c] + bi
                bu_ref[c, pl.ds(t, nb, stride=tl), :] = nr
                bu_ref[nc + c, pl.ds(t, nb, stride=tl), :] = ni
                sr[c], si[c] = nr, ni
        return tuple(sr) + tuple(si)

    init = tuple(st_ref[c] for c in range(2 * nc))
    fin = lax.fori_loop(0, tl // unroll, outer, init)
    for c in range(2 * nc):
        st_ref[c] = fin[c]

    dm = o_ref.shape[-1]
    for b in range(nb):
        s = jnp.concatenate([bu_ref[c, b * tl:(b + 1) * tl, :] for c in range(2 * nc)], axis=1)
        y = _dot(s.astype(BF16), wc_ref[...]) + d_ref[...] * u_ref[b]
        z = _dot(jax.nn.gelu(y).astype(BF16), wglu_ref[...])
        o_ref[b] = (z[:, :dm] * _sigmoid(z[:, dm:])).astype(o_ref.dtype)


def _s5_mixer(u, wb, wc, abar, d_skip, wglu, tl):
    bsz, seq, d_ssm = u.shape
    ns = abar.shape[-1]
    dm = wglu.shape[1] // 2
    return pl.pallas_call(
        functools.partial(_s5_kernel, nb=bsz, tl=tl, ns=ns, unroll=8),
        grid=(seq // tl,),
        in_specs=[pl.BlockSpec((bsz, tl, d_ssm), lambda i: (0, i, 0)),
                  _const_spec(wb.shape), _const_spec(wc.shape), _const_spec(abar.shape),
                  _const_spec((1, d_ssm)), _const_spec(wglu.shape)],
        out_specs=pl.BlockSpec((bsz, tl, dm), lambda i: (0, i, 0)),
        out_shape=jax.ShapeDtypeStruct((bsz, seq, dm), BF16),
        scratch_shapes=[pltpu.VMEM((2 * ns // LANES, bsz * tl, LANES), F32),
                        pltpu.VMEM((2 * ns // LANES, bsz, LANES), F32)],
        compiler_params=_params("arbitrary"),
        name="s5_mixer",
    )(u, wb, wc, abar, d_skip.reshape(1, d_ssm), wglu)


def _hgrn_kernel(x_ref, lb_ref, ng_ref, o_ref, st_ref, *, tl, hd, dk):
    cs, sb = HGRN_CHUNK, HGRN_SUB

    @pl.when(pl.program_id(1) == 0)
    def _():
        st_ref[...] = jnp.zeros_like(st_ref)

    def iota(shape, dim):
        return lax.broadcasted_iota(I32, shape, dim)

    tril = (iota((cs, cs), 0) >= iota((cs, cs), 1)).astype(F32)
    head_eq = (iota((hd, hd), 0) // dk) == (iota((hd, hd), 1) // dk)
    block_ones = head_eq.astype(BF16)
    block_mean = head_eq.astype(F32) * (1.0 / dk)
    sel = ((iota((sb, sb * sb), 1) // sb) == iota((sb, sb * sb), 0)).astype(BF16)
    s_iota = iota((sb, hd), 0)
    lane_head = iota((1, hd), 1) // dk
    n_heads = hd // dk
    lb = lb_ref[...]
    ng = ng_ref[...]

    def head_copies(a):
        return jnp.concatenate([jnp.where(lane_head == h, a, 0.0) for h in range(n_heads)], axis=0)

    def off_diag(q, key, iv, cum, t0, t1, s0, s1):
        r = cum[s1 - 1:s1, :]
        qt = q[t0:t1] * jnp.exp(cum[t0:t1] - r)
        ks = key[s0:s1] * jnp.exp(r - cum[s0:s1])
        sc = _nt_dot(qt.astype(BF16), head_copies(ks).astype(BF16))
        return _dot(sc.astype(BF16), head_copies(iv[s0:s1]).astype(BF16))

    def chunk(c, carry):
        r0 = pl.multiple_of(c * cs, cs)
        q = x_ref[pl.ds(r0, cs), 0:hd]
        fr = x_ref[pl.ds(r0, cs), hd:2 * hd]
        iv = x_ref[pl.ds(r0, cs), 2 * hd:3 * hd]
        g = x_ref[pl.ds(r0, cs), 3 * hd:4 * hd]
        forget = lb + (1.0 - lb) * _sigmoid(fr)
        lf = jnp.log(forget)
        key = 1.0 - forget
        cum = _dot(tril, lf, HIGHEST)
        last = cum[cs - 1:cs, :]
        st = st_ref[...]
        out = _nt_dot((q * jnp.exp(cum)).astype(BF16), st.astype(BF16))
        parts = []
        for a in range(cs // sb):
            lo = a * sb
            cum_a = cum[lo:lo + sb]
            qa = q[lo:lo + sb]
            ka = key[lo:lo + sb]
            slabs = []
            for t in range(sb):
                arg = jnp.where(s_iota <= t, cum_a[t:t + 1] - cum_a, -jnp.inf)
                slabs.append(jnp.exp(arg) * (qa[t:t + 1] * ka))
            z = jnp.concatenate(slabs, axis=0)
            p = _dot(z.astype(BF16), block_ones)
            w = p * jnp.concatenate([iv[lo:lo + sb]] * sb, axis=0)
            parts.append(_dot(sel, w.astype(BF16)))
        intra = jnp.concatenate(parts, axis=0)
        half = cs // 2
        lvl1 = off_diag(q, key, iv, cum, half, cs, 0, half)
        lvl2a = off_diag(q, key, iv, cum, sb, half, 0, sb)
        lvl2b = off_diag(q, key, iv, cum, half + sb, cs, half, half + sb)
        zeros = jnp.zeros((sb, hd), F32)
        intra = intra + jnp.concatenate([zeros, lvl2a, lvl1[0:sb], lvl1[sb:half] + lvl2b], axis=0)
        out = out + intra
        kc = key * jnp.exp(last - cum)
        upd = _tn_dot(iv.astype(BF16), kc.astype(BF16))
        st_ref[...] = st * jnp.exp(last) + jnp.where(head_eq, upd, 0.0)
        ms = _dot(out * out, block_mean, HIGHEST)
        on = out * lax.rsqrt(ms + EPS) * ng * _silu(g)
        o_ref[pl.ds(r0, cs), :] = on.astype(o_ref.dtype)
        return carry

    lax.fori_loop(0, tl // cs, chunk, 0)


def _hgrn_mixer(hg, lb, norm_g, tl):
    bsz, seq, d4 = hg.shape
    hd = d4 // 4
    dk = norm_g.shape[-1]
    ng = jnp.tile(norm_g, hd // dk).reshape(1, hd)
    return pl.pallas_call(
        functools.partial(_hgrn_kernel, tl=tl, hd=hd, dk=dk),
        grid=(bsz, seq // tl),
        in_specs=[pl.BlockSpec((None, tl, d4), lambda b, i: (b, i, 0)),
                  _const_spec((1, hd)), _const_spec((1, hd))],
        out_specs=pl.BlockSpec((None, tl, hd), lambda b, i: (b, i, 0)),
        out_shape=jax.ShapeDtypeStruct((bsz, seq, hd), BF16),
        scratch_shapes=[pltpu.VMEM((hd, hd), F32)],
        compiler_params=_params("arbitrary", "arbitrary"),
        name="hgrn_mixer",
    )(hg, lb.reshape(1, hd), ng)


def _moba_prep_kernel(x_ref, gq_ref, gk_ref, qt_ref, ka_ref, km_ref, vt_ref, *, nh, dh):
    i = pl.program_id(1)
    blk = MOBA_BLOCK
    hd = nh * dh

    @pl.when(i == 0)
    def _():
        km_ref[...] = jnp.zeros_like(km_ref)

    lane = lax.broadcasted_iota(I32, (blk, LANES - dh), 1)
    onehot = (lane == i).astype(F32)
    pad = jnp.zeros((blk, LANES - dh), F32)
    scale = dh ** -0.5
    for h in range(nh):
        qh = x_ref[:, h * dh:(h + 1) * dh]
        kh = x_ref[:, hd + h * dh:hd + (h + 1) * dh]
        vh = x_ref[:, 2 * hd + h * dh:2 * hd + (h + 1) * dh]
        qn = qh * lax.rsqrt(jnp.mean(qh * qh, axis=-1, keepdims=True) + EPS) * gq_ref[...] * scale
        kn = kh * lax.rsqrt(jnp.mean(kh * kh, axis=-1, keepdims=True) + EPS) * gk_ref[...]
        qt_ref[h] = jnp.concatenate([qn, pad], axis=1).T
        ka_ref[h] = jnp.concatenate([kn, onehot], axis=1).astype(BF16)
        kmean = jnp.mean(kn, axis=0, keepdims=True)
        km_ref[h, pl.ds(i, 1), :] = jnp.concatenate([kmean, jnp.zeros((1, LANES - dh), F32)], axis=1)
        vt_ref[h] = jnp.concatenate([vh, pad + 1.0], axis=1).T.astype(BF16)


def _moba_kernel(qt_ref, ka_ref, km_ref, vt_ref, o_ref, qa_scr, m_scr, acc_scr, *, nh, dh, nbp):
    i = pl.program_id(1)
    blk = MOBA_BLOCK
    r0 = pl.multiple_of(i * blk, blk)
    b_iota = lax.broadcasted_iota(I32, (nbp, blk), 0)
    key_i = lax.broadcasted_iota(I32, (blk, blk), 0)
    qry_i = lax.broadcasted_iota(I32, (blk, blk), 1)

    for h in range(nh):
        qt = qt_ref[h]
        gate = _dot(km_ref[h], qt, HIGHEST)
        work = jnp.where(b_iota < i, gate, -jnp.inf)
        chosen = jnp.zeros((nbp, blk), jnp.bool_)
        for _ in range(MOBA_TOPK):
            mx = jnp.max(work, axis=0, keepdims=True)
            first = jnp.min(jnp.where(work == mx, b_iota, nbp), axis=0, keepdims=True)
            hit = (b_iota == first) & (mx > -jnp.inf)
            chosen = chosen | hit
            work = jnp.where(hit, -jnp.inf, work)
        pen = jnp.where(chosen, 0.0, -MASK_BIG)
        qa_scr[h] = jnp.concatenate(
            [qt[0:dh], pen, jnp.zeros((LANES - dh - nbp, blk), F32)], axis=0).astype(BF16)
        s = _dot(ka_ref[h, pl.ds(r0, blk), :], qt.astype(BF16))
        s = jnp.where(key_i <= qry_i, s, -jnp.inf)
        m0 = jnp.max(s, axis=0, keepdims=True)
        m_scr[h] = m0
        acc_scr[h] = _dot(vt_ref[h, i], jnp.exp(s - m0).astype(BF16))

    def past(j, carry):
        k0 = pl.multiple_of(j * blk, blk)
        for h in range(nh):
            s = _dot(ka_ref[h, pl.ds(k0, blk), :], qa_scr[h])
            m_old = m_scr[h]
            m_new = jnp.maximum(m_old, jnp.max(s, axis=0, keepdims=True))
            p = jnp.exp(s - m_new).astype(BF16)
            acc_scr[h] = jnp.exp(m_old - m_new) * acc_scr[h] + _dot(vt_ref[h, j], p)
            m_scr[h] = m_new
        return carry

    lax.fori_loop(0, i, past, 0)

    for h in range(nh):
        acc = acc_scr[h]
        o_ref[h * dh:(h + 1) * dh, :] = (acc[0:dh] / acc[dh:dh + 1]).astype(o_ref.dtype)


def _moba_mixer(mo, gq, gk):
    bsz, seq, d3 = mo.shape
    hd = d3 // 3
    dh = gq.shape[-1]
    nh = hd // dh
    blk = MOBA_BLOCK
    nblk = seq // blk
    nbp = -(-nblk // 8) * 8
    assert seq % blk == 0 and dh + nbp <= LANES
    qt, ka, km, vt = pl.pallas_call(
        functools.partial(_moba_prep_kernel, nh=nh, dh=dh),
        grid=(bsz, nblk),
        in_specs=[pl.BlockSpec((None, blk, d3), lambda b, i: (b, i, 0)),
                  _const_spec((1, dh)), _const_spec((1, dh))],
        out_specs=[pl.BlockSpec((None, nh, LANES, blk), lambda b, i: (b, 0, 0, i)),
                   pl.BlockSpec((None, nh, blk, LANES), lambda b, i: (b, 0, i, 0)),
                   pl.BlockSpec((None, nh, nbp, LANES), lambda b, i: (b, 0, 0, 0)),
                   pl.BlockSpec((None, nh, None, LANES, blk), lambda b, i: (b, 0, i, 0, 0))],
        out_shape=(jax.ShapeDtypeStruct((bsz, nh, LANES, seq), F32),
                   jax.ShapeDtypeStruct((bsz, nh, seq, LANES), BF16),
                   jax.ShapeDtypeStruct((bsz, nh, nbp, LANES), F32),
                   jax.ShapeDtypeStruct((bsz, nh, nblk, LANES, blk), BF16)),
        compiler_params=_params("arbitrary", "arbitrary"),
        name="moba_prep",
    )(mo, gq.reshape(1, dh), gk.reshape(1, dh))
    return pl.pallas_call(
        functools.partial(_moba_kernel, nh=nh, dh=dh, nbp=nbp),
        grid=(bsz, nblk),
        in_specs=[pl.BlockSpec((None, nh, LANES, blk), lambda b, i: (b, 0, 0, i)),
                  pl.BlockSpec((None, nh, seq, LANES), lambda b, i: (b, 0, 0, 0)),
                  pl.BlockSpec((None, nh, nbp, LANES), lambda b, i: (b, 0, 0, 0)),
                  pl.BlockSpec((None, nh, nblk, LANES, blk), lambda b, i: (b, 0, 0, 0, 0))],
        out_specs=pl.BlockSpec((None, hd, blk), lambda b, i: (b, 0, i)),
        out_shape=jax.ShapeDtypeStruct((bsz, hd, seq), BF16),
        scratch_shapes=[pltpu.VMEM((nh, LANES, blk), BF16),
                        pltpu.VMEM((nh, 1, blk), F32),
                        pltpu.VMEM((nh, LANES, blk), F32)],
        compiler_params=_params("arbitrary", "arbitrary"),
        name="moba_attn",
    )(qt, ka, km, vt)


def _merge_kernel(x_ref, ya_ref, hb_ref, mc_ref, gl_ref, mod_ref, g2_ref, who_ref, wmo_ref, wout_ref,
                  rwh_ref, rwl_ref, rb_ref,
                  x1_ref, h2_ref, idx_ref, wgt_ref, pos_ref, cnt_ref, *, tm, ne):
    first = (pl.program_id(0) == 0) & (pl.program_id(1) == 0)

    @pl.when(first)
    def _():
        cnt_ref[...] = jnp.zeros_like(cnt_ref)

    d = x_ref.shape[-1]
    yb = _dot(hb_ref[...], who_ref[...])
    yc = _tn_dot(mc_ref[...], wmo_ref[...])
    merged = (_sigmoid(gl_ref[:, 0:d].astype(F32)) * ya_ref[...].astype(F32)
              + _sigmoid(gl_ref[:, d:2 * d].astype(F32)) * yb
              + _sigmoid(gl_ref[:, 2 * d:3 * d].astype(F32)) * yc)
    x1 = x_ref[...] + mod_ref[2:3, :] * _dot(merged.astype(BF16), wout_ref[...])
    x1_ref[...] = x1
    ms = jnp.mean(x1 * x1, axis=-1, keepdims=True)
    h2 = x1 * lax.rsqrt(ms + EPS) * g2_ref[...]
    h2 = h2 * (1.0 + mod_ref[4:5, :]) + mod_ref[3:4, :]
    h2_ref[...] = h2
    hi = h2.astype(BF16)
    lo = (h2 - hi.astype(F32)).astype(BF16)
    logits = _nt_dot(rwh_ref[...], hi) + _nt_dot(rwh_ref[...], lo) + _nt_dot(rwl_ref[...], hi)
    scores = _sigmoid(logits)
    e_iota = lax.broadcasted_iota(I32, (ne, tm), 0)
    work = scores + rb_ref[...]
    picked = jnp.zeros((ne, tm), F32)
    hits, tops = [], []
    for _ in range(TOP_K):
        mx = jnp.max(work, axis=0, keepdims=True)
        first_e = jnp.min(jnp.where(work == mx, e_iota, ne), axis=0, keepdims=True)
        hit = e_iota == first_e
        hits.append((hit, first_e))
        tops.append(jnp.sum(jnp.where(hit, scores, 0.0), axis=0, keepdims=True))
        work = jnp.where(hit, -jnp.inf, work)
        picked = jnp.where(hit, 1.0, picked)
    total = functools.reduce(lambda a, b: a + b, tops)
    upper = (lax.broadcasted_iota(I32, (tm, tm), 0) < lax.broadcasted_iota(I32, (tm, tm), 1)).astype(BF16)
    rank = _dot(picked.astype(BF16), upper) + cnt_ref[:, 0:1]
    zero_row = jnp.zeros((1, tm), F32)
    idx_rows, w_rows, pos_rows = [], [], []
    for (hit, first_e), top in zip(hits, tops):
        idx_rows.append(first_e)
        w_rows.append(top / total * ROUTED_SCALE)
        pos_rows.append(jnp.sum(jnp.where(hit, rank, 0.0), axis=0, keepdims=True))
    pad = 8 - TOP_K
    idx_ref[...] = jnp.concatenate(idx_rows + [jnp.zeros((1, tm), I32)] * pad, axis=0)
    wgt_ref[...] = jnp.concatenate(w_rows + [zero_row] * pad, axis=0)
    pos_ref[...] = jnp.concatenate(pos_rows + [zero_row] * pad, axis=0).astype(I32)
    cnt_ref[...] = cnt_ref[...] + jnp.sum(picked, axis=1, keepdims=True)


def _merge_router(x, ya, hb, mc, gl, mod, norm2_g, who, wmo, wout, rwh, rwl, rbias, tm):
    bsz, seq, d = x.shape
    ne = rwh.shape[0]
    nt = seq // tm
    n_tok = bsz * seq
    row = lambda n: pl.BlockSpec((None, tm, n), lambda b, i: (b, i, 0))
    col = pl.BlockSpec((8, tm), lambda b, i: (0, b * nt + i))
    return pl.pallas_call(
        functools.partial(_merge_kernel, tm=tm, ne=ne),
        grid=(bsz, nt),
        in_specs=[row(d), row(d), row(hb.shape[-1]),
                  pl.BlockSpec((None, mc.shape[1], tm), lambda b, i: (b, 0, i)), row(gl.shape[-1]),
                  pl.BlockSpec((None, N_MOD, d), lambda b, i: (b, 0, 0)),
                  _const_spec((1, d)), _const_spec(who.shape), _const_spec(wmo.shape),
                  _const_spec(wout.shape), _const_spec(rwh.shape), _const_spec(rwl.shape),
                  _const_spec((ne, 1))],
        out_specs=[row(d), row(d), col, col, col,
                   pl.BlockSpec((ne, LANES), lambda b, i: (0, 0))],
        out_shape=(jax.ShapeDtypeStruct((bsz, seq, d), F32),
                   jax.ShapeDtypeStruct((bsz, seq, d), F32),
                   jax.ShapeDtypeStruct((8, n_tok), I32),
                   jax.ShapeDtypeStruct((8, n_tok), F32),
                   jax.ShapeDtypeStruct((8, n_tok), I32),
                   jax.ShapeDtypeStruct((ne, LANES), F32)),
        compiler_params=_params("arbitrary", "arbitrary"),
        name="merge_router",
    )(x, ya, hb, mc, gl, mod, norm2_g.reshape(1, d), who, wmo, wout, rwh, rwl, rbias.reshape(ne, 1))


def _dispatch_kernel(dest_ref, h_ref, xs_ref, sem, *, tm):
    def body(t, carry):
        for k in range(TOP_K):
            pltpu.make_async_copy(h_ref.at[pl.ds(t, 1), :],
                                  xs_ref.at[pl.ds(dest_ref[k, t], 1), :], sem).start()
        return carry

    lax.fori_loop(0, tm, body, 0)
    for k in range(TOP_K):
        pltpu.make_async_copy(h_ref, xs_ref.at[pl.ds(0, tm), :], sem).wait()


def _dispatch(dest, h2, n_rows, tm):
    n_tok, d = h2.shape
    return pl.pallas_call(
        functools.partial(_dispatch_kernel, tm=tm),
        grid=(n_tok // tm,),
        in_specs=[pl.BlockSpec((8, tm), lambda i: (0, i), memory_space=pltpu.SMEM),
                  pl.BlockSpec((tm, d), lambda i: (i, 0))],
        out_specs=pl.BlockSpec(memory_space=pl.ANY),
        out_shape=jax.ShapeDtypeStruct((n_rows, d), F32),
        scratch_shapes=[pltpu.SemaphoreType.DMA],
        compiler_params=_params("arbitrary"),
        name="moe_dispatch",
    )(dest, h2)


def _expert_kernel(be_ref, nv_ref, x_ref, wg_ref, wu_ref, wd_ref, y_ref, *, rb):
    nv = nv_ref[pl.program_id(0)]

    @pl.when(nv > 0)
    def _():
        rows = lax.broadcasted_iota(I32, (rb, 1), 0)
        x = jnp.where(rows < nv, x_ref[...], 0.0).astype(BF16)
        g = _dot(x, wg_ref[...])
        u = _dot(x, wu_ref[...])
        y_ref[...] = _dot((_silu(g) * u).astype(BF16), wd_ref[...])


def _experts(block_expert, block_rows, xs, wg, wu, wd, rb):
    n_rows, d = xs.shape
    n_blocks = n_rows // rb
    de = wg.shape[-1]
    live = lambda i, be, nv: jnp.where(nv[i] > 0, i, n_blocks - 1)
    grid_spec = pltpu.PrefetchScalarGridSpec(
        num_scalar_prefetch=2,
        grid=(n_blocks,),
        in_specs=[pl.BlockSpec((rb, d), lambda i, be, nv: (live(i, be, nv), 0)),
                  pl.BlockSpec((None, d, de), lambda i, be, nv: (be[i], 0, 0)),
                  pl.BlockSpec((None, d, de), lambda i, be, nv: (be[i], 0, 0)),
                  pl.BlockSpec((None, de, d), lambda i, be, nv: (be[i], 0, 0))],
        out_specs=pl.BlockSpec((rb, d), lambda i, be, nv: (live(i, be, nv), 0)),
    )
    return pl.pallas_call(
        functools.partial(_expert_kernel, rb=rb),
        grid_spec=grid_spec,
        out_shape=jax.ShapeDtypeStruct((n_rows, d), F32),
        compiler_params=_params("arbitrary"),
        name="moe_experts",
    )(block_expert, block_rows, xs, wg, wu, wd)


def _combine_kernel(dest_ref, wgt_ref, x1_ref, h2_ref, mod_ref, sg_ref, su_ref, sd_ref, ys_ref,
                    o_ref, gbuf, sem, *, tm):
    def body(t, carry):
        for k in range(TOP_K):
            pltpu.make_async_copy(ys_ref.at[pl.ds(dest_ref[k, t], 1), :],
                                  gbuf.at[k, pl.ds(t, 1), :], sem).start()
        return carry

    lax.fori_loop(0, tm, body, 0)
    hb = h2_ref[...].astype(BF16)
    hid = _silu(_dot(hb, sg_ref[...])) * _dot(hb, su_ref[...])
    y = _dot(hid.astype(BF16), sd_ref[...])
    wcol = wgt_ref[...].T
    for k in range(TOP_K):
        pltpu.make_async_copy(ys_ref.at[pl.ds(0, tm), :], gbuf.at[k], sem).wait()
    for k in range(TOP_K):
        y = y + gbuf[k] * wcol[:, k:k + 1]
    o_ref[...] = x1_ref[...] + mod_ref[5:6, :] * y


def _combine(dest, wgt, x1, h2, mod, sg, su, sd, ys, tm):
    bsz, seq, d = x1.shape
    nt = seq // tm
    row = pl.BlockSpec((None, tm, d), lambda b, i: (b, i, 0))
    return pl.pallas_call(
        functools.partial(_combine_kernel, tm=tm),
        grid=(bsz, nt),
        in_specs=[pl.BlockSpec((8, tm), lambda b, i: (0, b * nt + i), memory_space=pltpu.SMEM),
                  pl.BlockSpec((8, tm), lambda b, i: (0, b * nt + i)),
                  row, row,
                  pl.BlockSpec((None, N_MOD, d), lambda b, i: (b, 0, 0)),
                  _const_spec(sg.shape), _const_spec(su.shape), _const_spec(sd.shape),
                  pl.BlockSpec(memory_space=pl.ANY)],
        out_specs=row,
        out_shape=jax.ShapeDtypeStruct((bsz, seq, d), F32),
        scratch_shapes=[pltpu.VMEM((TOP_K, tm, d), F32), pltpu.SemaphoreType.DMA],
        compiler_params=_params("arbitrary", "arbitrary"),
        name="moe_combine",
    )(dest, wgt, x1, h2, mod, sg, su, sd, ys)


def _dest_kernel(ps_ref, idx_ref, pos_ref, o_ref, *, ne):
    idx = idx_ref[...]
    acc = pos_ref[...]
    for e in range(ne):
        acc = acc + jnp.where(idx == e, ps_ref[e], 0)
    o_ref[...] = acc


def _routing_tables(idx, pos, counts, n_tok, ne, rb):
    n_assign = n_tok * TOP_K
    n_blocks = -(-n_assign // rb) + ne
    padded = (counts + rb - 1) // rb * rb
    pend = jnp.cumsum(padded)
    pstart = (pend - padded).astype(I32)
    tt = min(8192, n_tok)
    col = pl.BlockSpec((8, tt), lambda i: (0, i))
    dest = pl.pallas_call(
        functools.partial(_dest_kernel, ne=ne),
        grid=(n_tok // tt,),
        in_specs=[pl.BlockSpec(memory_space=pltpu.SMEM), col, col],
        out_specs=col,
        out_shape=jax.ShapeDtypeStruct((8, n_tok), I32),
        compiler_params=_params("arbitrary"),
        name="moe_dest",
    )(pstart, idx, pos)
    bstart = jnp.arange(n_blocks, dtype=I32) * rb
    bexp = jnp.minimum(jnp.sum((pend[None, :] <= bstart[:, None]).astype(I32), axis=1), ne - 1)
    brows = jnp.clip(counts[bexp] - (bstart - pstart[bexp]), 0, rb).astype(I32)
    return dest, bexp, brows, n_blocks


def _block_diag_in(bbt, g):
    gw, n = bbt.shape
    p = n // g
    rows = jnp.tile(bbt, (g, 1))
    rg = jnp.arange(g * gw)[:, None] // gw
    cg = jnp.arange(n)[None, :] // p
    return jnp.where(rg == cg, rows, 0.0)


def _block_diag_out(c, sign):
    g, gw, p = c.shape
    cols = jnp.tile(c.transpose(0, 2, 1).reshape(g * p, gw), (1, g))
    rg = jnp.arange(g * p)[:, None] // p
    cg = jnp.arange(g * gw)[None, :] // gw
    return jnp.where(rg == cg, sign * cols, 0.0)


def kernel(x, c, ada_w, ada_b, norm1_g, w_in, ssm_log_dt, ssm_lambda_re, ssm_lambda_im, ssm_b_re,
           ssm_b_im, ssm_c_re, ssm_c_im, ssm_d, ssm_w_glu, hgrn_lb_logits, hgrn_norm_g, hgrn_w_o,
           moba_q_norm_g, moba_k_norm_g, moba_w_o, w_out, norm2_g, router_w, router_bias,
           exp_w_gate, exp_w_up, exp_w_down, shared_w_gate, shared_w_up, shared_w_down):
    bsz, seq, d = x.shape
    depth = w_in.shape[0]
    n_tok = bsz * seq
    g = ssm_lambda_re.shape[1]
    d_ssm = ssm_d.shape[-1]
    d_hk = hgrn_lb_logits.shape[-1]
    d_moba = moba_w_o.shape[1]
    ne = router_w.shape[-1]
    widths = (d_ssm, 4 * d_hk, 3 * d_moba, 3 * d)
    tm = min(256, seq)
    rb = 512

    mod_all = _ada_mod(c, ada_w, ada_b).reshape(depth, bsz, N_MOD, d)
    abar, bbr, bbi, lower = _prep_params(ssm_log_dt, ssm_lambda_re, ssm_lambda_im, ssm_b_re, ssm_b_im,
                                         hgrn_lb_logits)

    for l in range(depth):
        mod = mod_all[l]
        u, hg, mo, gl = _inproj(x, mod, norm1_g[l], w_in[l].astype(BF16), widths, tm)
        wb = jnp.concatenate([_block_diag_in(bbr[l], g), _block_diag_in(bbi[l], g)], axis=1).astype(BF16)
        wc = jnp.concatenate([_block_diag_out(ssm_c_re[l], 1.0), _block_diag_out(ssm_c_im[l], -1.0)],
                             axis=0).astype(BF16)
        ya = _s5_mixer(u, wb, wc, abar[l], ssm_d[l], ssm_w_glu[l].astype(BF16), min(128, seq))
        hb = _hgrn_mixer(hg, lower[l], hgrn_norm_g[l], min(256, seq))
        mc = _moba_mixer(mo, moba_q_norm_g[l], moba_k_norm_g[l])
        rwt = router_w[l].T
        rwh = rwt.astype(BF16)
        rwl = (rwt - rwh.astype(F32)).astype(BF16)
        x1, h2, idx, wgt, pos, cnt = _merge_router(
            x, ya, hb, mc, gl, mod, norm2_g[l], hgrn_w_o[l].astype(BF16), moba_w_o[l].astype(BF16),
            w_out[l].astype(BF16), rwh, rwl, router_bias[l], tm)
        counts = cnt[:, 0].astype(I32)
        dest, bexp, brows, n_blocks = _routing_tables(idx, pos, counts, n_tok, ne, rb)
        xs = _dispatch(dest, h2.reshape(n_tok, d), n_blocks * rb, tm)
        ys = _experts(bexp, brows, xs, exp_w_gate[l].astype(BF16), exp_w_up[l].astype(BF16),
                      exp_w_down[l].astype(BF16), rb)
        x = _combine(dest, wgt, x1, h2, mod, shared_w_gate[l].astype(BF16),
                     shared_w_up[l].astype(BF16), shared_w_down[l].astype(BF16), ys, tm)
    return x
```

```python
import functools

import jax
import jax.numpy as jnp
from jax import lax
from jax.experimental import pallas as pl
from jax.experimental.pallas import tpu as pltpu

F32 = jnp.float32
BF16 = jnp.bfloat16
I32 = jnp.int32
HIGHEST = lax.Precision.HIGHEST

N_MOD = 6
EPS = 1e-6
MOBA_BLOCK = 256
MOBA_TOPK = 3
TOP_K = 6
ROUTED_SCALE = 2.5

LANES = 128
HGRN_CHUNK = 64
HGRN_SUB = 16
MASK_BIG = 30000.0
VMEM_LIMIT = 56 * 1024 * 1024


def _nt_dot(a, b, precision=None):
    return lax.dot_general(a, b, (((1,), (1,)), ((), ())), preferred_element_type=F32,
                           precision=precision)


def _tn_dot(a, b, precision=None):
    return lax.dot_general(a, b, (((0,), (0,)), ((), ())), preferred_element_type=F32,
                           precision=precision)


def _dot(a, b, precision=None):
    return jnp.dot(a, b, preferred_element_type=F32, precision=precision)


def _sigmoid(x):
    return 1.0 / (1.0 + jnp.exp(-x))


def _silu(x):
    return x * _sigmoid(x)


def _params(*sem):
    return pltpu.CompilerParams(dimension_semantics=sem, vmem_limit_bytes=VMEM_LIMIT)


def _const_spec(shape):
    nd = len(shape)
    return pl.BlockSpec(shape, lambda *_: (0,) * nd, pipeline_mode=pl.Buffered(1))


def _ada_kernel(c_ref, w_ref, b_ref, o_ref):
    c = c_ref[...]
    o_ref[...] = _dot(_silu(c), w_ref[...], HIGHEST) + b_ref[...]


def _ada_mod(c, ada_w, ada_b):
    depth, d, n = ada_w.shape
    bsz = c.shape[0]
    tn = n // 4
    return pl.pallas_call(
        _ada_kernel,
        grid=(depth, n // tn),
        in_specs=[pl.BlockSpec((bsz, d), lambda l, j: (0, 0)),
                  pl.BlockSpec((None, d, tn), lambda l, j: (l, 0, j)),
                  pl.BlockSpec((None, 1, tn), lambda l, j: (l, 0, j))],
        out_specs=pl.BlockSpec((None, bsz, tn), lambda l, j: (l, 0, j)),
        out_shape=jax.ShapeDtypeStruct((depth, bsz, n), F32),
        compiler_params=_params("arbitrary", "arbitrary"),
        name="ada_mod",
    )(c, ada_w, ada_b.reshape(depth, 1, n))


def _prep_kernel(ldt_ref, lr_ref, li_ref, btr_ref, bti_ref, lbl_ref,
                 abar_ref, bbr_ref, bbi_ref, lb_ref):
    depth = ldt_ref.shape[0]
    for l in range(depth):
        dt = jnp.exp(ldt_ref[l])
        lr = lr_ref[l]
        li = li_ref[l]
        mag = jnp.exp(lr * dt)
        ar = mag * jnp.cos(li * dt)
        ai = mag * jnp.sin(li * dt)
        inv = 1.0 / (lr * lr + li * li)
        cr = ((ar - 1.0) * lr + ai * li) * inv
        ci = (ai * lr - (ar - 1.0) * li) * inv
        abar_ref[l, 0:1, :] = ar
        abar_ref[l, 1:2, :] = ai
        btr = btr_ref[l]
        bti = bti_ref[l]
        bbr_ref[l] = cr * btr - ci * bti
        bbi_ref[l] = cr * bti + ci * btr
    rows = [lbl_ref[l:l + 1, :] for l in range(depth)]
    mx = functools.reduce(jnp.maximum, rows)
    ex = [jnp.exp(r - mx) for r in rows]
    den = functools.reduce(lambda a, b: a + b, ex)
    run = ex[0] / den
    first = run
    lb_ref[0:1, :] = run - first
    for l in range(1, depth):
        run = run + ex[l] / den
        lb_ref[l:l + 1, :] = run - first


def _prep_params(log_dt, lam_re, lam_im, b_re, b_im, lb_logits):
    depth, g, p = lam_re.shape
    gw = b_re.shape[-1]
    n = g * p
    hk = lb_logits.shape[-1]
    ldt = jnp.repeat(log_dt, p, axis=1).reshape(depth, 1, n)
    btr = b_re.transpose(0, 3, 1, 2).reshape(depth, gw, n)
    bti = b_im.transpose(0, 3, 1, 2).reshape(depth, gw, n)
    return pl.pallas_call(
        _prep_kernel,
        out_shape=(jax.ShapeDtypeStruct((depth, 2, n), F32),
                   jax.ShapeDtypeStruct((depth, gw, n), F32),
                   jax.ShapeDtypeStruct((depth, gw, n), F32),
                   jax.ShapeDtypeStruct((depth, hk), F32)),
        name="param_prep",
    )(ldt, lam_re.reshape(depth, 1, n), lam_im.reshape(depth, 1, n), btr, bti, lb_logits)


def _inproj_kernel(x_ref, mod_ref, g_ref, w_ref, u_ref, hg_ref, mo_ref, gl_ref, *, widths):
    x = x_ref[...]
    ms = jnp.mean(x * x, axis=-1, keepdims=True)
    h = x * lax.rsqrt(ms + EPS) * g_ref[...]
    h = h * (1.0 + mod_ref[1:2, :]) + mod_ref[0:1, :]
    hb = h.astype(BF16)
    d_ssm, d_hgrn, d_moba, d_gate = widths
    o0 = 0
    u_ref[...] = _dot(hb, w_ref[:, o0:o0 + d_ssm])
    o0 += d_ssm
    hg_ref[...] = _dot(hb, w_ref[:, o0:o0 + d_hgrn])
    o0 += d_hgrn
    mo_ref[...] = _dot(hb, w_ref[:, o0:o0 + d_moba])
    o0 += d_moba
    step = 1024
    for j in range(0, d_gate, step):
        gl_ref[:, j:j + step] = _dot(hb, w_ref[:, o0 + j:o0 + j + step]).astype(BF16)


def _inproj(x, mod, norm_g, w_in_bf, widths, tm):
    bsz, seq, d = x.shape
    d_ssm, d_hgrn, d_moba, d_gate = widths
    d_in = w_in_bf.shape[1]
    row = lambda n: pl.BlockSpec((None, tm, n), lambda b, i: (b, i, 0))
    return pl.pallas_call(
        functools.partial(_inproj_kernel, widths=widths),
        grid=(bsz, seq // tm),
        in_specs=[row(d),
                  pl.BlockSpec((None, N_MOD, d), lambda b, i: (b, 0, 0)),
                  _const_spec((1, d)),
                  _const_spec((d, d_in))],
        out_specs=[row(d_ssm), row(d_hgrn), row(d_moba), row(d_gate)],
        out_shape=(jax.ShapeDtypeStruct((bsz, seq, d_ssm), F32),
                   jax.ShapeDtypeStruct((bsz, seq, d_hgrn), F32),
                   jax.ShapeDtypeStruct((bsz, seq, d_moba), F32),
                   jax.ShapeDtypeStruct((bsz, seq, d_gate), BF16)),
        compiler_params=_params("arbitrary", "arbitrary"),
        name="in_proj",
    )(x, mod, norm_g.reshape(1, d), w_in_bf)


def _s5_kernel(u_ref, wb_ref, wc_ref, abar_ref, d_ref, wglu_ref, o_ref, bu_ref, st_ref,
               *, nb, tl, ns, unroll):
    nc = ns // LANES

    @pl.when(pl.program_id(0) == 0)
    def _():
        st_ref[...] = jnp.zeros_like(st_ref)

    for b in range(nb):
        bu = _dot(u_ref[b].astype(BF16), wb_ref[...])
        for c in range(2 * nc):
            bu_ref[c, b * tl:(b + 1) * tl, :] = bu[:, c * LANES:(c + 1) * LANES]

    ar = [jnp.broadcast_to(abar_ref[0:1, c * LANES:(c + 1) * LANES], (nb, LANES)) for c in range(nc)]
    ai = [jnp.broadcast_to(abar_ref[1:2, c * LANES:(c + 1) * LANES], (nb, LANES)) for c in range(nc)]

    def outer(j, carry):
        sr, si = list(carry[:nc]), list(carry[nc:])
        for k in range(unroll):
            t = j * unroll + k
            for c in range(nc):
                br = bu_ref[c, pl.ds(t, nb, stride=tl), :]
                bi = bu_ref[nc + c, pl.ds(t, nb, stride=tl), :]
                nr = ar[c] * sr[c] - ai[c] * si[c] + br
                ni = ar[c] * si[c] + ai[c] * sr[c] + bi
                bu_ref[c, pl.ds(t, nb, stride=tl), :] = nr
                bu_ref[nc + c, pl.ds(t, nb, stride=tl), :] = ni
                sr[c], si[c] = nr, ni
        return tuple(sr) + tuple(si)

    init = tuple(st_ref[c] for c in range(2 * nc))
    fin = lax.fori_loop(0, tl // unroll, outer, init)
    for c in range(2 * nc):
        st_ref[c] = fin[c]

    dm = o_ref.shape[-1]
    for b in range(nb):
        s = jnp.concatenate([bu_ref[c, b * tl:(b + 1) * tl, :] for c in range(2 * nc)], axis=1)
        y = _dot(s.astype(BF16), wc_ref[...]) + d_ref[...] * u_ref[b]
        z = _dot(jax.nn.gelu(y).astype(BF16), wglu_ref[...])
        o_ref[b] = (z[:, :dm] * _sigmoid(z[:, dm:])).astype(o_ref.dtype)


def _s5_mixer(u, wb, wc, abar, d_skip, wglu, tl):
    bsz, seq, d_ssm = u.shape
    ns = abar.shape[-1]
    dm = wglu.shape[1] // 2
    return pl.pallas_call(
        functools.partial(_s5_kernel, nb=bsz, tl=tl, ns=ns, unroll=8),
        grid=(seq // tl,),
        in_specs=[pl.BlockSpec((bsz, tl, d_ssm), lambda i: (0, i, 0)),
                  _const_spec(wb.shape), _const_spec(wc.shape), _const_spec(abar.shape),
                  _const_spec((1, d_ssm)), _const_spec(wglu.shape)],
        out_specs=pl.BlockSpec((bsz, tl, dm), lambda i: (0, i, 0)),
        out_shape=jax.ShapeDtypeStruct((bsz, seq, dm), BF16),
        scratch_shapes=[pltpu.VMEM((2 * ns // LANES, bsz * tl, LANES), F32),
                        pltpu.VMEM((2 * ns // LANES, bsz, LANES), F32)],
        compiler_params=_params("arbitrary"),
        name="s5_mixer",
    )(u, wb, wc, abar, d_skip.reshape(1, d_ssm), wglu)


def _hgrn_kernel(x_ref, lb_ref, ng_ref, o_ref, st_ref, *, tl, hd, dk):
    cs, sb = HGRN_CHUNK, HGRN_SUB

    @pl.when(pl.program_id(1) == 0)
    def _():
        st_ref[...] = jnp.zeros_like(st_ref)

    def iota(shape, dim):
        return lax.broadcasted_iota(I32, shape, dim)

    tril = (iota((cs, cs), 0) >= iota((cs, cs), 1)).astype(F32)
    head_eq = (iota((hd, hd), 0) // dk) == (iota((hd, hd), 1) // dk)
    block_ones = head_eq.astype(BF16)
    block_mean = head_eq.astype(F32) * (1.0 / dk)
    sel = ((iota((sb, sb * sb), 1) // sb) == iota((sb, sb * sb), 0)).astype(BF16)
    s_iota = iota((sb, hd), 0)
    lane_head = iota((1, hd), 1) // dk
    n_heads = hd // dk
    lb = lb_ref[...]
    ng = ng_ref[...]

    def head_copies(a):
        return jnp.concatenate([jnp.where(lane_head == h, a, 0.0) for h in range(n_heads)], axis=0)

    def off_diag(q, key, iv, cum, t0, t1, s0, s1):
        r = cum[s1 - 1:s1, :]
        qt = q[t0:t1] * jnp.exp(cum[t0:t1] - r)
        ks = key[s0:s1] * jnp.exp(r - cum[s0:s1])
        sc = _nt_dot(qt.astype(BF16), head_copies(ks).astype(BF16))
        return _dot(sc.astype(BF16), head_copies(iv[s0:s1]).astype(BF16))

    def chunk(c, carry):
        r0 = pl.multiple_of(c * cs, cs)
        q = x_ref[pl.ds(r0, cs), 0:hd]
        fr = x_ref[pl.ds(r0, cs), hd:2 * hd]
        iv = x_ref[pl.ds(r0, cs), 2 * hd:3 * hd]
        g = x_ref[pl.ds(r0, cs), 3 * hd:4 * hd]
        forget = lb + (1.0 - lb) * _sigmoid(fr)
        lf = jnp.log(forget)
        key = 1.0 - forget
        cum = _dot(tril, lf, HIGHEST)
        last = cum[cs - 1:cs, :]
        st = st_ref[...]
        out = _nt_dot((q * jnp.exp(cum)).astype(BF16), st.astype(BF16))
        parts = []
        for a in range(cs // sb):
            lo = a * sb
            cum_a = cum[lo:lo + sb]
            qa = q[lo:lo + sb]
            ka = key[lo:lo + sb]
            slabs = []
            for t in range(sb):
                arg = jnp.where(s_iota <= t, cum_a[t:t + 1] - cum_a, -jnp.inf)
                slabs.append(jnp.exp(arg) * (qa[t:t + 1] * ka))
            z = jnp.concatenate(slabs, axis=0)
            p = _dot(z.astype(BF16), block_ones)
            w = p * jnp.concatenate([iv[lo:lo + sb]] * sb, axis=0)
            parts.append(_dot(sel, w.astype(BF16)))
        intra = jnp.concatenate(parts, axis=0)
        half = cs // 2
        lvl1 = off_diag(q, key, iv, cum, half, cs, 0, half)
        lvl2a = off_diag(q, key, iv, cum, sb, half, 0, sb)
        lvl2b = off_diag(q, key, iv, cum, half + sb, cs, half, half + sb)
        zeros = jnp.zeros((sb, hd), F32)
        intra = intra + jnp.concatenate([zeros, lvl2a, lvl1[0:sb], lvl1[sb:half] + lvl2b], axis=0)
        out = out + intra
        kc = key * jnp.exp(last - cum)
        upd = _tn_dot(iv.astype(BF16), kc.astype(BF16))
        st_ref[...] = st * jnp.exp(last) + jnp.where(head_eq, upd, 0.0)
        ms = _dot(out * out, block_mean, HIGHEST)
        on = out * lax.rsqrt(ms + EPS) * ng * _silu(g)
        o_ref[pl.ds(r0, cs), :] = on.astype(o_ref.dtype)
        return carry

    lax.fori_loop(0, tl // cs, chunk, 0)


def _hgrn_mixer(hg, lb, norm_g, tl):
    bsz, seq, d4 = hg.shape
    hd = d4 // 4
    dk = norm_g.shape[-1]
    ng = jnp.tile(norm_g, hd // dk).reshape(1, hd)
    return pl.pallas_call(
        functools.partial(_hgrn_kernel, tl=tl, hd=hd, dk=dk),
        grid=(bsz, seq // tl),
        in_specs=[pl.BlockSpec((None, tl, d4), lambda b, i: (b, i, 0)),
                  _const_spec((1, hd)), _const_spec((1, hd))],
        out_specs=pl.BlockSpec((None, tl, hd), lambda b, i: (b, i, 0)),
        out_shape=jax.ShapeDtypeStruct((bsz, seq, hd), BF16),
        scratch_shapes=[pltpu.VMEM((hd, hd), F32)],
        compiler_params=_params("arbitrary", "arbitrary"),
        name="hgrn_mixer",
    )(hg, lb.reshape(1, hd), ng)


def _moba_prep_kernel(x_ref, gq_ref, gk_ref, qt_ref, ka_ref, km_ref, vt_ref, *, nh, dh):
    i = pl.program_id(1)
    blk = MOBA_BLOCK
    hd = nh * dh

    @pl.when(i == 0)
    def _():
        km_ref[...] = jnp.zeros_like(km_ref)

    lane = lax.broadcasted_iota(I32, (blk, LANES - dh), 1)
    onehot = (lane == i).astype(F32)
    pad = jnp.zeros((blk, LANES - dh), F32)
    scale = dh ** -0.5
    for h in range(nh):
        qh = x_ref[:, h * dh:(h + 1) * dh]
        kh = x_ref[:, hd + h * dh:hd + (h + 1) * dh]
        vh = x_ref[:, 2 * hd + h * dh:2 * hd + (h + 1) * dh]
        qn = qh * lax.rsqrt(jnp.mean(qh * qh, axis=-1, keepdims=True) + EPS) * gq_ref[...] * scale
        kn = kh * lax.rsqrt(jnp.mean(kh * kh, axis=-1, keepdims=True) + EPS) * gk_ref[...]
        qt_ref[h] = jnp.concatenate([qn, pad], axis=1).T
        ka_ref[h] = jnp.concatenate([kn, onehot], axis=1).astype(BF16)
        kmean = jnp.mean(kn, axis=0, keepdims=True)
        km_ref[h, pl.ds(i, 1), :] = jnp.concatenate([kmean, jnp.zeros((1, LANES - dh), F32)], axis=1)
        vt_ref[h] = jnp.concatenate([vh, pad + 1.0], axis=1).T.astype(BF16)


def _moba_kernel(qt_ref, ka_ref, km_ref, vt_ref, o_ref, qa_scr, m_scr, acc_scr, *, nh, dh, nbp):
    i = pl.program_id(1)
    blk = MOBA_BLOCK
    r0 = pl.multiple_of(i * blk, blk)
    b_iota = lax.broadcasted_iota(I32, (nbp, blk), 0)
    key_i = lax.broadcasted_iota(I32, (blk, blk), 0)
    qry_i = lax.broadcasted_iota(I32, (blk, blk), 1)

    for h in range(nh):
        qt = qt_ref[h]
        gate = _dot(km_ref[h], qt, HIGHEST)
        work = jnp.where(b_iota < i, gate, -jnp.inf)
        chosen = jnp.zeros((nbp, blk), jnp.bool_)
        for _ in range(MOBA_TOPK):
            mx = jnp.max(work, axis=0, keepdims=True)
            first = jnp.min(jnp.where(work == mx, b_iota, nbp), axis=0, keepdims=True)
            hit = (b_iota == first) & (mx > -jnp.inf)
            chosen = chosen | hit
            work = jnp.where(hit, -jnp.inf, work)
        pen = jnp.where(chosen, 0.0, -MASK_BIG)
        qa_scr[h] = jnp.concatenate(
            [qt[0:dh], pen, jnp.zeros((LANES - dh - nbp, blk), F32)], axis=0).astype(BF16)
        s = _dot(ka_ref[h, pl.ds(r0, blk), :], qt.astype(BF16))
        s = jnp.where(key_i <= qry_i, s, -jnp.inf)
        m0 = jnp.max(s, axis=0, keepdims=True)
        m_scr[h] = m0
        acc_scr[h] = _dot(vt_ref[h, i], jnp.exp(s - m0).astype(BF16))

    def past(j, carry):
        k0 = pl.multiple_of(j * blk, blk)
        scores = [_dot(ka_ref[h, pl.ds(k0, blk), :], qa_scr[h]) for h in range(nh)]
        probs, alphas = [], []
        for h in range(nh):
            m_old = m_scr[h]
            m_new = jnp.maximum(m_old, jnp.max(scores[h], axis=0, keepdims=True))
            probs.append(jnp.exp(scores[h] - m_new).astype(BF16))
            alphas.append(jnp.exp(m_old - m_new))
            m_scr[h] = m_new
        for h in range(nh):
            acc_scr[h] = alphas[h] * acc_scr[h] + _dot(vt_ref[h, j], probs[h])
        return carry

    lax.fori_loop(0, i, past, 0)

    for h in range(nh):
        acc = acc_scr[h]
        o_ref[h * dh:(h + 1) * dh, :] = (acc[0:dh] / acc[dh:dh + 1]).astype(o_ref.dtype)


def _moba_mixer(mo, gq, gk):
    bsz, seq, d3 = mo.shape
    hd = d3 // 3
    dh = gq.shape[-1]
    nh = hd // dh
    blk = MOBA_BLOCK
    nblk = seq // blk
    nbp = -(-nblk // 8) * 8
    assert seq % blk == 0 and dh + nbp <= LANES
    qt, ka, km, vt = pl.pallas_call(
        functools.partial(_moba_prep_kernel, nh=nh, dh=dh),
        grid=(bsz, nblk),
        in_specs=[pl.BlockSpec((None, blk, d3), lambda b, i: (b, i, 0)),
                  _const_spec((1, dh)), _const_spec((1, dh))],
        out_specs=[pl.BlockSpec((None, nh, LANES, blk), lambda b, i: (b, 0, 0, i)),
                   pl.BlockSpec((None, nh, blk, LANES), lambda b, i: (b, 0, i, 0)),
                   pl.BlockSpec((None, nh, nbp, LANES), lambda b, i: (b, 0, 0, 0)),
                   pl.BlockSpec((None, nh, None, LANES, blk), lambda b, i: (b, 0, i, 0, 0))],
        out_shape=(jax.ShapeDtypeStruct((bsz, nh, LANES, seq), F32),
                   jax.ShapeDtypeStruct((bsz, nh, seq, LANES), BF16),
                   jax.ShapeDtypeStruct((bsz, nh, nbp, LANES), F32),
                   jax.ShapeDtypeStruct((bsz, nh, nblk, LANES, blk), BF16)),
        compiler_params=_params("arbitrary", "arbitrary"),
        name="moba_prep",
    )(mo, gq.reshape(1, dh), gk.reshape(1, dh))
    return pl.pallas_call(
        functools.partial(_moba_kernel, nh=nh, dh=dh, nbp=nbp),
        grid=(bsz, nblk),
        in_specs=[pl.BlockSpec((None, nh, LANES, blk), lambda b, i: (b, 0, 0, i)),
                  pl.BlockSpec((None, nh, seq, LANES), lambda b, i: (b, 0, 0, 0)),
                  pl.BlockSpec((None, nh, nbp, LANES), lambda b, i: (b, 0, 0, 0)),
                  pl.BlockSpec((None, nh, nblk, LANES, blk), lambda b, i: (b, 0, 0, 0, 0))],
        out_specs=pl.BlockSpec((None, hd, blk), lambda b, i: (b, 0, i)),
        out_shape=jax.ShapeDtypeStruct((bsz, hd, seq), BF16),
        scratch_shapes=[pltpu.VMEM((nh, LANES, blk), BF16),
                        pltpu.VMEM((nh, 1, blk), F32),
                        pltpu.VMEM((nh, LANES, blk), F32)],
        compiler_params=_params("arbitrary", "arbitrary"),
        name="moba_attn",
    )(qt, ka, km, vt)


def _merge_kernel(x_ref, ya_ref, hb_ref, mc_ref, gl_ref, mod_ref, g2_ref, who_ref, wmo_ref, wout_ref,
                  rwh_ref, rwl_ref, rb_ref,
                  x1_ref, h2_ref, idx_ref, wgt_ref, pos_ref, cnt_ref, *, tm, ne):
    first = (pl.program_id(0) == 0) & (pl.program_id(1) == 0)

    @pl.when(first)
    def _():
        cnt_ref[...] = jnp.zeros_like(cnt_ref)

    d = x_ref.shape[-1]
    yb = _dot(hb_ref[...], who_ref[...])
    yc = _tn_dot(mc_ref[...], wmo_ref[...])
    merged = (_sigmoid(gl_ref[:, 0:d].astype(F32)) * ya_ref[...].astype(F32)
              + _sigmoid(gl_ref[:, d:2 * d].astype(F32)) * yb
              + _sigmoid(gl_ref[:, 2 * d:3 * d].astype(F32)) * yc)
    x1 = x_ref[...] + mod_ref[2:3, :] * _dot(merged.astype(BF16), wout_ref[...])
    x1_ref[...] = x1
    ms = jnp.mean(x1 * x1, axis=-1, keepdims=True)
    h2 = x1 * lax.rsqrt(ms + EPS) * g2_ref[...]
    h2 = h2 * (1.0 + mod_ref[4:5, :]) + mod_ref[3:4, :]
    h2_ref[...] = h2
    hi = h2.astype(BF16)
    lo = (h2 - hi.astype(F32)).astype(BF16)
    logits = _nt_dot(rwh_ref[...], hi) + _nt_dot(rwh_ref[...], lo) + _nt_dot(rwl_ref[...], hi)
    scores = _sigmoid(logits)
    e_iota = lax.broadcasted_iota(I32, (ne, tm), 0)
    work = scores + rb_ref[...]
    picked = jnp.zeros((ne, tm), F32)
    hits, tops = [], []
    for _ in range(TOP_K):
        mx = jnp.max(work, axis=0, keepdims=True)
        first_e = jnp.min(jnp.where(work == mx, e_iota, ne), axis=0, keepdims=True)
        hit = e_iota == first_e
        hits.append((hit, first_e))
        tops.append(jnp.sum(jnp.where(hit, scores, 0.0), axis=0, keepdims=True))
        work = jnp.where(hit, -jnp.inf, work)
        picked = jnp.where(hit, 1.0, picked)
    total = functools.reduce(lambda a, b: a + b, tops)
    upper = (lax.broadcasted_iota(I32, (tm, tm), 0) < lax.broadcasted_iota(I32, (tm, tm), 1)).astype(BF16)
    rank = _dot(picked.astype(BF16), upper) + cnt_ref[:, 0:1]
    zero_row = jnp.zeros((1, tm), F32)
    idx_rows, w_rows, pos_rows = [], [], []
    for (hit, first_e), top in zip(hits, tops):
        idx_rows.append(first_e)
        w_rows.append(top / total * ROUTED_SCALE)
        pos_rows.append(jnp.sum(jnp.where(hit, rank, 0.0), axis=0, keepdims=True))
    pad = 8 - TOP_K
    idx_ref[...] = jnp.concatenate(idx_rows + [jnp.zeros((1, tm), I32)] * pad, axis=0)
    wgt_ref[...] = jnp.concatenate(w_rows + [zero_row] * pad, axis=0)
    pos_ref[...] = jnp.concatenate(pos_rows + [zero_row] * pad, axis=0).astype(I32)
    cnt_ref[...] = cnt_ref[...] + jnp.sum(picked, axis=1, keepdims=True)


def _merge_router(x, ya, hb, mc, gl, mod, norm2_g, who, wmo, wout, rwh, rwl, rbias, tm):
    bsz, seq, d = x.shape
    ne = rwh.shape[0]
    nt = seq // tm
    n_tok = bsz * seq
    row = lambda n: pl.BlockSpec((None, tm, n), lambda b, i: (b, i, 0))
    col = pl.BlockSpec((8, tm), lambda b, i: (0, b * nt + i))
    return pl.pallas_call(
        functools.partial(_merge_kernel, tm=tm, ne=ne),
        grid=(bsz, nt),
        in_specs=[row(d), row(d), row(hb.shape[-1]),
                  pl.BlockSpec((None, mc.shape[1], tm), lambda b, i: (b, 0, i)), row(gl.shape[-1]),
                  pl.BlockSpec((None, N_MOD, d), lambda b, i: (b, 0, 0)),
                  _const_spec((1, d)), _const_spec(who.shape), _const_spec(wmo.shape),
                  _const_spec(wout.shape), _const_spec(rwh.shape), _const_spec(rwl.shape),
                  _const_spec((ne, 1))],
        out_specs=[row(d), row(d), col, col, col,
                   pl.BlockSpec((ne, LANES), lambda b, i: (0, 0))],
        out_shape=(jax.ShapeDtypeStruct((bsz, seq, d), F32),
                   jax.ShapeDtypeStruct((bsz, seq, d), F32),
                   jax.ShapeDtypeStruct((8, n_tok), I32),
                   jax.ShapeDtypeStruct((8, n_tok), F32),
                   jax.ShapeDtypeStruct((8, n_tok), I32),
                   jax.ShapeDtypeStruct((ne, LANES), F32)),
        compiler_params=_params("arbitrary", "arbitrary"),
        name="merge_router",
    )(x, ya, hb, mc, gl, mod, norm2_g.reshape(1, d), who, wmo, wout, rwh, rwl, rbias.reshape(ne, 1))


def _dispatch_kernel(dest_ref, h_ref, xs_ref, sem, *, tm):
    def body(t, carry):
        for k in range(TOP_K):
            pltpu.make_async_copy(h_ref.at[pl.ds(t, 1), :],
                                  xs_ref.at[pl.ds(dest_ref[k, t], 1), :], sem).start()
        return carry

    lax.fori_loop(0, tm, body, 0)
    for k in range(TOP_K):
        pltpu.make_async_copy(h_ref, xs_ref.at[pl.ds(0, tm), :], sem).wait()


def _dispatch(dest, h2, n_rows, tm):
    n_tok, d = h2.shape
    return pl.pallas_call(
        functools.partial(_dispatch_kernel, tm=tm),
        grid=(n_tok // tm,),
        in_specs=[pl.BlockSpec((8, tm), lambda i: (0, i), memory_space=pltpu.SMEM),
                  pl.BlockSpec((tm, d), lambda i: (i, 0))],
        out_specs=pl.BlockSpec(memory_space=pl.ANY),
        out_shape=jax.ShapeDtypeStruct((n_rows, d), F32),
        scratch_shapes=[pltpu.SemaphoreType.DMA],
        compiler_params=_params("arbitrary"),
        name="moe_dispatch",
    )(dest, h2)


def _expert_kernel(be_ref, nv_ref, x_ref, wg_ref, wu_ref, wd_ref, y_ref, *, rb):
    nv = nv_ref[pl.program_id(0)]

    @pl.when(nv > 0)
    def _():
        rows = lax.broadcasted_iota(I32, (rb, 1), 0)
        x = jnp.where(rows < nv, x_ref[...], 0.0).astype(BF16)
        g = _dot(x, wg_ref[...])
        u = _dot(x, wu_ref[...])
        y_ref[...] = _dot((_silu(g) * u).astype(BF16), wd_ref[...])


def _experts(block_expert, block_rows, xs, wg, wu, wd, rb):
    n_rows, d = xs.shape
    n_blocks = n_rows // rb
    de = wg.shape[-1]
    live = lambda i, be, nv: jnp.where(nv[i] > 0, i, n_blocks - 1)
    grid_spec = pltpu.PrefetchScalarGridSpec(
        num_scalar_prefetch=2,
        grid=(n_blocks,),
        in_specs=[pl.BlockSpec((rb, d), lambda i, be, nv: (live(i, be, nv), 0)),
                  pl.BlockSpec((None, d, de), lambda i, be, nv: (be[i], 0, 0)),
                  pl.BlockSpec((None, d, de), lambda i, be, nv: (be[i], 0, 0)),
                  pl.BlockSpec((None, de, d), lambda i, be, nv: (be[i], 0, 0))],
        out_specs=pl.BlockSpec((rb, d), lambda i, be, nv: (live(i, be, nv), 0)),
    )
    return pl.pallas_call(
        functools.partial(_expert_kernel, rb=rb),
        grid_spec=grid_spec,
        out_shape=jax.ShapeDtypeStruct((n_rows, d), F32),
        compiler_params=_params("arbitrary"),
        name="moe_experts",
    )(block_expert, block_rows, xs, wg, wu, wd)


def _combine_kernel(dest_ref, wgt_ref, x1_ref, h2_ref, mod_ref, sg_ref, su_ref, sd_ref, ys_ref,
                    o_ref, gbuf, sem, *, tm):
    def body(t, carry):
        for k in range(TOP_K):
            pltpu.make_async_copy(ys_ref.at[pl.ds(dest_ref[k, t], 1), :],
                                  gbuf.at[k, pl.ds(t, 1), :], sem).start()
        return carry

    lax.fori_loop(0, tm, body, 0)
    hb = h2_ref[...].astype(BF16)
    hid = _silu(_dot(hb, sg_ref[...])) * _dot(hb, su_ref[...])
    y = _dot(hid.astype(BF16), sd_ref[...])
    wcol = wgt_ref[...].T
    for k in range(TOP_K):
        pltpu.make_async_copy(ys_ref.at[pl.ds(0, tm), :], gbuf.at[k], sem).wait()
    for k in range(TOP_K):
        y = y + gbuf[k] * wcol[:, k:k + 1]
    o_ref[...] = x1_ref[...] + mod_ref[5:6, :] * y


def _combine(dest, wgt, x1, h2, mod, sg, su, sd, ys, tm):
    bsz, seq, d = x1.shape
    nt = seq // tm
    row = pl.BlockSpec((None, tm, d), lambda b, i: (b, i, 0))
    return pl.pallas_call(
        functools.partial(_combine_kernel, tm=tm),
        grid=(bsz, nt),
        in_specs=[pl.BlockSpec((8, tm), lambda b, i: (0, b * nt + i), memory_space=pltpu.SMEM),
                  pl.BlockSpec((8, tm), lambda b, i: (0, b * nt + i)),
                  row, row,
                  pl.BlockSpec((None, N_MOD, d), lambda b, i: (b, 0, 0)),
                  _const_spec(sg.shape), _const_spec(su.shape), _const_spec(sd.shape),
                  pl.BlockSpec(memory_space=pl.ANY)],
        out_specs=row,
        out_shape=jax.ShapeDtypeStruct((bsz, seq, d), F32),
        scratch_shapes=[pltpu.VMEM((TOP_K, tm, d), F32), pltpu.SemaphoreType.DMA],
        compiler_params=_params("arbitrary", "arbitrary"),
        name="moe_combine",
    )(dest, wgt, x1, h2, mod, sg, su, sd, ys)


def _dest_kernel(ps_ref, idx_ref, pos_ref, o_ref, *, ne):
    idx = idx_ref[...]
    acc = pos_ref[...]
    for e in range(ne):
        acc = acc + jnp.where(idx == e, ps_ref[e], 0)
    o_ref[...] = acc


def _routing_tables(idx, pos, counts, n_tok, ne, rb):
    n_assign = n_tok * TOP_K
    n_blocks = -(-n_assign // rb) + ne
    padded = (counts + rb - 1) // rb * rb
    pend = jnp.cumsum(padded)
    pstart = (pend - padded).astype(I32)
    tt = min(8192, n_tok)
    col = pl.BlockSpec((8, tt), lambda i: (0, i))
    dest = pl.pallas_call(
        functools.partial(_dest_kernel, ne=ne),
        grid=(n_tok // tt,),
        in_specs=[pl.BlockSpec(memory_space=pltpu.SMEM), col, col],
        out_specs=col,
        out_shape=jax.ShapeDtypeStruct((8, n_tok), I32),
        compiler_params=_params("arbitrary"),
        name="moe_dest",
    )(pstart, idx, pos)
    bstart = jnp.arange(n_blocks, dtype=I32) * rb
    bexp = jnp.minimum(jnp.sum((pend[None, :] <= bstart[:, None]).astype(I32), axis=1), ne - 1)
    brows = jnp.clip(counts[bexp] - (bstart - pstart[bexp]), 0, rb).astype(I32)
    return dest, bexp, brows, n_blocks


def _block_diag_in(bbt, g):
    gw, n = bbt.shape
    p = n // g
    rows = jnp.tile(bbt, (g, 1))
    rg = jnp.arange(g * gw)[:, None] // gw
    cg = jnp.arange(n)[None, :] // p
    return jnp.where(rg == cg, rows, 0.0)


def _block_diag_out(c, sign):
    g, gw, p = c.shape
    cols = jnp.tile(c.transpose(0, 2, 1).reshape(g * p, gw), (1, g))
    rg = jnp.arange(g * p)[:, None] // p
    cg = jnp.arange(g * gw)[None, :] // gw
    return jnp.where(rg == cg, sign * cols, 0.0)


def kernel(x, c, ada_w, ada_b, norm1_g, w_in, ssm_log_dt, ssm_lambda_re, ssm_lambda_im, ssm_b_re,
           ssm_b_im, ssm_c_re, ssm_c_im, ssm_d, ssm_w_glu, hgrn_lb_logits, hgrn_norm_g, hgrn_w_o,
           moba_q_norm_g, moba_k_norm_g, moba_w_o, w_out, norm2_g, router_w, router_bias,
           exp_w_gate, exp_w_up, exp_w_down, shared_w_gate, shared_w_up, shared_w_down):
    bsz, seq, d = x.shape
    depth = w_in.shape[0]
    n_tok = bsz * seq
    g = ssm_lambda_re.shape[1]
    d_ssm = ssm_d.shape[-1]
    d_hk = hgrn_lb_logits.shape[-1]
    d_moba = moba_w_o.shape[1]
    ne = router_w.shape[-1]
    widths = (d_ssm, 4 * d_hk, 3 * d_moba, 3 * d)
    tm = min(256, seq)
    rb = 512

    mod_all = _ada_mod(c, ada_w, ada_b).reshape(depth, bsz, N_MOD, d)
    abar, bbr, bbi, lower = _prep_params(ssm_log_dt, ssm_lambda_re, ssm_lambda_im, ssm_b_re, ssm_b_im,
                                         hgrn_lb_logits)

    for l in range(depth):
        mod = mod_all[l]
        u, hg, mo, gl = _inproj(x, mod, norm1_g[l], w_in[l].astype(BF16), widths, tm)
        wb = jnp.concatenate([_block_diag_in(bbr[l], g), _block_diag_in(bbi[l], g)], axis=1).astype(BF16)
        wc = jnp.concatenate([_block_diag_out(ssm_c_re[l], 1.0), _block_diag_out(ssm_c_im[l], -1.0)],
                             axis=0).astype(BF16)
        ya = _s5_mixer(u, wb, wc, abar[l], ssm_d[l], ssm_w_glu[l].astype(BF16), min(128, seq))
        hb = _hgrn_mixer(hg, lower[l], hgrn_norm_g[l], min(256, seq))
        mc = _moba_mixer(mo, moba_q_norm_g[l], moba_k_norm_g[l])
        rwt = router_w[l].T
        rwh = rwt.astype(BF16)
        rwl = (rwt - rwh.astype(F32)).astype(BF16)
        x1, h2, idx, wgt, pos, cnt = _merge_router(
            x, ya, hb, mc, gl, mod, norm2_g[l], hgrn_w_o[l].astype(BF16), moba_w_o[l].astype(BF16),
            w_out[l].astype(BF16), rwh, rwl, router_bias[l], tm)
        counts = cnt[:, 0].astype(I32)
        dest, bexp, brows, n_blocks = _routing_tables(idx, pos, counts, n_tok, ne, rb)
        xs = _dispatch(dest, h2.reshape(n_tok, d), n_blocks * rb, tm)
        ys = _experts(bexp, brows, xs, exp_w_gate[l].astype(BF16), exp_w_up[l].astype(BF16),
                      exp_w_down[l].astype(BF16), rb)
        x = _combine(dest, wgt, x1, h2, mod, shared_w_gate[l].astype(BF16),
                     shared_w_up[l].astype(BF16), shared_w_down[l].astype(BF16), ys, tm)
    return x
```

```python
import functools

import jax
import jax.numpy as jnp
from jax import lax
from jax.experimental import pallas as pl
from jax.experimental.pallas import tpu as pltpu

F32 = jnp.float32
BF16 = jnp.bfloat16
I32 = jnp.int32
HIGHEST = lax.Precision.HIGHEST

N_MOD = 6
EPS = 1e-6
MOBA_BLOCK = 256
MOBA_TOPK = 3
TOP_K = 6
ROUTED_SCALE = 2.5

LANES = 128
HGRN_CHUNK = 64
HGRN_SUB = 16
S5_ROW_PAD = 8
MASK_BIG = 30000.0
VMEM_LIMIT = 56 * 1024 * 1024


def _nt_dot(a, b, precision=None):
    return lax.dot_general(a, b, (((1,), (1,)), ((), ())), preferred_element_type=F32,
                           precision=precision)


def _tn_dot(a, b, precision=None):
    return lax.dot_general(a, b, (((0,), (0,)), ((), ())), preferred_element_type=F32,
                           precision=precision)


def _dot(a, b, precision=None):
    return jnp.dot(a, b, preferred_element_type=F32, precision=precision)


def _bf16_terms(x, n):
    terms = []
    for _ in range(n):
        t = x.astype(BF16)
        terms.append(t)
        x = x - t.astype(F32)
    return terms


def _dot_exact_lhs(a, b, n):
    ab = a.astype(BF16)
    return functools.reduce(lambda p, q: p + q, [_dot(ab, t) for t in _bf16_terms(b, n)])


def _dot_exact_rhs(a, b, n):
    bb = b.astype(BF16)
    return functools.reduce(lambda p, q: p + q, [_dot(t, bb) for t in _bf16_terms(a, n)])


def _sigmoid(x):
    return 1.0 / (1.0 + jnp.exp(-x))


def _silu(x):
    return x * _sigmoid(x)


def _params(*sem):
    return pltpu.CompilerParams(dimension_semantics=sem, vmem_limit_bytes=VMEM_LIMIT)


def _const_spec(shape):
    nd = len(shape)
    return pl.BlockSpec(shape, lambda *_: (0,) * nd, pipeline_mode=pl.Buffered(1))


def _ada_kernel(c_ref, w_ref, b_ref, o_ref):
    c = c_ref[...]
    o_ref[...] = _dot(_silu(c), w_ref[...], HIGHEST) + b_ref[...]


def _ada_mod(c, ada_w, ada_b):
    depth, d, n = ada_w.shape
    bsz = c.shape[0]
    tn = n // 4
    return pl.pallas_call(
        _ada_kernel,
        grid=(depth, n // tn),
        in_specs=[pl.BlockSpec((bsz, d), lambda l, j: (0, 0)),
                  pl.BlockSpec((None, d, tn), lambda l, j: (l, 0, j)),
                  pl.BlockSpec((None, 1, tn), lambda l, j: (l, 0, j))],
        out_specs=pl.BlockSpec((None, bsz, tn), lambda l, j: (l, 0, j)),
        out_shape=jax.ShapeDtypeStruct((depth, bsz, n), F32),
        compiler_params=_params("arbitrary", "arbitrary"),
        name="ada_mod",
    )(c, ada_w, ada_b.reshape(depth, 1, n))


def _prep_kernel(ldt_ref, lr_ref, li_ref, btr_ref, bti_ref, lbl_ref,
                 abar_ref, bbr_ref, bbi_ref, lb_ref):
    depth = ldt_ref.shape[0]
    for l in range(depth):
        dt = jnp.exp(ldt_ref[l])
        lr = lr_ref[l]
        li = li_ref[l]
        mag = jnp.exp(lr * dt)
        ar = mag * jnp.cos(li * dt)
        ai = mag * jnp.sin(li * dt)
        inv = 1.0 / (lr * lr + li * li)
        cr = ((ar - 1.0) * lr + ai * li) * inv
        ci = (ai * lr - (ar - 1.0) * li) * inv
        abar_ref[l, 0:1, :] = ar
        abar_ref[l, 1:2, :] = ai
        btr = btr_ref[l]
        bti = bti_ref[l]
        bbr_ref[l] = cr * btr - ci * bti
        bbi_ref[l] = cr * bti + ci * btr
    rows = [lbl_ref[l:l + 1, :] for l in range(depth)]
    mx = functools.reduce(jnp.maximum, rows)
    ex = [jnp.exp(r - mx) for r in rows]
    den = functools.reduce(lambda a, b: a + b, ex)
    run = ex[0] / den
    first = run
    lb_ref[0:1, :] = run - first
    for l in range(1, depth):
        run = run + ex[l] / den
        lb_ref[l:l + 1, :] = run - first


def _prep_params(log_dt, lam_re, lam_im, b_re, b_im, lb_logits):
    depth, g, p = lam_re.shape
    gw = b_re.shape[-1]
    n = g * p
    hk = lb_logits.shape[-1]
    ldt = jnp.repeat(log_dt, p, axis=1).reshape(depth, 1, n)
    btr = b_re.transpose(0, 3, 1, 2).reshape(depth, gw, n)
    bti = b_im.transpose(0, 3, 1, 2).reshape(depth, gw, n)
    return pl.pallas_call(
        _prep_kernel,
        out_shape=(jax.ShapeDtypeStruct((depth, 2, n), F32),
                   jax.ShapeDtypeStruct((depth, gw, n), F32),
                   jax.ShapeDtypeStruct((depth, gw, n), F32),
                   jax.ShapeDtypeStruct((depth, hk), F32)),
        name="param_prep",
    )(ldt, lam_re.reshape(depth, 1, n), lam_im.reshape(depth, 1, n), btr, bti, lb_logits)


def _inproj_kernel(x_ref, mod_ref, g_ref, w_ref, u_ref, hg_ref, mo_ref, gl_ref, *, widths):
    x = x_ref[...]
    ms = jnp.mean(x * x, axis=-1, keepdims=True)
    h = x * lax.rsqrt(ms + EPS) * g_ref[...]
    h = h * (1.0 + mod_ref[1:2, :]) + mod_ref[0:1, :]
    hb = h.astype(BF16)
    d_ssm, d_hgrn, d_moba, d_gate = widths
    o0 = 0
    u_ref[...] = _dot(hb, w_ref[:, o0:o0 + d_ssm])
    o0 += d_ssm
    hg_ref[...] = _dot(hb, w_ref[:, o0:o0 + d_hgrn])
    o0 += d_hgrn
    mo_ref[...] = _dot(hb, w_ref[:, o0:o0 + d_moba])
    o0 += d_moba
    step = 1024
    for j in range(0, d_gate, step):
        gl_ref[:, j:j + step] = _dot(hb, w_ref[:, o0 + j:o0 + j + step]).astype(BF16)


def _inproj(x, mod, norm_g, w_in_bf, widths, tm):
    bsz, seq, d = x.shape
    d_ssm, d_hgrn, d_moba, d_gate = widths
    d_in = w_in_bf.shape[1]
    row = lambda n: pl.BlockSpec((None, tm, n), lambda b, i: (b, i, 0))
    return pl.pallas_call(
        functools.partial(_inproj_kernel, widths=widths),
        grid=(bsz, seq // tm),
        in_specs=[row(d),
                  pl.BlockSpec((None, N_MOD, d), lambda b, i: (b, 0, 0)),
                  _const_spec((1, d)),
                  _const_spec((d, d_in))],
        out_specs=[row(d_ssm), row(d_hgrn), row(d_moba), row(d_gate)],
        out_shape=(jax.ShapeDtypeStruct((bsz, seq, d_ssm), F32),
                   jax.ShapeDtypeStruct((bsz, seq, d_hgrn), F32),
                   jax.ShapeDtypeStruct((bsz, seq, d_moba), F32),
                   jax.ShapeDtypeStruct((bsz, seq, d_gate), BF16)),
        compiler_params=_params("arbitrary", "arbitrary"),
        name="in_proj",
    )(x, mod, norm_g.reshape(1, d), w_in_bf)


def _s5_kernel(u_ref, wb_ref, wc_ref, abar_ref, d_ref, wglu_ref, o_ref, bu_ref, st_ref,
               *, nb, tl, ns, unroll):
    nc = ns // LANES
    rs = tl + S5_ROW_PAD

    @pl.when(pl.program_id(0) == 0)
    def _():
        st_ref[...] = jnp.zeros_like(st_ref)

    for b in range(nb):
        bu = _dot(u_ref[b].astype(BF16), wb_ref[...])
        for c in range(2 * nc):
            bu_ref[c, b * rs:b * rs + tl, :] = bu[:, c * LANES:(c + 1) * LANES]

    ar = [jnp.broadcast_to(abar_ref[0:1, c * LANES:(c + 1) * LANES], (nb, LANES)) for c in range(nc)]
    ai = [jnp.broadcast_to(abar_ref[1:2, c * LANES:(c + 1) * LANES], (nb, LANES)) for c in range(nc)]

    def outer(j, carry):
        sr, si = list(carry[:nc]), list(carry[nc:])
        for k in range(unroll):
            t = j * unroll + k
            for c in range(nc):
                br = bu_ref[c, pl.ds(t, nb, stride=rs), :]
                bi = bu_ref[nc + c, pl.ds(t, nb, stride=rs), :]
                nr = ar[c] * sr[c] - ai[c] * si[c] + br
                ni = ar[c] * si[c] + ai[c] * sr[c] + bi
                bu_ref[c, pl.ds(t, nb, stride=rs), :] = nr
                bu_ref[nc + c, pl.ds(t, nb, stride=rs), :] = ni
                sr[c], si[c] = nr, ni
        return tuple(sr) + tuple(si)

    init = tuple(st_ref[c] for c in range(2 * nc))
    fin = lax.fori_loop(0, tl // unroll, outer, init)
    for c in range(2 * nc):
        st_ref[c] = fin[c]

    dm = o_ref.shape[-1]
    for b in range(nb):
        s = jnp.concatenate([bu_ref[c, b * rs:b * rs + tl, :] for c in range(2 * nc)], axis=1)
        y = _dot(s.astype(BF16), wc_ref[...]) + d_ref[...] * u_ref[b]
        z = _dot(jax.nn.gelu(y).astype(BF16), wglu_ref[...])
        o_ref[b] = (z[:, :dm] * _sigmoid(z[:, dm:])).astype(o_ref.dtype)


def _s5_mixer(u, wb, wc, abar, d_skip, wglu, tl):
    bsz, seq, d_ssm = u.shape
    ns = abar.shape[-1]
    dm = wglu.shape[1] // 2
    return pl.pallas_call(
        functools.partial(_s5_kernel, nb=bsz, tl=tl, ns=ns, unroll=8),
        grid=(seq // tl,),
        in_specs=[pl.BlockSpec((bsz, tl, d_ssm), lambda i: (0, i, 0)),
                  _const_spec(wb.shape), _const_spec(wc.shape), _const_spec(abar.shape),
                  _const_spec((1, d_ssm)), _const_spec(wglu.shape)],
        out_specs=pl.BlockSpec((bsz, tl, dm), lambda i: (0, i, 0)),
        out_shape=jax.ShapeDtypeStruct((bsz, seq, dm), BF16),
        scratch_shapes=[pltpu.VMEM((2 * ns // LANES, bsz * (tl + S5_ROW_PAD), LANES), F32),
                        pltpu.VMEM((2 * ns // LANES, bsz, LANES), F32)],
        compiler_params=_params("arbitrary"),
        name="s5_mixer",
    )(u, wb, wc, abar, d_skip.reshape(1, d_ssm), wglu)


def _hgrn_kernel(x_ref, lb_ref, ng_ref, o_ref, st_ref, *, tl, hd, dk):
    cs, sb = HGRN_CHUNK, HGRN_SUB

    @pl.when(pl.program_id(1) == 0)
    def _():
        st_ref[...] = jnp.zeros_like(st_ref)

    def iota(shape, dim):
        return lax.broadcasted_iota(I32, shape, dim)

    tril = (iota((cs, cs), 0) >= iota((cs, cs), 1)).astype(F32)
    head_eq = (iota((hd, hd), 0) // dk) == (iota((hd, hd), 1) // dk)
    block_ones = head_eq.astype(BF16)
    block_mean = head_eq.astype(F32) * (1.0 / dk)
    sel = ((iota((sb, sb * sb), 1) // sb) == iota((sb, sb * sb), 0)).astype(BF16)
    s_iota = iota((sb, hd), 0)
    lane_head = iota((1, hd), 1) // dk
    n_heads = hd // dk
    lb = lb_ref[...]
    ng = ng_ref[...]

    def head_copies(a):
        return jnp.concatenate([jnp.where(lane_head == h, a, 0.0) for h in range(n_heads)], axis=0)

    def off_diag_scores(q, key, cum, t0, t1, s0, s1):
        r = cum[s1 - 1:s1, :]
        qt = q[t0:t1] * jnp.exp(cum[t0:t1] - r)
        ks = key[s0:s1] * jnp.exp(r - cum[s0:s1])
        return _nt_dot(qt.astype(BF16), head_copies(ks).astype(BF16))

    nch = tl // cs
    half = cs // 2
    spans = ((half, cs, 0, half), (sb, half, 0, sb), (half + sb, cs, half, half + sb))
    pre = []
    for c in range(nch):
        r0 = c * cs
        q = x_ref[r0:r0 + cs, 0:hd]
        fr = x_ref[r0:r0 + cs, hd:2 * hd]
        iv = x_ref[r0:r0 + cs, 2 * hd:3 * hd]
        forget = lb + (1.0 - lb) * _sigmoid(fr)
        key = 1.0 - forget
        cum = _dot_exact_lhs(tril, jnp.log(forget), 3)
        pre.append((q, key, iv, cum))
    heads_sum = []
    for q, key, iv, cum in pre:
        for a in range(cs // sb):
            lo = a * sb
            cum_a, qa, ka = cum[lo:lo + sb], q[lo:lo + sb], key[lo:lo + sb]
            slabs = []
            for t in range(sb):
                arg = jnp.where(s_iota <= t, cum_a[t:t + 1] - cum_a, -jnp.inf)
                slabs.append(jnp.exp(arg) * (qa[t:t + 1] * ka))
            heads_sum.append(_dot(jnp.concatenate(slabs, axis=0).astype(BF16), block_ones))
    off_scores = [[off_diag_scores(q, key, cum, *sp) for sp in spans] for q, key, iv, cum in pre]
    intras = []
    for c, (q, key, iv, cum) in enumerate(pre):
        parts = []
        for a in range(cs // sb):
            lo = a * sb
            w = heads_sum[c * (cs // sb) + a] * jnp.concatenate([iv[lo:lo + sb]] * sb, axis=0)
            parts.append(_dot(sel, w.astype(BF16)))
        lvl1, lvl2a, lvl2b = [
            _dot(sc.astype(BF16), head_copies(iv[sp[2]:sp[3]]).astype(BF16))
            for sc, sp in zip(off_scores[c], spans)]
        zeros = jnp.zeros((sb, hd), F32)
        intras.append(jnp.concatenate(parts, axis=0)
                      + jnp.concatenate([zeros, lvl2a, lvl1[0:sb], lvl1[sb:half] + lvl2b], axis=0))
    upds, decays, qcs = [], [], []
    for q, key, iv, cum in pre:
        last = cum[cs - 1:cs, :]
        kc = key * jnp.exp(last - cum)
        upds.append(jnp.where(head_eq, _tn_dot(iv.astype(BF16), kc.astype(BF16)), 0.0))
        decays.append(jnp.exp(last))
        qcs.append((q * jnp.exp(cum)).astype(BF16))
    st = st_ref[...]
    for c in range(nch):
        out = intras[c] + _nt_dot(qcs[c], st.astype(BF16))
        st = st * decays[c] + upds[c]
        ms = _dot_exact_rhs(out * out, block_mean, 2)
        g = x_ref[c * cs:(c + 1) * cs, 3 * hd:4 * hd]
        o_ref[c * cs:(c + 1) * cs, :] = (out * lax.rsqrt(ms + EPS) * ng * _silu(g)).astype(o_ref.dtype)
    st_ref[...] = st


def _hgrn_mixer(hg, lb, norm_g, tl):
    bsz, seq, d4 = hg.shape
    hd = d4 // 4
    dk = norm_g.shape[-1]
    ng = jnp.tile(norm_g, hd // dk).reshape(1, hd)
    return pl.pallas_call(
        functools.partial(_hgrn_kernel, tl=tl, hd=hd, dk=dk),
        grid=(bsz, seq // tl),
        in_specs=[pl.BlockSpec((None, tl, d4), lambda b, i: (b, i, 0)),
                  _const_spec((1, hd)), _const_spec((1, hd))],
        out_specs=pl.BlockSpec((None, tl, hd), lambda b, i: (b, i, 0)),
        out_shape=jax.ShapeDtypeStruct((bsz, seq, hd), BF16),
        scratch_shapes=[pltpu.VMEM((hd, hd), F32)],
        compiler_params=_params("arbitrary", "arbitrary"),
        name="hgrn_mixer",
    )(hg, lb.reshape(1, hd), ng)


def _moba_prep_kernel(x_ref, gq_ref, gk_ref, qt_ref, ka_ref, km_ref, vt_ref, *, nh, dh):
    i = pl.program_id(1)
    blk = MOBA_BLOCK
    hd = nh * dh

    @pl.when(i == 0)
    def _():
        km_ref[...] = jnp.zeros_like(km_ref)

    lane = lax.broadcasted_iota(I32, (blk, LANES - dh), 1)
    onehot = (lane == i).astype(F32)
    pad = jnp.zeros((blk, LANES - dh), F32)
    scale = dh ** -0.5
    for h in range(nh):
        qh = x_ref[:, h * dh:(h + 1) * dh]
        kh = x_ref[:, hd + h * dh:hd + (h + 1) * dh]
        vh = x_ref[:, 2 * hd + h * dh:2 * hd + (h + 1) * dh]
        qn = qh * lax.rsqrt(jnp.mean(qh * qh, axis=-1, keepdims=True) + EPS) * gq_ref[...] * scale
        kn = kh * lax.rsqrt(jnp.mean(kh * kh, axis=-1, keepdims=True) + EPS) * gk_ref[...]
        qt_ref[h] = jnp.concatenate([qn, pad], axis=1).T
        ka_ref[h] = jnp.concatenate([kn, onehot], axis=1).astype(BF16)
        kmean = jnp.mean(kn, axis=0, keepdims=True)
        km_ref[h, pl.ds(i, 1), :] = jnp.concatenate([kmean, jnp.zeros((1, LANES - dh), F32)], axis=1)
        vt_ref[h] = jnp.concatenate([vh, pad + 1.0], axis=1).T.astype(BF16)


def _moba_kernel(qt_ref, ka_ref, km_ref, vt_ref, o_ref, qa_scr, m_scr, acc_scr, *, nh, dh, nbp):
    i = pl.program_id(1)
    blk = MOBA_BLOCK
    r0 = pl.multiple_of(i * blk, blk)
    b_iota = lax.broadcasted_iota(I32, (nbp, blk), 0)
    key_i = lax.broadcasted_iota(I32, (blk, blk), 0)
    qry_i = lax.broadcasted_iota(I32, (blk, blk), 1)

    for h in range(nh):
        qt = qt_ref[h]
        gate = _dot(km_ref[h], qt, HIGHEST)
        work = jnp.where(b_iota < i, gate, -jnp.inf)
        chosen = jnp.zeros((nbp, blk), jnp.bool_)
        for _ in range(MOBA_TOPK):
            mx = jnp.max(work, axis=0, keepdims=True)
            first = jnp.min(jnp.where(work == mx, b_iota, nbp), axis=0, keepdims=True)
            hit = (b_iota == first) & (mx > -jnp.inf)
            chosen = chosen | hit
            work = jnp.where(hit, -jnp.inf, work)
        pen = jnp.where(chosen, 0.0, -MASK_BIG)
        qa_scr[h] = jnp.concatenate(
            [qt[0:dh], pen, jnp.zeros((LANES - dh - nbp, blk), F32)], axis=0).astype(BF16)
        s = _dot(ka_ref[h, pl.ds(r0, blk), :], qt.astype(BF16))
        s = jnp.where(key_i <= qry_i, s, -jnp.inf)
        m0 = jnp.max(s, axis=0, keepdims=True)
        m_scr[h] = m0
        acc_scr[h] = _dot(vt_ref[h, i], jnp.exp(s - m0).astype(BF16))

    def past_blocks(js):
        scores = [[_dot(ka_ref[h, pl.ds(pl.multiple_of(j * blk, blk), blk), :], qa_scr[h]) for j in js]
                  for h in range(nh)]
        probs, alphas = [], []
        for h in range(nh):
            m_old = m_scr[h]
            m_new = m_old
            for s in scores[h]:
                m_new = jnp.maximum(m_new, jnp.max(s, axis=0, keepdims=True))
            probs.append([jnp.exp(s - m_new).astype(BF16) for s in scores[h]])
            alphas.append(jnp.exp(m_old - m_new))
            m_scr[h] = m_new
        for h in range(nh):
            acc = alphas[h] * acc_scr[h]
            for j, p in zip(js, probs[h]):
                acc = acc + _dot(vt_ref[h, j], p)
            acc_scr[h] = acc

    def past_pair(jj, carry):
        past_blocks((2 * jj, 2 * jj + 1))
        return carry

    lax.fori_loop(0, i // 2, past_pair, 0)

    @pl.when(i % 2 == 1)
    def _():
        past_blocks((i - 1,))

    for h in range(nh):
        acc = acc_scr[h]
        o_ref[h * dh:(h + 1) * dh, :] = (acc[0:dh] / acc[dh:dh + 1]).astype(o_ref.dtype)


def _moba_mixer(mo, gq, gk):
    bsz, seq, d3 = mo.shape
    hd = d3 // 3
    dh = gq.shape[-1]
    nh = hd // dh
    blk = MOBA_BLOCK
    nblk = seq // blk
    nbp = -(-nblk // 8) * 8
    assert seq % blk == 0 and dh + nbp <= LANES
    qt, ka, km, vt = pl.pallas_call(
        functools.partial(_moba_prep_kernel, nh=nh, dh=dh),
        grid=(bsz, nblk),
        in_specs=[pl.BlockSpec((None, blk, d3), lambda b, i: (b, i, 0)),
                  _const_spec((1, dh)), _const_spec((1, dh))],
        out_specs=[pl.BlockSpec((None, nh, LANES, blk), lambda b, i: (b, 0, 0, i)),
                   pl.BlockSpec((None, nh, blk, LANES), lambda b, i: (b, 0, i, 0)),
                   pl.BlockSpec((None, nh, nbp, LANES), lambda b, i: (b, 0, 0, 0)),
                   pl.BlockSpec((None, nh, None, LANES, blk), lambda b, i: (b, 0, i, 0, 0))],
        out_shape=(jax.ShapeDtypeStruct((bsz, nh, LANES, seq), F32),
                   jax.ShapeDtypeStruct((bsz, nh, seq, LANES), BF16),
                   jax.ShapeDtypeStruct((bsz, nh, nbp, LANES), F32),
                   jax.ShapeDtypeStruct((bsz, nh, nblk, LANES, blk), BF16)),
        compiler_params=_params("arbitrary", "arbitrary"),
        name="moba_prep",
    )(mo, gq.reshape(1, dh), gk.reshape(1, dh))
    return pl.pallas_call(
        functools.partial(_moba_kernel, nh=nh, dh=dh, nbp=nbp),
        grid=(bsz, nblk),
        in_specs=[pl.BlockSpec((None, nh, LANES, blk), lambda b, i: (b, 0, 0, i)),
                  pl.BlockSpec((None, nh, seq, LANES), lambda b, i: (b, 0, 0, 0)),
                  pl.BlockSpec((None, nh, nbp, LANES), lambda b, i: (b, 0, 0, 0)),
                  pl.BlockSpec((None, nh, nblk, LANES, blk), lambda b, i: (b, 0, 0, 0, 0))],
        out_specs=pl.BlockSpec((None, hd, blk), lambda b, i: (b, 0, i)),
        out_shape=jax.ShapeDtypeStruct((bsz, hd, seq), BF16),
        scratch_shapes=[pltpu.VMEM((nh, LANES, blk), BF16),
                        pltpu.VMEM((nh, 1, blk), F32),
                        pltpu.VMEM((nh, LANES, blk), F32)],
        compiler_params=_params("arbitrary", "arbitrary"),
        name="moba_attn",
    )(qt, ka, km, vt)


def _merge_kernel(x_ref, ya_ref, hb_ref, mc_ref, gl_ref, mod_ref, g2_ref, who_ref, wmo_ref, wout_ref,
                  rwh_ref, rwl_ref, rb_ref,
                  x1_ref, h2_ref, idx_ref, wgt_ref, pos_ref, cnt_ref, *, tm, ne):
    first = (pl.program_id(0) == 0) & (pl.program_id(1) == 0)

    @pl.when(first)
    def _():
        cnt_ref[...] = jnp.zeros_like(cnt_ref)

    d = x_ref.shape[-1]
    yb = _dot(hb_ref[...], who_ref[...])
    yc = _tn_dot(mc_ref[...], wmo_ref[...])
    merged = (_sigmoid(gl_ref[:, 0:d].astype(F32)) * ya_ref[...].astype(F32)
              + _sigmoid(gl_ref[:, d:2 * d].astype(F32)) * yb
              + _sigmoid(gl_ref[:, 2 * d:3 * d].astype(F32)) * yc)
    x1 = x_ref[...] + mod_ref[2:3, :] * _dot(merged.astype(BF16), wout_ref[...])
    x1_ref[...] = x1
    ms = jnp.mean(x1 * x1, axis=-1, keepdims=True)
    h2 = x1 * lax.rsqrt(ms + EPS) * g2_ref[...]
    h2 = h2 * (1.0 + mod_ref[4:5, :]) + mod_ref[3:4, :]
    h2_ref[...] = h2
    hi = h2.astype(BF16)
    lo = (h2 - hi.astype(F32)).astype(BF16)
    logits = _nt_dot(rwh_ref[...], hi) + _nt_dot(rwh_ref[...], lo) + _nt_dot(rwl_ref[...], hi)
    scores = _sigmoid(logits)
    e_iota = lax.broadcasted_iota(I32, (ne, tm), 0)
    work = scores + rb_ref[...]
    picked = jnp.zeros((ne, tm), F32)
    hits, tops = [], []
    for _ in range(TOP_K):
        mx = jnp.max(work, axis=0, keepdims=True)
        first_e = jnp.min(jnp.where(work == mx, e_iota, ne), axis=0, keepdims=True)
        hit = e_iota == first_e
        hits.append((hit, first_e))
        tops.append(jnp.sum(jnp.where(hit, scores, 0.0), axis=0, keepdims=True))
        work = jnp.where(hit, -jnp.inf, work)
        picked = jnp.where(hit, 1.0, picked)
    total = functools.reduce(lambda a, b: a + b, tops)
    upper = (lax.broadcasted_iota(I32, (tm, tm), 0) < lax.broadcasted_iota(I32, (tm, tm), 1)).astype(BF16)
    rank = _dot(picked.astype(BF16), upper) + cnt_ref[:, 0:1]
    zero_row = jnp.zeros((1, tm), F32)
    idx_rows, w_rows, pos_rows = [], [], []
    for (hit, first_e), top in zip(hits, tops):
        idx_rows.append(first_e)
        w_rows.append(top / total * ROUTED_SCALE)
        pos_rows.append(jnp.sum(jnp.where(hit, rank, 0.0), axis=0, keepdims=True))
    pad = 8 - TOP_K
    idx_ref[...] = jnp.concatenate(idx_rows + [jnp.zeros((1, tm), I32)] * pad, axis=0)
    wgt_ref[...] = jnp.concatenate(w_rows + [zero_row] * pad, axis=0)
    pos_ref[...] = jnp.concatenate(pos_rows + [zero_row] * pad, axis=0).astype(I32)
    cnt_ref[...] = cnt_ref[...] + jnp.sum(picked, axis=1, keepdims=True)


def _merge_router(x, ya, hb, mc, gl, mod, norm2_g, who, wmo, wout, rwh, rwl, rbias, tm):
    bsz, seq, d = x.shape
    ne = rwh.shape[0]
    nt = seq // tm
    n_tok = bsz * seq
    row = lambda n: pl.BlockSpec((None, tm, n), lambda b, i: (b, i, 0))
    col = pl.BlockSpec((8, tm), lambda b, i: (0, b * nt + i))
    return pl.pallas_call(
        functools.partial(_merge_kernel, tm=tm, ne=ne),
        grid=(bsz, nt),
        in_specs=[row(d), row(d), row(hb.shape[-1]),
                  pl.BlockSpec((None, mc.shape[1], tm), lambda b, i: (b, 0, i)), row(gl.shape[-1]),
                  pl.BlockSpec((None, N_MOD, d), lambda b, i: (b, 0, 0)),
                  _const_spec((1, d)), _const_spec(who.shape), _const_spec(wmo.shape),
                  _const_spec(wout.shape), _const_spec(rwh.shape), _const_spec(rwl.shape),
                  _const_spec((ne, 1))],
        out_specs=[row(d), row(d), col, col, col,
                   pl.BlockSpec((ne, LANES), lambda b, i: (0, 0))],
        out_shape=(jax.ShapeDtypeStruct((bsz, seq, d), F32),
                   jax.ShapeDtypeStruct((bsz, seq, d), F32),
                   jax.ShapeDtypeStruct((8, n_tok), I32),
                   jax.ShapeDtypeStruct((8, n_tok), F32),
                   jax.ShapeDtypeStruct((8, n_tok), I32),
                   jax.ShapeDtypeStruct((ne, LANES), F32)),
        compiler_params=_params("arbitrary", "arbitrary"),
        name="merge_router",
    )(x, ya, hb, mc, gl, mod, norm2_g.reshape(1, d), who, wmo, wout, rwh, rwl, rbias.reshape(ne, 1))


def _dispatch_kernel(dest_ref, h_ref, xs_ref, sem, *, tm):
    def body(t, carry):
        for k in range(TOP_K):
            pltpu.make_async_copy(h_ref.at[pl.ds(t, 1), :],
                                  xs_ref.at[pl.ds(dest_ref[k, t], 1), :], sem).start()
        return carry

    lax.fori_loop(0, tm, body, 0)
    for k in range(TOP_K):
        pltpu.make_async_copy(h_ref, xs_ref.at[pl.ds(0, tm), :], sem).wait()


def _dispatch(dest, h2, n_rows, tm):
    n_tok, d = h2.shape
    return pl.pallas_call(
        functools.partial(_dispatch_kernel, tm=tm),
        grid=(n_tok // tm,),
        in_specs=[pl.BlockSpec((8, tm), lambda i: (0, i), memory_space=pltpu.SMEM),
                  pl.BlockSpec((tm, d), lambda i: (i, 0))],
        out_specs=pl.BlockSpec(memory_space=pl.ANY),
        out_shape=jax.ShapeDtypeStruct((n_rows, d), F32),
        scratch_shapes=[pltpu.SemaphoreType.DMA],
        compiler_params=_params("arbitrary"),
        name="moe_dispatch",
    )(dest, h2)


def _expert_kernel(be_ref, nv_ref, x_ref, wg_ref, wu_ref, wd_ref, y_ref, *, rb):
    nv = nv_ref[pl.program_id(0)]

    @pl.when(nv > 0)
    def _():
        rows = lax.broadcasted_iota(I32, (rb, 1), 0)
        x = jnp.where(rows < nv, x_ref[...], 0.0).astype(BF16)
        g = _dot(x, wg_ref[...])
        u = _dot(x, wu_ref[...])
        y_ref[...] = _dot((_silu(g) * u).astype(BF16), wd_ref[...])


def _experts(block_expert, block_rows, xs, wg, wu, wd, rb):
    n_rows, d = xs.shape
    n_blocks = n_rows // rb
    de = wg.shape[-1]
    live = lambda i, be, nv: jnp.where(nv[i] > 0, i, n_blocks - 1)
    grid_spec = pltpu.PrefetchScalarGridSpec(
        num_scalar_prefetch=2,
        grid=(n_blocks,),
        in_specs=[pl.BlockSpec((rb, d), lambda i, be, nv: (live(i, be, nv), 0)),
                  pl.BlockSpec((None, d, de), lambda i, be, nv: (be[i], 0, 0)),
                  pl.BlockSpec((None, d, de), lambda i, be, nv: (be[i], 0, 0)),
                  pl.BlockSpec((None, de, d), lambda i, be, nv: (be[i], 0, 0))],
        out_specs=pl.BlockSpec((rb, d), lambda i, be, nv: (live(i, be, nv), 0)),
    )
    return pl.pallas_call(
        functools.partial(_expert_kernel, rb=rb),
        grid_spec=grid_spec,
        out_shape=jax.ShapeDtypeStruct((n_rows, d), F32),
        compiler_params=_params("arbitrary"),
        name="moe_experts",
    )(block_expert, block_rows, xs, wg, wu, wd)


def _combine_kernel(dest_ref, wgt_ref, x1_ref, h2_ref, mod_ref, sg_ref, su_ref, sd_ref, ys_ref,
                    o_ref, gbuf, sem, *, tm):
    def body(t, carry):
        for k in range(TOP_K):
            pltpu.make_async_copy(ys_ref.at[pl.ds(dest_ref[k, t], 1), :],
                                  gbuf.at[k, pl.ds(t, 1), :], sem).start()
        return carry

    lax.fori_loop(0, tm, body, 0)
    hb = h2_ref[...].astype(BF16)
    hid = _silu(_dot(hb, sg_ref[...])) * _dot(hb, su_ref[...])
    y = _dot(hid.astype(BF16), sd_ref[...])
    wcol = wgt_ref[...].T
    for k in range(TOP_K):
        pltpu.make_async_copy(ys_ref.at[pl.ds(0, tm), :], gbuf.at[k], sem).wait()
    for k in range(TOP_K):
        y = y + gbuf[k] * wcol[:, k:k + 1]
    o_ref[...] = x1_ref[...] + mod_ref[5:6, :] * y


def _combine(dest, wgt, x1, h2, mod, sg, su, sd, ys, tm):
    bsz, seq, d = x1.shape
    nt = seq // tm
    row = pl.BlockSpec((None, tm, d), lambda b, i: (b, i, 0))
    return pl.pallas_call(
        functools.partial(_combine_kernel, tm=tm),
        grid=(bsz, nt),
        in_specs=[pl.BlockSpec((8, tm), lambda b, i: (0, b * nt + i), memory_space=pltpu.SMEM),
                  pl.BlockSpec((8, tm), lambda b, i: (0, b * nt + i)),
                  row, row,
                  pl.BlockSpec((None, N_MOD, d), lambda b, i: (b, 0, 0)),
                  _const_spec(sg.shape), _const_spec(su.shape), _const_spec(sd.shape),
                  pl.BlockSpec(memory_space=pl.ANY)],
        out_specs=row,
        out_shape=jax.ShapeDtypeStruct((bsz, seq, d), F32),
        scratch_shapes=[pltpu.VMEM((TOP_K, tm, d), F32), pltpu.SemaphoreType.DMA],
        compiler_params=_params("arbitrary", "arbitrary"),
        name="moe_combine",
    )(dest, wgt, x1, h2, mod, sg, su, sd, ys)


def _dest_kernel(ps_ref, idx_ref, pos_ref, o_ref, *, ne):
    idx = idx_ref[...]
    acc = pos_ref[...]
    for e in range(ne):
        acc = acc + jnp.where(idx == e, ps_ref[e], 0)
    o_ref[...] = acc


def _routing_tables(idx, pos, counts, n_tok, ne, rb):
    n_assign = n_tok * TOP_K
    n_blocks = -(-n_assign // rb) + ne
    padded = (counts + rb - 1) // rb * rb
    pend = jnp.cumsum(padded)
    pstart = (pend - padded).astype(I32)
    tt = min(8192, n_tok)
    col = pl.BlockSpec((8, tt), lambda i: (0, i))
    dest = pl.pallas_call(
        functools.partial(_dest_kernel, ne=ne),
        grid=(n_tok // tt,),
        in_specs=[pl.BlockSpec(memory_space=pltpu.SMEM), col, col],
        out_specs=col,
        out_shape=jax.ShapeDtypeStruct((8, n_tok), I32),
        compiler_params=_params("arbitrary"),
        name="moe_dest",
    )(pstart, idx, pos)
    bstart = jnp.arange(n_blocks, dtype=I32) * rb
    bexp = jnp.minimum(jnp.sum((pend[None, :] <= bstart[:, None]).astype(I32), axis=1), ne - 1)
    brows = jnp.clip(counts[bexp] - (bstart - pstart[bexp]), 0, rb).astype(I32)
    return dest, bexp, brows, n_blocks


def _block_diag_in(bbt, g):
    gw, n = bbt.shape
    p = n // g
    rows = jnp.tile(bbt, (g, 1))
    rg = jnp.arange(g * gw)[:, None] // gw
    cg = jnp.arange(n)[None, :] // p
    return jnp.where(rg == cg, rows, 0.0)


def _block_diag_out(c, sign):
    g, gw, p = c.shape
    cols = jnp.tile(c.transpose(0, 2, 1).reshape(g * p, gw), (1, g))
    rg = jnp.arange(g * p)[:, None] // p
    cg = jnp.arange(g * gw)[None, :] // gw
    return jnp.where(rg == cg, sign * cols, 0.0)


def kernel(x, c, ada_w, ada_b, norm1_g, w_in, ssm_log_dt, ssm_lambda_re, ssm_lambda_im, ssm_b_re,
           ssm_b_im, ssm_c_re, ssm_c_im, ssm_d, ssm_w_glu, hgrn_lb_logits, hgrn_norm_g, hgrn_w_o,
           moba_q_norm_g, moba_k_norm_g, moba_w_o, w_out, norm2_g, router_w, router_bias,
           exp_w_gate, exp_w_up, exp_w_down, shared_w_gate, shared_w_up, shared_w_down):
    bsz, seq, d = x.shape
    depth = w_in.shape[0]
    n_tok = bsz * seq
    g = ssm_lambda_re.shape[1]
    d_ssm = ssm_d.shape[-1]
    d_hk = hgrn_lb_logits.shape[-1]
    d_moba = moba_w_o.shape[1]
    ne = router_w.shape[-1]
    widths = (d_ssm, 4 * d_hk, 3 * d_moba, 3 * d)
    tm = min(256, seq)
    rb = 512

    mod_all = _ada_mod(c, ada_w, ada_b).reshape(depth, bsz, N_MOD, d)
    abar, bbr, bbi, lower = _prep_params(ssm_log_dt, ssm_lambda_re, ssm_lambda_im, ssm_b_re, ssm_b_im,
                                         hgrn_lb_logits)

    for l in range(depth):
        mod = mod_all[l]
        u, hg, mo, gl = _inproj(x, mod, norm1_g[l], w_in[l].astype(BF16), widths, tm)
        wb = jnp.concatenate([_block_diag_in(bbr[l], g), _block_diag_in(bbi[l], g)], axis=1).astype(BF16)
        wc = jnp.concatenate([_block_diag_out(ssm_c_re[l], 1.0), _block_diag_out(ssm_c_im[l], -1.0)],
                             axis=0).astype(BF16)
        ya = _s5_mixer(u, wb, wc, abar[l], ssm_d[l], ssm_w_glu[l].astype(BF16), min(128, seq))
        hb = _hgrn_mixer(hg, lower[l], hgrn_norm_g[l], min(256, seq))
        mc = _moba_mixer(mo, moba_q_norm_g[l], moba_k_norm_g[l])
        rwt = router_w[l].T
        rwh = rwt.astype(BF16)
        rwl = (rwt - rwh.astype(F32)).astype(BF16)
        x1, h2, idx, wgt, pos, cnt = _merge_router(
            x, ya, hb, mc, gl, mod, norm2_g[l], hgrn_w_o[l].astype(BF16), moba_w_o[l].astype(BF16),
            w_out[l].astype(BF16), rwh, rwl, router_bias[l], tm)
        counts = cnt[:, 0].astype(I32)
        dest, bexp, brows, n_blocks = _routing_tables(idx, pos, counts, n_tok, ne, rb)
        xs = _dispatch(dest, h2.reshape(n_tok, d), n_blocks * rb, tm)
        ys = _experts(bexp, brows, xs, exp_w_gate[l].astype(BF16), exp_w_up[l].astype(BF16),
                      exp_w_down[l].astype(BF16), rb)
        x = _combine(dest, wgt, x1, h2, mod, shared_w_gate[l].astype(BF16),
                     shared_w_up[l].astype(BF16), shared_w_down[l].astype(BF16), ys, tm)
    return x
```

```python
import functools

import jax
import jax.numpy as jnp
from jax import lax
from jax.experimental import pallas as pl
from jax.experimental.pallas import tpu as pltpu

F32 = jnp.float32
BF16 = jnp.bfloat16
I32 = jnp.int32
HIGHEST = lax.Precision.HIGHEST

N_MOD = 6
EPS = 1e-6
MOBA_BLOCK = 256
MOBA_TOPK = 3
TOP_K = 6
ROUTED_SCALE = 2.5

LANES = 128
HGRN_CHUNK = 64
HGRN_SUB = 16
S5_ROW_PAD = 8
MASK_BIG = 30000.0
VMEM_LIMIT = 56 * 1024 * 1024


def _nt_dot(a, b, precision=None):
    return lax.dot_general(a, b, (((1,), (1,)), ((), ())), preferred_element_type=F32,
                           precision=precision)


def _tn_dot(a, b, precision=None):
    return lax.dot_general(a, b, (((0,), (0,)), ((), ())), preferred_element_type=F32,
                           precision=precision)


def _dot(a, b, precision=None):
    return jnp.dot(a, b, preferred_element_type=F32, precision=precision)


def _bf16_terms(x, n):
    terms = []
    for _ in range(n):
        t = x.astype(BF16)
        terms.append(t)
        x = x - t.astype(F32)
    return terms


def _dot_exact_lhs(a, b, n):
    ab = a.astype(BF16)
    return functools.reduce(lambda p, q: p + q, [_dot(ab, t) for t in _bf16_terms(b, n)])


def _dot_exact_rhs(a, b, n):
    bb = b.astype(BF16)
    return functools.reduce(lambda p, q: p + q, [_dot(t, bb) for t in _bf16_terms(a, n)])


def _pack_bf16_pairs(y):
    m = y.shape[1] // 2
    lo = lax.bitcast_convert_type(y[:, :m].astype(BF16).astype(F32), jnp.uint32)
    hi = lax.bitcast_convert_type(y[:, m:].astype(BF16).astype(F32), jnp.uint32)
    return hi | (lo >> 16)


def _unpack_bf16_pairs(w):
    lo = lax.bitcast_convert_type(w << 16, F32)
    hi = lax.bitcast_convert_type(w & jnp.uint32(0xFFFF0000), F32)
    return jnp.concatenate([lo, hi], axis=1)


def _sigmoid(x):
    return 1.0 / (1.0 + jnp.exp(-x))


def _silu(x):
    return x * _sigmoid(x)


def _params(*sem):
    return pltpu.CompilerParams(dimension_semantics=sem, vmem_limit_bytes=VMEM_LIMIT)


def _const_spec(shape):
    nd = len(shape)
    return pl.BlockSpec(shape, lambda *_: (0,) * nd, pipeline_mode=pl.Buffered(1))


def _ada_kernel(c_ref, w_ref, b_ref, o_ref):
    c = c_ref[...]
    o_ref[...] = _dot(_silu(c), w_ref[...], HIGHEST) + b_ref[...]


def _ada_mod(c, ada_w, ada_b):
    depth, d, n = ada_w.shape
    bsz = c.shape[0]
    tn = n // 4
    return pl.pallas_call(
        _ada_kernel,
        grid=(depth, n // tn),
        in_specs=[pl.BlockSpec((bsz, d), lambda l, j: (0, 0)),
                  pl.BlockSpec((None, d, tn), lambda l, j: (l, 0, j)),
                  pl.BlockSpec((None, 1, tn), lambda l, j: (l, 0, j))],
        out_specs=pl.BlockSpec((None, bsz, tn), lambda l, j: (l, 0, j)),
        out_shape=jax.ShapeDtypeStruct((depth, bsz, n), F32),
        compiler_params=_params("arbitrary", "arbitrary"),
        name="ada_mod",
    )(c, ada_w, ada_b.reshape(depth, 1, n))


def _prep_kernel(ldt_ref, lr_ref, li_ref, btr_ref, bti_ref, lbl_ref,
                 abar_ref, bbr_ref, bbi_ref, lb_ref):
    depth = ldt_ref.shape[0]
    for l in range(depth):
        dt = jnp.exp(ldt_ref[l])
        lr = lr_ref[l]
        li = li_ref[l]
        mag = jnp.exp(lr * dt)
        ar = mag * jnp.cos(li * dt)
        ai = mag * jnp.sin(li * dt)
        inv = 1.0 / (lr * lr + li * li)
        cr = ((ar - 1.0) * lr + ai * li) * inv
        ci = (ai * lr - (ar - 1.0) * li) * inv
        abar_ref[l, 0:1, :] = ar
        abar_ref[l, 1:2, :] = ai
        btr = btr_ref[l]
        bti = bti_ref[l]
        bbr_ref[l] = cr * btr - ci * bti
        bbi_ref[l] = cr * bti + ci * btr
    rows = [lbl_ref[l:l + 1, :] for l in range(depth)]
    mx = functools.reduce(jnp.maximum, rows)
    ex = [jnp.exp(r - mx) for r in rows]
    den = functools.reduce(lambda a, b: a + b, ex)
    run = ex[0] / den
    first = run
    lb_ref[0:1, :] = run - first
    for l in range(1, depth):
        run = run + ex[l] / den
        lb_ref[l:l + 1, :] = run - first


def _prep_params(log_dt, lam_re, lam_im, b_re, b_im, lb_logits):
    depth, g, p = lam_re.shape
    gw = b_re.shape[-1]
    n = g * p
    hk = lb_logits.shape[-1]
    ldt = jnp.repeat(log_dt, p, axis=1).reshape(depth, 1, n)
    btr = b_re.transpose(0, 3, 1, 2).reshape(depth, gw, n)
    bti = b_im.transpose(0, 3, 1, 2).reshape(depth, gw, n)
    return pl.pallas_call(
        _prep_kernel,
        out_shape=(jax.ShapeDtypeStruct((depth, 2, n), F32),
                   jax.ShapeDtypeStruct((depth, gw, n), F32),
                   jax.ShapeDtypeStruct((depth, gw, n), F32),
                   jax.ShapeDtypeStruct((depth, hk), F32)),
        name="param_prep",
    )(ldt, lam_re.reshape(depth, 1, n), lam_im.reshape(depth, 1, n), btr, bti, lb_logits)


def _inproj_kernel(x_ref, mod_ref, g_ref, w_ref, u_ref, hg_ref, mo_ref, gl_ref, *, widths):
    x = x_ref[...]
    ms = jnp.mean(x * x, axis=-1, keepdims=True)
    h = x * lax.rsqrt(ms + EPS) * g_ref[...]
    h = h * (1.0 + mod_ref[1:2, :]) + mod_ref[0:1, :]
    hb = h.astype(BF16)
    d_ssm, d_hgrn, d_moba, d_gate = widths
    o0 = 0
    u_ref[...] = _dot(hb, w_ref[:, o0:o0 + d_ssm])
    o0 += d_ssm
    hg_ref[...] = _dot(hb, w_ref[:, o0:o0 + d_hgrn])
    o0 += d_hgrn
    mo_ref[...] = _dot(hb, w_ref[:, o0:o0 + d_moba])
    o0 += d_moba
    step = 1024
    for j in range(0, d_gate, step):
        gl_ref[:, j:j + step] = _dot(hb, w_ref[:, o0 + j:o0 + j + step]).astype(BF16)


def _inproj(x, mod, norm_g, w_in_bf, widths, tm):
    bsz, seq, d = x.shape
    d_ssm, d_hgrn, d_moba, d_gate = widths
    d_in = w_in_bf.shape[1]
    row = lambda n: pl.BlockSpec((None, tm, n), lambda b, i: (b, i, 0))
    return pl.pallas_call(
        functools.partial(_inproj_kernel, widths=widths),
        grid=(bsz, seq // tm),
        in_specs=[row(d),
                  pl.BlockSpec((None, N_MOD, d), lambda b, i: (b, 0, 0)),
                  _const_spec((1, d)),
                  _const_spec((d, d_in))],
        out_specs=[row(d_ssm), row(d_hgrn), row(d_moba), row(d_gate)],
        out_shape=(jax.ShapeDtypeStruct((bsz, seq, d_ssm), F32),
                   jax.ShapeDtypeStruct((bsz, seq, d_hgrn), F32),
                   jax.ShapeDtypeStruct((bsz, seq, d_moba), F32),
                   jax.ShapeDtypeStruct((bsz, seq, d_gate), BF16)),
        compiler_params=_params("arbitrary", "arbitrary"),
        name="in_proj",
    )(x, mod, norm_g.reshape(1, d), w_in_bf)


def _s5_kernel(u_ref, wb_ref, wc_ref, abar_ref, d_ref, wglu_ref, o_ref, bu_ref, st_ref,
               *, nb, tl, ns, unroll):
    nc = ns // LANES
    rs = tl + S5_ROW_PAD

    @pl.when(pl.program_id(0) == 0)
    def _():
        st_ref[...] = jnp.zeros_like(st_ref)

    for b in range(nb):
        bu = _dot(u_ref[b].astype(BF16), wb_ref[...])
        for c in range(2 * nc):
            bu_ref[c, b * rs:b * rs + tl, :] = bu[:, c * LANES:(c + 1) * LANES]

    ar = [jnp.broadcast_to(abar_ref[0:1, c * LANES:(c + 1) * LANES], (nb, LANES)) for c in range(nc)]
    ai = [jnp.broadcast_to(abar_ref[1:2, c * LANES:(c + 1) * LANES], (nb, LANES)) for c in range(nc)]

    def outer(j, carry):
        sr, si = list(carry[:nc]), list(carry[nc:])
        for k in range(unroll):
            t = j * unroll + k
            for c in range(nc):
                br = bu_ref[c, pl.ds(t, nb, stride=rs), :]
                bi = bu_ref[nc + c, pl.ds(t, nb, stride=rs), :]
                nr = ar[c] * sr[c] - ai[c] * si[c] + br
                ni = ar[c] * si[c] + ai[c] * sr[c] + bi
                bu_ref[c, pl.ds(t, nb, stride=rs), :] = nr
                bu_ref[nc + c, pl.ds(t, nb, stride=rs), :] = ni
                sr[c], si[c] = nr, ni
        return tuple(sr) + tuple(si)

    init = tuple(st_ref[c] for c in range(2 * nc))
    fin = lax.fori_loop(0, tl // unroll, outer, init)
    for c in range(2 * nc):
        st_ref[c] = fin[c]

    dm = o_ref.shape[-1]
    for b in range(nb):
        s = jnp.concatenate([bu_ref[c, b * rs:b * rs + tl, :] for c in range(2 * nc)], axis=1)
        y = _dot(s.astype(BF16), wc_ref[...]) + d_ref[...] * u_ref[b]
        z = _dot(jax.nn.gelu(y).astype(BF16), wglu_ref[...])
        o_ref[b] = (z[:, :dm] * _sigmoid(z[:, dm:])).astype(o_ref.dtype)


def _s5_mixer(u, wb, wc, abar, d_skip, wglu, tl):
    bsz, seq, d_ssm = u.shape
    ns = abar.shape[-1]
    dm = wglu.shape[1] // 2
    return pl.pallas_call(
        functools.partial(_s5_kernel, nb=bsz, tl=tl, ns=ns, unroll=8),
        grid=(seq // tl,),
        in_specs=[pl.BlockSpec((bsz, tl, d_ssm), lambda i: (0, i, 0)),
                  _const_spec(wb.shape), _const_spec(wc.shape), _const_spec(abar.shape),
                  _const_spec((1, d_ssm)), _const_spec(wglu.shape)],
        out_specs=pl.BlockSpec((bsz, tl, dm), lambda i: (0, i, 0)),
        out_shape=jax.ShapeDtypeStruct((bsz, seq, dm), BF16),
        scratch_shapes=[pltpu.VMEM((2 * ns // LANES, bsz * (tl + S5_ROW_PAD), LANES), F32),
                        pltpu.VMEM((2 * ns // LANES, bsz, LANES), F32)],
        compiler_params=_params("arbitrary"),
        name="s5_mixer",
    )(u, wb, wc, abar, d_skip.reshape(1, d_ssm), wglu)


def _hgrn_kernel(x_ref, lb_ref, ng_ref, o_ref, st_ref, *, tl, hd, dk):
    cs, sb = HGRN_CHUNK, HGRN_SUB

    @pl.when(pl.program_id(1) == 0)
    def _():
        st_ref[...] = jnp.zeros_like(st_ref)

    def iota(shape, dim):
        return lax.broadcasted_iota(I32, shape, dim)

    tril = (iota((cs, cs), 0) >= iota((cs, cs), 1)).astype(F32)
    head_eq = (iota((hd, hd), 0) // dk) == (iota((hd, hd), 1) // dk)
    block_ones = head_eq.astype(BF16)
    block_mean = head_eq.astype(F32) * (1.0 / dk)
    sel = ((iota((sb, sb * sb), 1) // sb) == iota((sb, sb * sb), 0)).astype(BF16)
    s_iota = iota((sb, hd), 0)
    lane_head = iota((1, hd), 1) // dk
    n_heads = hd // dk
    lb = lb_ref[...]
    ng = ng_ref[...]

    def head_copies(a):
        return jnp.concatenate([jnp.where(lane_head == h, a, 0.0) for h in range(n_heads)], axis=0)

    def off_diag_scores(q, key, cum, t0, t1, s0, s1):
        r = cum[s1 - 1:s1, :]
        qt = q[t0:t1] * jnp.exp(cum[t0:t1] - r)
        ks = key[s0:s1] * jnp.exp(r - cum[s0:s1])
        return _nt_dot(qt.astype(BF16), head_copies(ks).astype(BF16))

    nch = tl // cs
    half = cs // 2
    spans = ((half, cs, 0, half), (sb, half, 0, sb), (half + sb, cs, half, half + sb))
    pre = []
    for c in range(nch):
        r0 = c * cs
        q = x_ref[r0:r0 + cs, 0:hd]
        fr = x_ref[r0:r0 + cs, hd:2 * hd]
        iv = x_ref[r0:r0 + cs, 2 * hd:3 * hd]
        forget = lb + (1.0 - lb) * _sigmoid(fr)
        key = 1.0 - forget
        cum = _dot_exact_lhs(tril, jnp.log(forget), 3)
        pre.append((q, key, iv, cum))
    heads_sum = []
    for q, key, iv, cum in pre:
        for a in range(cs // sb):
            lo = a * sb
            cum_a, qa, ka = cum[lo:lo + sb], q[lo:lo + sb], key[lo:lo + sb]
            slabs = []
            for t in range(sb):
                arg = jnp.where(s_iota <= t, cum_a[t:t + 1] - cum_a, -jnp.inf)
                slabs.append(jnp.exp(arg) * (qa[t:t + 1] * ka))
            heads_sum.append(_dot(jnp.concatenate(slabs, axis=0).astype(BF16), block_ones))
    off_scores = [[off_diag_scores(q, key, cum, *sp) for sp in spans] for q, key, iv, cum in pre]
    intras = []
    for c, (q, key, iv, cum) in enumerate(pre):
        parts = []
        for a in range(cs // sb):
            lo = a * sb
            w = heads_sum[c * (cs // sb) + a] * jnp.concatenate([iv[lo:lo + sb]] * sb, axis=0)
            parts.append(_dot(sel, w.astype(BF16)))
        lvl1, lvl2a, lvl2b = [
            _dot(sc.astype(BF16), head_copies(iv[sp[2]:sp[3]]).astype(BF16))
            for sc, sp in zip(off_scores[c], spans)]
        zeros = jnp.zeros((sb, hd), F32)
        intras.append(jnp.concatenate(parts, axis=0)
                      + jnp.concatenate([zeros, lvl2a, lvl1[0:sb], lvl1[sb:half] + lvl2b], axis=0))
    upds, decays, qcs = [], [], []
    for q, key, iv, cum in pre:
        last = cum[cs - 1:cs, :]
        kc = key * jnp.exp(last - cum)
        upds.append(jnp.where(head_eq, _tn_dot(iv.astype(BF16), kc.astype(BF16)), 0.0))
        decays.append(jnp.exp(last))
        qcs.append((q * jnp.exp(cum)).astype(BF16))
    st = st_ref[...]
    for c in range(nch):
        out = intras[c] + _nt_dot(qcs[c], st.astype(BF16))
        st = st * decays[c] + upds[c]
        ms = _dot_exact_rhs(out * out, block_mean, 2)
        g = x_ref[c * cs:(c + 1) * cs, 3 * hd:4 * hd]
        o_ref[c * cs:(c + 1) * cs, :] = (out * lax.rsqrt(ms + EPS) * ng * _silu(g)).astype(o_ref.dtype)
    st_ref[...] = st


def _hgrn_mixer(hg, lb, norm_g, tl):
    bsz, seq, d4 = hg.shape
    hd = d4 // 4
    dk = norm_g.shape[-1]
    ng = jnp.tile(norm_g, hd // dk).reshape(1, hd)
    return pl.pallas_call(
        functools.partial(_hgrn_kernel, tl=tl, hd=hd, dk=dk),
        grid=(bsz, seq // tl),
        in_specs=[pl.BlockSpec((None, tl, d4), lambda b, i: (b, i, 0)),
                  _const_spec((1, hd)), _const_spec((1, hd))],
        out_specs=pl.BlockSpec((None, tl, hd), lambda b, i: (b, i, 0)),
        out_shape=jax.ShapeDtypeStruct((bsz, seq, hd), BF16),
        scratch_shapes=[pltpu.VMEM((hd, hd), F32)],
        compiler_params=_params("arbitrary", "arbitrary"),
        name="hgrn_mixer",
    )(hg, lb.reshape(1, hd), ng)


def _moba_prep_kernel(x_ref, gq_ref, gk_ref, qt_ref, ka_ref, km_ref, vt_ref, *, nh, dh):
    i = pl.program_id(1)
    blk = MOBA_BLOCK
    hd = nh * dh

    @pl.when(i == 0)
    def _():
        km_ref[...] = jnp.zeros_like(km_ref)

    lane = lax.broadcasted_iota(I32, (blk, LANES - dh), 1)
    onehot = (lane == i).astype(F32)
    pad = jnp.zeros((blk, LANES - dh), F32)
    scale = dh ** -0.5
    for h in range(nh):
        qh = x_ref[:, h * dh:(h + 1) * dh]
        kh = x_ref[:, hd + h * dh:hd + (h + 1) * dh]
        vh = x_ref[:, 2 * hd + h * dh:2 * hd + (h + 1) * dh]
        qn = qh * lax.rsqrt(jnp.mean(qh * qh, axis=-1, keepdims=True) + EPS) * gq_ref[...] * scale
        kn = kh * lax.rsqrt(jnp.mean(kh * kh, axis=-1, keepdims=True) + EPS) * gk_ref[...]
        qt_ref[h] = jnp.concatenate([qn, pad], axis=1).T
        ka_ref[h] = jnp.concatenate([kn, onehot], axis=1).astype(BF16)
        kmean = jnp.mean(kn, axis=0, keepdims=True)
        km_ref[h, pl.ds(i, 1), :] = jnp.concatenate([kmean, jnp.zeros((1, LANES - dh), F32)], axis=1)
        vt_ref[h] = jnp.concatenate([vh, pad + 1.0], axis=1).T.astype(BF16)


def _moba_kernel(qt_ref, ka_ref, km_ref, vt_ref, o_ref, qa_scr, m_scr, acc_scr, *, nh, dh, nbp):
    i = pl.program_id(1)
    blk = MOBA_BLOCK
    r0 = pl.multiple_of(i * blk, blk)
    b_iota = lax.broadcasted_iota(I32, (nbp, blk), 0)
    key_i = lax.broadcasted_iota(I32, (blk, blk), 0)
    qry_i = lax.broadcasted_iota(I32, (blk, blk), 1)

    for h in range(nh):
        qt = qt_ref[h]
        gate = _dot(km_ref[h], qt, HIGHEST)
        work = jnp.where(b_iota < i, gate, -jnp.inf)
        chosen = jnp.zeros((nbp, blk), jnp.bool_)
        for _ in range(MOBA_TOPK):
            mx = jnp.max(work, axis=0, keepdims=True)
            first = jnp.min(jnp.where(work == mx, b_iota, nbp), axis=0, keepdims=True)
            hit = (b_iota == first) & (mx > -jnp.inf)
            chosen = chosen | hit
            work = jnp.where(hit, -jnp.inf, work)
        pen = jnp.where(chosen, 0.0, -MASK_BIG)
        qa_scr[h] = jnp.concatenate(
            [qt[0:dh], pen, jnp.zeros((LANES - dh - nbp, blk), F32)], axis=0).astype(BF16)
        s = _dot(ka_ref[h, pl.ds(r0, blk), :], qt.astype(BF16))
        s = jnp.where(key_i <= qry_i, s, -jnp.inf)
        m0 = jnp.max(s, axis=0, keepdims=True)
        m_scr[h] = m0
        acc_scr[h] = _dot(vt_ref[h, i], jnp.exp(s - m0).astype(BF16))

    def past_blocks(js):
        scores = [[_dot(ka_ref[h, pl.ds(pl.multiple_of(j * blk, blk), blk), :], qa_scr[h]) for j in js]
                  for h in range(nh)]
        probs, alphas = [], []
        for h in range(nh):
            m_old = m_scr[h]
            m_new = m_old
            for s in scores[h]:
                m_new = jnp.maximum(m_new, jnp.max(s, axis=0, keepdims=True))
            probs.append([jnp.exp(s - m_new).astype(BF16) for s in scores[h]])
            alphas.append(jnp.exp(m_old - m_new))
            m_scr[h] = m_new
        for h in range(nh):
            acc = alphas[h] * acc_scr[h]
            for j, p in zip(js, probs[h]):
                acc = acc + _dot(vt_ref[h, j], p)
            acc_scr[h] = acc

    def past_pair(jj, carry):
        past_blocks((2 * jj, 2 * jj + 1))
        return carry

    lax.fori_loop(0, i // 2, past_pair, 0)

    @pl.when(i % 2 == 1)
    def _():
        past_blocks((i - 1,))

    for h in range(nh):
        acc = acc_scr[h]
        o_ref[h * dh:(h + 1) * dh, :] = (acc[0:dh] / acc[dh:dh + 1]).astype(o_ref.dtype)


def _moba_mixer(mo, gq, gk):
    bsz, seq, d3 = mo.shape
    hd = d3 // 3
    dh = gq.shape[-1]
    nh = hd // dh
    blk = MOBA_BLOCK
    nblk = seq // blk
    nbp = -(-nblk // 8) * 8
    assert seq % blk == 0 and dh + nbp <= LANES
    qt, ka, km, vt = pl.pallas_call(
        functools.partial(_moba_prep_kernel, nh=nh, dh=dh),
        grid=(bsz, nblk),
        in_specs=[pl.BlockSpec((None, blk, d3), lambda b, i: (b, i, 0)),
                  _const_spec((1, dh)), _const_spec((1, dh))],
        out_specs=[pl.BlockSpec((None, nh, LANES, blk), lambda b, i: (b, 0, 0, i)),
                   pl.BlockSpec((None, nh, blk, LANES), lambda b, i: (b, 0, i, 0)),
                   pl.BlockSpec((None, nh, nbp, LANES), lambda b, i: (b, 0, 0, 0)),
                   pl.BlockSpec((None, nh, None, LANES, blk), lambda b, i: (b, 0, i, 0, 0))],
        out_shape=(jax.ShapeDtypeStruct((bsz, nh, LANES, seq), F32),
                   jax.ShapeDtypeStruct((bsz, nh, seq, LANES), BF16),
                   jax.ShapeDtypeStruct((bsz, nh, nbp, LANES), F32),
                   jax.ShapeDtypeStruct((bsz, nh, nblk, LANES, blk), BF16)),
        compiler_params=_params("arbitrary", "arbitrary"),
        name="moba_prep",
    )(mo, gq.reshape(1, dh), gk.reshape(1, dh))
    return pl.pallas_call(
        functools.partial(_moba_kernel, nh=nh, dh=dh, nbp=nbp),
        grid=(bsz, nblk),
        in_specs=[pl.BlockSpec((None, nh, LANES, blk), lambda b, i: (b, 0, 0, i)),
                  pl.BlockSpec((None, nh, seq, LANES), lambda b, i: (b, 0, 0, 0)),
                  pl.BlockSpec((None, nh, nbp, LANES), lambda b, i: (b, 0, 0, 0)),
                  pl.BlockSpec((None, nh, nblk, LANES, blk), lambda b, i: (b, 0, 0, 0, 0))],
        out_specs=pl.BlockSpec((None, hd, blk), lambda b, i: (b, 0, i)),
        out_shape=jax.ShapeDtypeStruct((bsz, hd, seq), BF16),
        scratch_shapes=[pltpu.VMEM((nh, LANES, blk), BF16),
                        pltpu.VMEM((nh, 1, blk), F32),
                        pltpu.VMEM((nh, LANES, blk), F32)],
        compiler_params=_params("arbitrary", "arbitrary"),
        name="moba_attn",
    )(qt, ka, km, vt)


def _merge_kernel(x_ref, ya_ref, hb_ref, mc_ref, gl_ref, mod_ref, g2_ref, who_ref, wmo_ref, wout_ref,
                  rwh_ref, rwl_ref, rb_ref,
                  x1_ref, h2_ref, idx_ref, wgt_ref, pos_ref, cnt_ref, *, tm, ne):
    first = (pl.program_id(0) == 0) & (pl.program_id(1) == 0)

    @pl.when(first)
    def _():
        cnt_ref[...] = jnp.zeros_like(cnt_ref)

    d = x_ref.shape[-1]
    yb = _dot(hb_ref[...], who_ref[...])
    yc = _tn_dot(mc_ref[...], wmo_ref[...])
    merged = (_sigmoid(gl_ref[:, 0:d].astype(F32)) * ya_ref[...].astype(F32)
              + _sigmoid(gl_ref[:, d:2 * d].astype(F32)) * yb
              + _sigmoid(gl_ref[:, 2 * d:3 * d].astype(F32)) * yc)
    x1 = x_ref[...] + mod_ref[2:3, :] * _dot(merged.astype(BF16), wout_ref[...])
    x1_ref[...] = x1
    ms = jnp.mean(x1 * x1, axis=-1, keepdims=True)
    h2 = x1 * lax.rsqrt(ms + EPS) * g2_ref[...]
    h2 = h2 * (1.0 + mod_ref[4:5, :]) + mod_ref[3:4, :]
    h2_ref[...] = _pack_bf16_pairs(h2)
    hi = h2.astype(BF16)
    lo = (h2 - hi.astype(F32)).astype(BF16)
    logits = _nt_dot(rwh_ref[...], hi) + _nt_dot(rwh_ref[...], lo) + _nt_dot(rwl_ref[...], hi)
    scores = _sigmoid(logits)
    e_iota = lax.broadcasted_iota(I32, (ne, tm), 0)
    work = scores + rb_ref[...]
    picked = jnp.zeros((ne, tm), F32)
    hits, tops = [], []
    for _ in range(TOP_K):
        mx = jnp.max(work, axis=0, keepdims=True)
        first_e = jnp.min(jnp.where(work == mx, e_iota, ne), axis=0, keepdims=True)
        hit = e_iota == first_e
        hits.append((hit, first_e))
        tops.append(jnp.sum(jnp.where(hit, scores, 0.0), axis=0, keepdims=True))
        work = jnp.where(hit, -jnp.inf, work)
        picked = jnp.where(hit, 1.0, picked)
    total = functools.reduce(lambda a, b: a + b, tops)
    upper = (lax.broadcasted_iota(I32, (tm, tm), 0) < lax.broadcasted_iota(I32, (tm, tm), 1)).astype(BF16)
    rank = _dot(picked.astype(BF16), upper) + cnt_ref[:, 0:1]
    zero_row = jnp.zeros((1, tm), F32)
    idx_rows, w_rows, pos_rows = [], [], []
    for (hit, first_e), top in zip(hits, tops):
        idx_rows.append(first_e)
        w_rows.append(top / total * ROUTED_SCALE)
        pos_rows.append(jnp.sum(jnp.where(hit, rank, 0.0), axis=0, keepdims=True))
    pad = 8 - TOP_K
    idx_ref[...] = jnp.concatenate(idx_rows + [jnp.zeros((1, tm), I32)] * pad, axis=0)
    wgt_ref[...] = jnp.concatenate(w_rows + [zero_row] * pad, axis=0)
    pos_ref[...] = jnp.concatenate(pos_rows + [zero_row] * pad, axis=0).astype(I32)
    cnt_ref[...] = cnt_ref[...] + jnp.sum(picked, axis=1, keepdims=True)


def _merge_router(x, ya, hb, mc, gl, mod, norm2_g, who, wmo, wout, rwh, rwl, rbias, tm):
    bsz, seq, d = x.shape
    ne = rwh.shape[0]
    nt = seq // tm
    n_tok = bsz * seq
    row = lambda n: pl.BlockSpec((None, tm, n), lambda b, i: (b, i, 0))
    col = pl.BlockSpec((8, tm), lambda b, i: (0, b * nt + i))
    return pl.pallas_call(
        functools.partial(_merge_kernel, tm=tm, ne=ne),
        grid=(bsz, nt),
        in_specs=[row(d), row(d), row(hb.shape[-1]),
                  pl.BlockSpec((None, mc.shape[1], tm), lambda b, i: (b, 0, i)), row(gl.shape[-1]),
                  pl.BlockSpec((None, N_MOD, d), lambda b, i: (b, 0, 0)),
                  _const_spec((1, d)), _const_spec(who.shape), _const_spec(wmo.shape),
                  _const_spec(wout.shape), _const_spec(rwh.shape), _const_spec(rwl.shape),
                  _const_spec((ne, 1))],
        out_specs=[row(d), row(d // 2), col, col, col,
                   pl.BlockSpec((ne, LANES), lambda b, i: (0, 0))],
        out_shape=(jax.ShapeDtypeStruct((bsz, seq, d), F32),
                   jax.ShapeDtypeStruct((bsz, seq, d // 2), jnp.uint32),
                   jax.ShapeDtypeStruct((8, n_tok), I32),
                   jax.ShapeDtypeStruct((8, n_tok), F32),
                   jax.ShapeDtypeStruct((8, n_tok), I32),
                   jax.ShapeDtypeStruct((ne, LANES), F32)),
        compiler_params=_params("arbitrary", "arbitrary"),
        name="merge_router",
    )(x, ya, hb, mc, gl, mod, norm2_g.reshape(1, d), who, wmo, wout, rwh, rwl, rbias.reshape(ne, 1))


def _dispatch_kernel(dest_ref, h_ref, xs_ref, sem, *, tm):
    def body(t, carry):
        for k in range(TOP_K):
            pltpu.make_async_copy(h_ref.at[pl.ds(t, 1), :],
                                  xs_ref.at[pl.ds(dest_ref[k, t], 1), :], sem).start()
        return carry

    lax.fori_loop(0, tm, body, 0)
    for k in range(TOP_K):
        pltpu.make_async_copy(h_ref, xs_ref.at[pl.ds(0, tm), :], sem).wait()


def _dispatch(dest, h2, n_rows, tm):
    n_tok, d = h2.shape
    return pl.pallas_call(
        functools.partial(_dispatch_kernel, tm=tm),
        grid=(n_tok // tm,),
        in_specs=[pl.BlockSpec((8, tm), lambda i: (0, i), memory_space=pltpu.SMEM),
                  pl.BlockSpec((tm, d), lambda i: (i, 0))],
        out_specs=pl.BlockSpec(memory_space=pl.ANY),
        out_shape=jax.ShapeDtypeStruct((n_rows, d), h2.dtype),
        scratch_shapes=[pltpu.SemaphoreType.DMA],
        compiler_params=_params("arbitrary"),
        name="moe_dispatch",
    )(dest, h2)


def _expert_kernel(be_ref, nv_ref, x_ref, wg_ref, wu_ref, wd_ref, y_ref, *, rb):
    nv = nv_ref[pl.program_id(0)]

    @pl.when(nv > 0)
    def _():
        rows = lax.broadcasted_iota(I32, (rb, 1), 0)
        x = jnp.where(rows < nv, _unpack_bf16_pairs(x_ref[...]), 0.0).astype(BF16)
        g = _dot(x, wg_ref[...])
        u = _dot(x, wu_ref[...])
        y_ref[...] = _pack_bf16_pairs(_dot((_silu(g) * u).astype(BF16), wd_ref[...]))


def _experts(block_expert, block_rows, xs, wg, wu, wd, rb):
    n_rows = xs.shape[0]
    n_blocks = n_rows // rb
    _, d, de = wg.shape
    live = lambda i, be, nv: jnp.where(nv[i] > 0, i, n_blocks - 1)
    grid_spec = pltpu.PrefetchScalarGridSpec(
        num_scalar_prefetch=2,
        grid=(n_blocks,),
        in_specs=[pl.BlockSpec((rb, d // 2), lambda i, be, nv: (live(i, be, nv), 0)),
                  pl.BlockSpec((None, d, de), lambda i, be, nv: (be[i], 0, 0)),
                  pl.BlockSpec((None, d, de), lambda i, be, nv: (be[i], 0, 0)),
                  pl.BlockSpec((None, de, d), lambda i, be, nv: (be[i], 0, 0))],
        out_specs=pl.BlockSpec((rb, d // 2), lambda i, be, nv: (live(i, be, nv), 0)),
    )
    return pl.pallas_call(
        functools.partial(_expert_kernel, rb=rb),
        grid_spec=grid_spec,
        out_shape=jax.ShapeDtypeStruct((n_rows, d // 2), jnp.uint32),
        compiler_params=_params("arbitrary"),
        name="moe_experts",
    )(block_expert, block_rows, xs, wg, wu, wd)


def _combine_kernel(dest_ref, wgt_ref, x1_ref, h2_ref, mod_ref, sg_ref, su_ref, sd_ref, ys_ref,
                    o_ref, gbuf, sem, *, tm):
    def body(t, carry):
        for k in range(TOP_K):
            pltpu.make_async_copy(ys_ref.at[pl.ds(dest_ref[k, t], 1), :],
                                  gbuf.at[k, pl.ds(t, 1), :], sem).start()
        return carry

    lax.fori_loop(0, tm, body, 0)
    hb = _unpack_bf16_pairs(h2_ref[...]).astype(BF16)
    hid = _silu(_dot(hb, sg_ref[...])) * _dot(hb, su_ref[...])
    y = _dot(hid.astype(BF16), sd_ref[...])
    wcol = wgt_ref[...].T
    for k in range(TOP_K):
        pltpu.make_async_copy(ys_ref.at[pl.ds(0, tm), :], gbuf.at[k], sem).wait()
    for k in range(TOP_K):
        y = y + _unpack_bf16_pairs(gbuf[k]) * wcol[:, k:k + 1]
    o_ref[...] = x1_ref[...] + mod_ref[5:6, :] * y


def _combine(dest, wgt, x1, h2, mod, sg, su, sd, ys, tm):
    bsz, seq, d = x1.shape
    nt = seq // tm
    row = pl.BlockSpec((None, tm, d), lambda b, i: (b, i, 0))
    return pl.pallas_call(
        functools.partial(_combine_kernel, tm=tm),
        grid=(bsz, nt),
        in_specs=[pl.BlockSpec((8, tm), lambda b, i: (0, b * nt + i), memory_space=pltpu.SMEM),
                  pl.BlockSpec((8, tm), lambda b, i: (0, b * nt + i)),
                  row, pl.BlockSpec((None, tm, d // 2), lambda b, i: (b, i, 0)),
                  pl.BlockSpec((None, N_MOD, d), lambda b, i: (b, 0, 0)),
                  _const_spec(sg.shape), _const_spec(su.shape), _const_spec(sd.shape),
                  pl.BlockSpec(memory_space=pl.ANY)],
        out_specs=row,
        out_shape=jax.ShapeDtypeStruct((bsz, seq, d), F32),
        scratch_shapes=[pltpu.VMEM((TOP_K, tm, d // 2), jnp.uint32), pltpu.SemaphoreType.DMA],
        compiler_params=_params("arbitrary", "arbitrary"),
        name="moe_combine",
    )(dest, wgt, x1, h2, mod, sg, su, sd, ys)


def _dest_kernel(ps_ref, idx_ref, pos_ref, o_ref, *, ne):
    idx = idx_ref[...]
    acc = pos_ref[...]
    for e in range(ne):
        acc = acc + jnp.where(idx == e, ps_ref[e], 0)
    o_ref[...] = acc


def _routing_tables(idx, pos, counts, n_tok, ne, rb):
    n_assign = n_tok * TOP_K
    n_blocks = -(-n_assign // rb) + ne
    padded = (counts + rb - 1) // rb * rb
    pend = jnp.cumsum(padded)
    pstart = (pend - padded).astype(I32)
    tt = min(8192, n_tok)
    col = pl.BlockSpec((8, tt), lambda i: (0, i))
    dest = pl.pallas_call(
        functools.partial(_dest_kernel, ne=ne),
        grid=(n_tok // tt,),
        in_specs=[pl.BlockSpec(memory_space=pltpu.SMEM), col, col],
        out_specs=col,
        out_shape=jax.ShapeDtypeStruct((8, n_tok), I32),
        compiler_params=_params("arbitrary"),
        name="moe_dest",
    )(pstart, idx, pos)
    bstart = jnp.arange(n_blocks, dtype=I32) * rb
    bexp = jnp.minimum(jnp.sum((pend[None, :] <= bstart[:, None]).astype(I32), axis=1), ne - 1)
    brows = jnp.clip(counts[bexp] - (bstart - pstart[bexp]), 0, rb).astype(I32)
    return dest, bexp, brows, n_blocks


def _block_diag_in(bbt, g):
    gw, n = bbt.shape
    p = n // g
    rows = jnp.tile(bbt, (g, 1))
    rg = jnp.arange(g * gw)[:, None] // gw
    cg = jnp.arange(n)[None, :] // p
    return jnp.where(rg == cg, rows, 0.0)


def _block_diag_out(c, sign):
    g, gw, p = c.shape
    cols = jnp.tile(c.transpose(0, 2, 1).reshape(g * p, gw), (1, g))
    rg = jnp.arange(g * p)[:, None] // p
    cg = jnp.arange(g * gw)[None, :] // gw
    return jnp.where(rg == cg, sign * cols, 0.0)


def kernel(x, c, ada_w, ada_b, norm1_g, w_in, ssm_log_dt, ssm_lambda_re, ssm_lambda_im, ssm_b_re,
           ssm_b_im, ssm_c_re, ssm_c_im, ssm_d, ssm_w_glu, hgrn_lb_logits, hgrn_norm_g, hgrn_w_o,
           moba_q_norm_g, moba_k_norm_g, moba_w_o, w_out, norm2_g, router_w, router_bias,
           exp_w_gate, exp_w_up, exp_w_down, shared_w_gate, shared_w_up, shared_w_down):
    bsz, seq, d = x.shape
    depth = w_in.shape[0]
    n_tok = bsz * seq
    g = ssm_lambda_re.shape[1]
    d_ssm = ssm_d.shape[-1]
    d_hk = hgrn_lb_logits.shape[-1]
    d_moba = moba_w_o.shape[1]
    ne = router_w.shape[-1]
    widths = (d_ssm, 4 * d_hk, 3 * d_moba, 3 * d)
    tm = min(256, seq)
    rb = 512

    mod_all = _ada_mod(c, ada_w, ada_b).reshape(depth, bsz, N_MOD, d)
    abar, bbr, bbi, lower = _prep_params(ssm_log_dt, ssm_lambda_re, ssm_lambda_im, ssm_b_re, ssm_b_im,
                                         hgrn_lb_logits)

    for l in range(depth):
        mod = mod_all[l]
        u, hg, mo, gl = _inproj(x, mod, norm1_g[l], w_in[l].astype(BF16), widths, tm)
        wb = jnp.concatenate([_block_diag_in(bbr[l], g), _block_diag_in(bbi[l], g)], axis=1).astype(BF16)
        wc = jnp.concatenate([_block_diag_out(ssm_c_re[l], 1.0), _block_diag_out(ssm_c_im[l], -1.0)],
                             axis=0).astype(BF16)
        ya = _s5_mixer(u, wb, wc, abar[l], ssm_d[l], ssm_w_glu[l].astype(BF16), min(128, seq))
        hb = _hgrn_mixer(hg, lower[l], hgrn_norm_g[l], min(256, seq))
        mc = _moba_mixer(mo, moba_q_norm_g[l], moba_k_norm_g[l])
        rwt = router_w[l].T
        rwh = rwt.astype(BF16)
        rwl = (rwt - rwh.astype(F32)).astype(BF16)
        x1, h2, idx, wgt, pos, cnt = _merge_router(
            x, ya, hb, mc, gl, mod, norm2_g[l], hgrn_w_o[l].astype(BF16), moba_w_o[l].astype(BF16),
            w_out[l].astype(BF16), rwh, rwl, router_bias[l], tm)
        counts = cnt[:, 0].astype(I32)
        dest, bexp, brows, n_blocks = _routing_tables(idx, pos, counts, n_tok, ne, rb)
        xs = _dispatch(dest, h2.reshape(n_tok, d // 2), n_blocks * rb, tm)
        ys = _experts(bexp, brows, xs, exp_w_gate[l].astype(BF16), exp_w_up[l].astype(BF16),
                      exp_w_down[l].astype(BF16), rb)
        x = _combine(dest, wgt, x1, h2, mod, shared_w_gate[l].astype(BF16),
                     shared_w_up[l].astype(BF16), shared_w_down[l].astype(BF16), ys, tm)
    return x
```

```python
import functools

import jax
import jax.numpy as jnp
from jax import lax
from jax.experimental import pallas as pl
from jax.experimental.pallas import tpu as pltpu

F32 = jnp.float32
BF16 = jnp.bfloat16
I32 = jnp.int32
HIGHEST = lax.Precision.HIGHEST

N_MOD = 6
EPS = 1e-6
MOBA_BLOCK = 256
MOBA_TOPK = 3
TOP_K = 6
ROUTED_SCALE = 2.5

LANES = 128
HGRN_CHUNK = 64
HGRN_SUB = 16
S5_ROW_PAD = 8
MOBA_SUM_ROWS = 16
MASK_BIG = 30000.0
VMEM_LIMIT = 56 * 1024 * 1024


def _nt_dot(a, b, precision=None):
    return lax.dot_general(a, b, (((1,), (1,)), ((), ())), preferred_element_type=F32,
                           precision=precision)


def _tn_dot(a, b, precision=None):
    return lax.dot_general(a, b, (((0,), (0,)), ((), ())), preferred_element_type=F32,
                           precision=precision)


def _dot(a, b, precision=None):
    return jnp.dot(a, b, preferred_element_type=F32, precision=precision)


def _bf16_terms(x, n):
    terms = []
    for _ in range(n):
        t = x.astype(BF16)
        terms.append(t)
        x = x - t.astype(F32)
    return terms


def _dot_exact_lhs(a, b, n):
    ab = a.astype(BF16)
    return functools.reduce(lambda p, q: p + q, [_dot(ab, t) for t in _bf16_terms(b, n)])


def _dot_exact_rhs(a, b, n):
    bb = b.astype(BF16)
    return functools.reduce(lambda p, q: p + q, [_dot(t, bb) for t in _bf16_terms(a, n)])


def _pack_bf16_pairs(y):
    m = y.shape[1] // 2
    lo = lax.bitcast_convert_type(y[:, :m].astype(BF16).astype(F32), jnp.uint32)
    hi = lax.bitcast_convert_type(y[:, m:].astype(BF16).astype(F32), jnp.uint32)
    return hi | (lo >> 16)


def _unpack_bf16_pairs(w):
    lo = lax.bitcast_convert_type(w << 16, F32)
    hi = lax.bitcast_convert_type(w & jnp.uint32(0xFFFF0000), F32)
    return jnp.concatenate([lo, hi], axis=1)


def _sigmoid(x):
    return 1.0 / (1.0 + jnp.exp(-x))


def _silu(x):
    return x * _sigmoid(x)


def _params(*sem):
    return pltpu.CompilerParams(dimension_semantics=sem, vmem_limit_bytes=VMEM_LIMIT)


def _const_spec(shape):
    nd = len(shape)
    return pl.BlockSpec(shape, lambda *_: (0,) * nd, pipeline_mode=pl.Buffered(1))


def _ada_kernel(c_ref, w_ref, b_ref, o_ref):
    c = c_ref[...]
    o_ref[...] = _dot(_silu(c), w_ref[...], HIGHEST) + b_ref[...]


def _ada_mod(c, ada_w, ada_b):
    depth, d, n = ada_w.shape
    bsz = c.shape[0]
    tn = n // 4
    return pl.pallas_call(
        _ada_kernel,
        grid=(depth, n // tn),
        in_specs=[pl.BlockSpec((bsz, d), lambda l, j: (0, 0)),
                  pl.BlockSpec((None, d, tn), lambda l, j: (l, 0, j)),
                  pl.BlockSpec((None, 1, tn), lambda l, j: (l, 0, j))],
        out_specs=pl.BlockSpec((None, bsz, tn), lambda l, j: (l, 0, j)),
        out_shape=jax.ShapeDtypeStruct((depth, bsz, n), F32),
        compiler_params=_params("arbitrary", "arbitrary"),
        name="ada_mod",
    )(c, ada_w, ada_b.reshape(depth, 1, n))


def _prep_kernel(ldt_ref, lr_ref, li_ref, btr_ref, bti_ref, lbl_ref,
                 abar_ref, bbr_ref, bbi_ref, lb_ref):
    depth = ldt_ref.shape[0]
    for l in range(depth):
        dt = jnp.exp(ldt_ref[l])
        lr = lr_ref[l]
        li = li_ref[l]
        mag = jnp.exp(lr * dt)
        ar = mag * jnp.cos(li * dt)
        ai = mag * jnp.sin(li * dt)
        inv = 1.0 / (lr * lr + li * li)
        cr = ((ar - 1.0) * lr + ai * li) * inv
        ci = (ai * lr - (ar - 1.0) * li) * inv
        abar_ref[l, 0:1, :] = ar
        abar_ref[l, 1:2, :] = ai
        btr = btr_ref[l]
        bti = bti_ref[l]
        bbr_ref[l] = cr * btr - ci * bti
        bbi_ref[l] = cr * bti + ci * btr
    rows = [lbl_ref[l:l + 1, :] for l in range(depth)]
    mx = functools.reduce(jnp.maximum, rows)
    ex = [jnp.exp(r - mx) for r in rows]
    den = functools.reduce(lambda a, b: a + b, ex)
    run = ex[0] / den
    first = run
    lb_ref[0:1, :] = run - first
    for l in range(1, depth):
        run = run + ex[l] / den
        lb_ref[l:l + 1, :] = run - first


def _prep_params(log_dt, lam_re, lam_im, b_re, b_im, lb_logits):
    depth, g, p = lam_re.shape
    gw = b_re.shape[-1]
    n = g * p
    hk = lb_logits.shape[-1]
    ldt = jnp.repeat(log_dt, p, axis=1).reshape(depth, 1, n)
    btr = b_re.transpose(0, 3, 1, 2).reshape(depth, gw, n)
    bti = b_im.transpose(0, 3, 1, 2).reshape(depth, gw, n)
    return pl.pallas_call(
        _prep_kernel,
        out_shape=(jax.ShapeDtypeStruct((depth, 2, n), F32),
                   jax.ShapeDtypeStruct((depth, gw, n), F32),
                   jax.ShapeDtypeStruct((depth, gw, n), F32),
                   jax.ShapeDtypeStruct((depth, hk), F32)),
        name="param_prep",
    )(ldt, lam_re.reshape(depth, 1, n), lam_im.reshape(depth, 1, n), btr, bti, lb_logits)


def _inproj_kernel(x_ref, mod_ref, g_ref, w_ref, u_ref, hg_ref, mo_ref, gl_ref, *, widths):
    x = x_ref[...]
    ms = jnp.mean(x * x, axis=-1, keepdims=True)
    h = x * lax.rsqrt(ms + EPS) * g_ref[...]
    h = h * (1.0 + mod_ref[1:2, :]) + mod_ref[0:1, :]
    hb = h.astype(BF16)
    d_ssm, d_hgrn, d_moba, d_gate = widths
    o0 = 0
    u_ref[...] = _dot(hb, w_ref[:, o0:o0 + d_ssm])
    o0 += d_ssm
    hg_ref[...] = _dot(hb, w_ref[:, o0:o0 + d_hgrn])
    o0 += d_hgrn
    mo_ref[...] = _dot(hb, w_ref[:, o0:o0 + d_moba])
    o0 += d_moba
    step = 1024
    for j in range(0, d_gate, step):
        gl_ref[:, j:j + step] = _dot(hb, w_ref[:, o0 + j:o0 + j + step]).astype(BF16)


def _inproj(x, mod, norm_g, w_in_bf, widths, tm):
    bsz, seq, d = x.shape
    d_ssm, d_hgrn, d_moba, d_gate = widths
    d_in = w_in_bf.shape[1]
    row = lambda n: pl.BlockSpec((None, tm, n), lambda b, i: (b, i, 0))
    return pl.pallas_call(
        functools.partial(_inproj_kernel, widths=widths),
        grid=(bsz, seq // tm),
        in_specs=[row(d),
                  pl.BlockSpec((None, N_MOD, d), lambda b, i: (b, 0, 0)),
                  _const_spec((1, d)),
                  _const_spec((d, d_in))],
        out_specs=[row(d_ssm), row(d_hgrn), row(d_moba), row(d_gate)],
        out_shape=(jax.ShapeDtypeStruct((bsz, seq, d_ssm), F32),
                   jax.ShapeDtypeStruct((bsz, seq, d_hgrn), F32),
                   jax.ShapeDtypeStruct((bsz, seq, d_moba), F32),
                   jax.ShapeDtypeStruct((bsz, seq, d_gate), BF16)),
        compiler_params=_params("arbitrary", "arbitrary"),
        name="in_proj",
    )(x, mod, norm_g.reshape(1, d), w_in_bf)


def _s5_kernel(u_ref, wb_ref, wc_ref, abar_ref, d_ref, wglu_ref, o_ref, bu_ref, st_ref,
               *, nb, tl, ns, unroll):
    nc = ns // LANES
    rs = tl + S5_ROW_PAD

    @pl.when(pl.program_id(0) == 0)
    def _():
        st_ref[...] = jnp.zeros_like(st_ref)

    for b in range(nb):
        bu = _dot(u_ref[b].astype(BF16), wb_ref[...])
        for c in range(2 * nc):
            bu_ref[c, b * rs:b * rs + tl, :] = bu[:, c * LANES:(c + 1) * LANES]

    ar = [jnp.broadcast_to(abar_ref[0:1, c * LANES:(c + 1) * LANES], (nb, LANES)) for c in range(nc)]
    ai = [jnp.broadcast_to(abar_ref[1:2, c * LANES:(c + 1) * LANES], (nb, LANES)) for c in range(nc)]

    def outer(j, carry):
        sr, si = list(carry[:nc]), list(carry[nc:])
        for k in range(unroll):
            t = j * unroll + k
            for c in range(nc):
                br = bu_ref[c, pl.ds(t, nb, stride=rs), :]
                bi = bu_ref[nc + c, pl.ds(t, nb, stride=rs), :]
                nr = ar[c] * sr[c] - ai[c] * si[c] + br
                ni = ar[c] * si[c] + ai[c] * sr[c] + bi
                bu_ref[c, pl.ds(t, nb, stride=rs), :] = nr
                bu_ref[nc + c, pl.ds(t, nb, stride=rs), :] = ni
                sr[c], si[c] = nr, ni
        return tuple(sr) + tuple(si)

    init = tuple(st_ref[c] for c in range(2 * nc))
    fin = lax.fori_loop(0, tl // unroll, outer, init)
    for c in range(2 * nc):
        st_ref[c] = fin[c]

    dm = o_ref.shape[-1]
    for b in range(nb):
        s = jnp.concatenate([bu_ref[c, b * rs:b * rs + tl, :] for c in range(2 * nc)], axis=1)
        y = _dot(s.astype(BF16), wc_ref[...]) + d_ref[...] * u_ref[b]
        z = _dot(jax.nn.gelu(y).astype(BF16), wglu_ref[...])
        o_ref[b] = (z[:, :dm] * _sigmoid(z[:, dm:])).astype(o_ref.dtype)


def _s5_mixer(u, wb, wc, abar, d_skip, wglu, tl):
    bsz, seq, d_ssm = u.shape
    ns = abar.shape[-1]
    dm = wglu.shape[1] // 2
    return pl.pallas_call(
        functools.partial(_s5_kernel, nb=bsz, tl=tl, ns=ns, unroll=8),
        grid=(seq // tl,),
        in_specs=[pl.BlockSpec((bsz, tl, d_ssm), lambda i: (0, i, 0)),
                  _const_spec(wb.shape), _const_spec(wc.shape), _const_spec(abar.shape),
                  _const_spec((1, d_ssm)), _const_spec(wglu.shape)],
        out_specs=pl.BlockSpec((bsz, tl, dm), lambda i: (0, i, 0)),
        out_shape=jax.ShapeDtypeStruct((bsz, seq, dm), BF16),
        scratch_shapes=[pltpu.VMEM((2 * ns // LANES, bsz * (tl + S5_ROW_PAD), LANES), F32),
                        pltpu.VMEM((2 * ns // LANES, bsz, LANES), F32)],
        compiler_params=_params("arbitrary"),
        name="s5_mixer",
    )(u, wb, wc, abar, d_skip.reshape(1, d_ssm), wglu)


def _hgrn_kernel(x_ref, lb_ref, ng_ref, o_ref, st_ref, *, tl, hd, dk):
    cs, sb = HGRN_CHUNK, HGRN_SUB

    @pl.when(pl.program_id(1) == 0)
    def _():
        st_ref[...] = jnp.zeros_like(st_ref)

    def iota(shape, dim):
        return lax.broadcasted_iota(I32, shape, dim)

    tril = (iota((cs, cs), 0) >= iota((cs, cs), 1)).astype(F32)
    head_eq = (iota((hd, hd), 0) // dk) == (iota((hd, hd), 1) // dk)
    block_ones = head_eq.astype(BF16)
    block_mean = head_eq.astype(F32) * (1.0 / dk)
    sel = ((iota((sb, sb * sb), 1) // sb) == iota((sb, sb * sb), 0)).astype(BF16)
    s_iota = iota((sb, hd), 0)
    lane_head = iota((1, hd), 1) // dk
    n_heads = hd // dk
    lb = lb_ref[...]
    ng = ng_ref[...]

    def head_copies(a):
        return jnp.concatenate([jnp.where(lane_head == h, a, 0.0) for h in range(n_heads)], axis=0)

    def off_diag_scores(q, key, cum, t0, t1, s0, s1):
        r = cum[s1 - 1:s1, :]
        qt = q[t0:t1] * jnp.exp(cum[t0:t1] - r)
        ks = key[s0:s1] * jnp.exp(r - cum[s0:s1])
        return _nt_dot(qt.astype(BF16), head_copies(ks).astype(BF16))

    nch = tl // cs
    half = cs // 2
    spans = ((half, cs, 0, half), (sb, half, 0, sb), (half + sb, cs, half, half + sb))
    pre = []
    for c in range(nch):
        r0 = c * cs
        q = x_ref[r0:r0 + cs, 0:hd]
        fr = x_ref[r0:r0 + cs, hd:2 * hd]
        iv = x_ref[r0:r0 + cs, 2 * hd:3 * hd]
        forget = lb + (1.0 - lb) * _sigmoid(fr)
        key = 1.0 - forget
        cum = _dot_exact_lhs(tril, jnp.log(forget), 3)
        pre.append((q, key, iv, cum))
    heads_sum = []
    for q, key, iv, cum in pre:
        for a in range(cs // sb):
            lo = a * sb
            cum_a, qa, ka = cum[lo:lo + sb], q[lo:lo + sb], key[lo:lo + sb]
            slabs = []
            for t in range(sb):
                arg = jnp.where(s_iota <= t, cum_a[t:t + 1] - cum_a, -jnp.inf)
                slabs.append(jnp.exp(arg) * (qa[t:t + 1] * ka))
            heads_sum.append(_dot(jnp.concatenate(slabs, axis=0).astype(BF16), block_ones))
    off_scores = [[off_diag_scores(q, key, cum, *sp) for sp in spans] for q, key, iv, cum in pre]
    intras = []
    for c, (q, key, iv, cum) in enumerate(pre):
        parts = []
        for a in range(cs // sb):
            lo = a * sb
            w = heads_sum[c * (cs // sb) + a] * jnp.concatenate([iv[lo:lo + sb]] * sb, axis=0)
            parts.append(_dot(sel, w.astype(BF16)))
        lvl1, lvl2a, lvl2b = [
            _dot(sc.astype(BF16), head_copies(iv[sp[2]:sp[3]]).astype(BF16))
            for sc, sp in zip(off_scores[c], spans)]
        zeros = jnp.zeros((sb, hd), F32)
        intras.append(jnp.concatenate(parts, axis=0)
                      + jnp.concatenate([zeros, lvl2a, lvl1[0:sb], lvl1[sb:half] + lvl2b], axis=0))
    upds, decays, qcs = [], [], []
    for q, key, iv, cum in pre:
        last = cum[cs - 1:cs, :]
        kc = key * jnp.exp(last - cum)
        upds.append(jnp.where(head_eq, _tn_dot(iv.astype(BF16), kc.astype(BF16)), 0.0))
        decays.append(jnp.exp(last))
        qcs.append((q * jnp.exp(cum)).astype(BF16))
    st = st_ref[...]
    for c in range(nch):
        out = intras[c] + _nt_dot(qcs[c], st.astype(BF16))
        st = st * decays[c] + upds[c]
        ms = _dot_exact_rhs(out * out, block_mean, 2)
        g = x_ref[c * cs:(c + 1) * cs, 3 * hd:4 * hd]
        o_ref[c * cs:(c + 1) * cs, :] = (out * lax.rsqrt(ms + EPS) * ng * _silu(g)).astype(o_ref.dtype)
    st_ref[...] = st


def _hgrn_mixer(hg, lb, norm_g, tl):
    bsz, seq, d4 = hg.shape
    hd = d4 // 4
    dk = norm_g.shape[-1]
    ng = jnp.tile(norm_g, hd // dk).reshape(1, hd)
    return pl.pallas_call(
        functools.partial(_hgrn_kernel, tl=tl, hd=hd, dk=dk),
        grid=(bsz, seq // tl),
        in_specs=[pl.BlockSpec((None, tl, d4), lambda b, i: (b, i, 0)),
                  _const_spec((1, hd)), _const_spec((1, hd))],
        out_specs=pl.BlockSpec((None, tl, hd), lambda b, i: (b, i, 0)),
        out_shape=jax.ShapeDtypeStruct((bsz, seq, hd), BF16),
        scratch_shapes=[pltpu.VMEM((hd, hd), F32)],
        compiler_params=_params("arbitrary", "arbitrary"),
        name="hgrn_mixer",
    )(hg, lb.reshape(1, hd), ng)


def _moba_prep_kernel(x_ref, gq_ref, gk_ref, qt_ref, ka_ref, km_ref, vt_ref, *, nh, dh):
    i = pl.program_id(1)
    blk = MOBA_BLOCK
    hd = nh * dh

    @pl.when(i == 0)
    def _():
        km_ref[...] = jnp.zeros_like(km_ref)

    lane = lax.broadcasted_iota(I32, (blk, LANES - dh), 1)
    onehot = (lane == i).astype(F32)
    pad = jnp.zeros((blk, LANES - dh), F32)
    scale = dh ** -0.5
    for h in range(nh):
        qh = x_ref[:, h * dh:(h + 1) * dh]
        kh = x_ref[:, hd + h * dh:hd + (h + 1) * dh]
        vh = x_ref[:, 2 * hd + h * dh:2 * hd + (h + 1) * dh]
        qn = qh * lax.rsqrt(jnp.mean(qh * qh, axis=-1, keepdims=True) + EPS) * gq_ref[...] * scale
        kn = kh * lax.rsqrt(jnp.mean(kh * kh, axis=-1, keepdims=True) + EPS) * gk_ref[...]
        qt_ref[h] = jnp.concatenate([qn, pad], axis=1).T
        ka_ref[h] = jnp.concatenate([kn, onehot], axis=1).astype(BF16)
        kmean = jnp.mean(kn, axis=0, keepdims=True)
        km_ref[h, pl.ds(i, 1), :] = jnp.concatenate([kmean, jnp.zeros((1, LANES - dh), F32)], axis=1)
        vt_ref[h] = jnp.concatenate([vh, pad + 1.0], axis=1).T[0:dh + MOBA_SUM_ROWS].astype(BF16)


def _moba_kernel(qt_ref, ka_ref, km_ref, vt_ref, o_ref, qa_scr, m_scr, acc_scr, *, nh, dh, nbp):
    i = pl.program_id(1)
    blk = MOBA_BLOCK
    r0 = pl.multiple_of(i * blk, blk)
    b_iota = lax.broadcasted_iota(I32, (nbp, blk), 0)
    key_i = lax.broadcasted_iota(I32, (blk, blk), 0)
    qry_i = lax.broadcasted_iota(I32, (blk, blk), 1)

    for h in range(nh):
        qt = qt_ref[h]
        gate = _dot(km_ref[h], qt, HIGHEST)
        work = jnp.where(b_iota < i, gate, -jnp.inf)
        chosen = jnp.zeros((nbp, blk), jnp.bool_)
        for _ in range(MOBA_TOPK):
            mx = jnp.max(work, axis=0, keepdims=True)
            first = jnp.min(jnp.where(work == mx, b_iota, nbp), axis=0, keepdims=True)
            hit = (b_iota == first) & (mx > -jnp.inf)
            chosen = chosen | hit
            work = jnp.where(hit, -jnp.inf, work)
        pen = jnp.where(chosen, 0.0, -MASK_BIG)
        qa_scr[h] = jnp.concatenate(
            [qt[0:dh], pen, jnp.zeros((LANES - dh - nbp, blk), F32)], axis=0).astype(BF16)
        s = _dot(ka_ref[h, pl.ds(r0, blk), :], qt.astype(BF16))
        s = jnp.where(key_i <= qry_i, s, -jnp.inf)
        m0 = jnp.max(s, axis=0, keepdims=True)
        m_scr[h] = m0
        acc_scr[h] = _dot(vt_ref[h, i], jnp.exp(s - m0).astype(BF16))

    def past_blocks(js):
        scores = [[_dot(ka_ref[h, pl.ds(pl.multiple_of(j * blk, blk), blk), :], qa_scr[h]) for j in js]
                  for h in range(nh)]
        probs, alphas = [], []
        for h in range(nh):
            m_old = m_scr[h]
            m_new = m_old
            for s in scores[h]:
                m_new = jnp.maximum(m_new, jnp.max(s, axis=0, keepdims=True))
            probs.append([jnp.exp(s - m_new).astype(BF16) for s in scores[h]])
            alphas.append(jnp.exp(m_old - m_new))
            m_scr[h] = m_new
        for h in range(nh):
            acc = alphas[h] * acc_scr[h]
            for j, p in zip(js, probs[h]):
                acc = acc + _dot(vt_ref[h, j], p)
            acc_scr[h] = acc

    group = 4

    def past_group(jj, carry):
        past_blocks(tuple(group * jj + k for k in range(group)))
        return carry

    lax.fori_loop(0, i // group, past_group, 0)
    done = (i // group) * group
    for width in (2, 1):
        has = (i % (2 * width)) >= width

        @pl.when(has)
        def _(done=done, width=width):
            past_blocks(tuple(done + k for k in range(width)))

        done = done + jnp.where(has, width, 0)

    for h in range(nh):
        acc = acc_scr[h]
        o_ref[h * dh:(h + 1) * dh, :] = (acc[0:dh] / acc[dh:dh + 1]).astype(o_ref.dtype)


def _moba_mixer(mo, gq, gk):
    bsz, seq, d3 = mo.shape
    hd = d3 // 3
    dh = gq.shape[-1]
    nh = hd // dh
    blk = MOBA_BLOCK
    nblk = seq // blk
    nbp = -(-nblk // 8) * 8
    vr = dh + MOBA_SUM_ROWS
    assert seq % blk == 0 and dh + nbp <= LANES
    qt, ka, km, vt = pl.pallas_call(
        functools.partial(_moba_prep_kernel, nh=nh, dh=dh),
        grid=(bsz, nblk),
        in_specs=[pl.BlockSpec((None, blk, d3), lambda b, i: (b, i, 0)),
                  _const_spec((1, dh)), _const_spec((1, dh))],
        out_specs=[pl.BlockSpec((None, nh, LANES, blk), lambda b, i: (b, 0, 0, i)),
                   pl.BlockSpec((None, nh, blk, LANES), lambda b, i: (b, 0, i, 0)),
                   pl.BlockSpec((None, nh, nbp, LANES), lambda b, i: (b, 0, 0, 0)),
                   pl.BlockSpec((None, nh, None, vr, blk), lambda b, i: (b, 0, i, 0, 0))],
        out_shape=(jax.ShapeDtypeStruct((bsz, nh, LANES, seq), F32),
                   jax.ShapeDtypeStruct((bsz, nh, seq, LANES), BF16),
                   jax.ShapeDtypeStruct((bsz, nh, nbp, LANES), F32),
                   jax.ShapeDtypeStruct((bsz, nh, nblk, vr, blk), BF16)),
        compiler_params=_params("arbitrary", "arbitrary"),
        name="moba_prep",
    )(mo, gq.reshape(1, dh), gk.reshape(1, dh))
    return pl.pallas_call(
        functools.partial(_moba_kernel, nh=nh, dh=dh, nbp=nbp),
        grid=(bsz, nblk),
        in_specs=[pl.BlockSpec((None, nh, LANES, blk), lambda b, i: (b, 0, 0, i)),
                  pl.BlockSpec((None, nh, seq, LANES), lambda b, i: (b, 0, 0, 0)),
                  pl.BlockSpec((None, nh, nbp, LANES), lambda b, i: (b, 0, 0, 0)),
                  pl.BlockSpec((None, nh, nblk, vr, blk), lambda b, i: (b, 0, 0, 0, 0))],
        out_specs=pl.BlockSpec((None, hd, blk), lambda b, i: (b, 0, i)),
        out_shape=jax.ShapeDtypeStruct((bsz, hd, seq), BF16),
        scratch_shapes=[pltpu.VMEM((nh, LANES, blk), BF16),
                        pltpu.VMEM((nh, 1, blk), F32),
                        pltpu.VMEM((nh, vr, blk), F32)],
        compiler_params=_params("arbitrary", "arbitrary"),
        name="moba_attn",
    )(qt, ka, km, vt)


def _merge_kernel(x_ref, ya_ref, hb_ref, mc_ref, gl_ref, mod_ref, g2_ref, who_ref, wmo_ref, wout_ref,
                  rwh_ref, rwl_ref, rb_ref,
                  x1_ref, h2_ref, idx_ref, wgt_ref, pos_ref, cnt_ref, *, tm, ne):
    first = (pl.program_id(0) == 0) & (pl.program_id(1) == 0)

    @pl.when(first)
    def _():
        cnt_ref[...] = jnp.zeros_like(cnt_ref)

    d = x_ref.shape[-1]
    yb = _dot(hb_ref[...], who_ref[...])
    yc = _tn_dot(mc_ref[...], wmo_ref[...])
    merged = (_sigmoid(gl_ref[:, 0:d].astype(F32)) * ya_ref[...].astype(F32)
              + _sigmoid(gl_ref[:, d:2 * d].astype(F32)) * yb
              + _sigmoid(gl_ref[:, 2 * d:3 * d].astype(F32)) * yc)
    x1 = x_ref[...] + mod_ref[2:3, :] * _dot(merged.astype(BF16), wout_ref[...])
    x1_ref[...] = x1
    ms = jnp.mean(x1 * x1, axis=-1, keepdims=True)
    h2 = x1 * lax.rsqrt(ms + EPS) * g2_ref[...]
    h2 = h2 * (1.0 + mod_ref[4:5, :]) + mod_ref[3:4, :]
    h2_ref[...] = _pack_bf16_pairs(h2)
    hi = h2.astype(BF16)
    lo = (h2 - hi.astype(F32)).astype(BF16)
    logits = _nt_dot(rwh_ref[...], hi) + _nt_dot(rwh_ref[...], lo) + _nt_dot(rwl_ref[...], hi)
    scores = _sigmoid(logits)
    e_iota = lax.broadcasted_iota(I32, (ne, tm), 0)
    work = scores + rb_ref[...]
    picked = jnp.zeros((ne, tm), F32)
    hits, tops = [], []
    for _ in range(TOP_K):
        mx = jnp.max(work, axis=0, keepdims=True)
        first_e = jnp.min(jnp.where(work == mx, e_iota, ne), axis=0, keepdims=True)
        hit = e_iota == first_e
        hits.append((hit, first_e))
        tops.append(jnp.sum(jnp.where(hit, scores, 0.0), axis=0, keepdims=True))
        work = jnp.where(hit, -jnp.inf, work)
        picked = jnp.where(hit, 1.0, picked)
    total = functools.reduce(lambda a, b: a + b, tops)
    upper = (lax.broadcasted_iota(I32, (tm, tm), 0) < lax.broadcasted_iota(I32, (tm, tm), 1)).astype(BF16)
    rank = _dot(picked.astype(BF16), upper) + cnt_ref[:, 0:1]
    zero_row = jnp.zeros((1, tm), F32)
    idx_rows, w_rows, pos_rows = [], [], []
    for (hit, first_e), top in zip(hits, tops):
        idx_rows.append(first_e)
        w_rows.append(top / total * ROUTED_SCALE)
        pos_rows.append(jnp.sum(jnp.where(hit, rank, 0.0), axis=0, keepdims=True))
    pad = 8 - TOP_K
    idx_ref[...] = jnp.concatenate(idx_rows + [jnp.zeros((1, tm), I32)] * pad, axis=0)
    wgt_ref[...] = jnp.concatenate(w_rows + [zero_row] * pad, axis=0)
    pos_ref[...] = jnp.concatenate(pos_rows + [zero_row] * pad, axis=0).astype(I32)
    cnt_ref[...] = cnt_ref[...] + jnp.sum(picked, axis=1, keepdims=True)


def _merge_router(x, ya, hb, mc, gl, mod, norm2_g, who, wmo, wout, rwh, rwl, rbias, tm):
    bsz, seq, d = x.shape
    ne = rwh.shape[0]
    nt = seq // tm
    n_tok = bsz * seq
    row = lambda n: pl.BlockSpec((None, tm, n), lambda b, i: (b, i, 0))
    col = pl.BlockSpec((8, tm), lambda b, i: (0, b * nt + i))
    return pl.pallas_call(
        functools.partial(_merge_kernel, tm=tm, ne=ne),
        grid=(bsz, nt),
        in_specs=[row(d), row(d), row(hb.shape[-1]),
                  pl.BlockSpec((None, mc.shape[1], tm), lambda b, i: (b, 0, i)), row(gl.shape[-1]),
                  pl.BlockSpec((None, N_MOD, d), lambda b, i: (b, 0, 0)),
                  _const_spec((1, d)), _const_spec(who.shape), _const_spec(wmo.shape),
                  _const_spec(wout.shape), _const_spec(rwh.shape), _const_spec(rwl.shape),
                  _const_spec((ne, 1))],
        out_specs=[row(d), row(d // 2), col, col, col,
                   pl.BlockSpec((ne, LANES), lambda b, i: (0, 0))],
        out_shape=(jax.ShapeDtypeStruct((bsz, seq, d), F32),
                   jax.ShapeDtypeStruct((bsz, seq, d // 2), jnp.uint32),
                   jax.ShapeDtypeStruct((8, n_tok), I32),
                   jax.ShapeDtypeStruct((8, n_tok), F32),
                   jax.ShapeDtypeStruct((8, n_tok), I32),
                   jax.ShapeDtypeStruct((ne, LANES), F32)),
        compiler_params=_params("arbitrary", "arbitrary"),
        name="merge_router",
    )(x, ya, hb, mc, gl, mod, norm2_g.reshape(1, d), who, wmo, wout, rwh, rwl, rbias.reshape(ne, 1))


def _dispatch_kernel(dest_ref, h_ref, xs_ref, sem, *, tm):
    def body(t, carry):
        for k in range(TOP_K):
            pltpu.make_async_copy(h_ref.at[pl.ds(t, 1), :],
                                  xs_ref.at[pl.ds(dest_ref[k, t], 1), :], sem).start()
        return carry

    lax.fori_loop(0, tm, body, 0)
    for k in range(TOP_K):
        pltpu.make_async_copy(h_ref, xs_ref.at[pl.ds(0, tm), :], sem).wait()


def _dispatch(dest, h2, n_rows, tm):
    n_tok, d = h2.shape
    return pl.pallas_call(
        functools.partial(_dispatch_kernel, tm=tm),
        grid=(n_tok // tm,),
        in_specs=[pl.BlockSpec((8, tm), lambda i: (0, i), memory_space=pltpu.SMEM),
                  pl.BlockSpec((tm, d), lambda i: (i, 0))],
        out_specs=pl.BlockSpec(memory_space=pl.ANY),
        out_shape=jax.ShapeDtypeStruct((n_rows, d), h2.dtype),
        scratch_shapes=[pltpu.SemaphoreType.DMA],
        compiler_params=_params("arbitrary"),
        name="moe_dispatch",
    )(dest, h2)


def _expert_kernel(be_ref, nv_ref, x_ref, wg_ref, wu_ref, wd_ref, y_ref, *, rb):
    nv = nv_ref[pl.program_id(0)]

    @pl.when(nv > 0)
    def _():
        rows = lax.broadcasted_iota(I32, (rb, 1), 0)
        x = jnp.where(rows < nv, _unpack_bf16_pairs(x_ref[...]), 0.0).astype(BF16)
        g = _dot(x, wg_ref[...])
        u = _dot(x, wu_ref[...])
        y_ref[...] = _pack_bf16_pairs(_dot((_silu(g) * u).astype(BF16), wd_ref[...]))


def _experts(block_expert, block_rows, xs, wg, wu, wd, rb):
    n_rows = xs.shape[0]
    n_blocks = n_rows // rb
    _, d, de = wg.shape
    live = lambda i, be, nv: jnp.where(nv[i] > 0, i, n_blocks - 1)
    grid_spec = pltpu.PrefetchScalarGridSpec(
        num_scalar_prefetch=2,
        grid=(n_blocks,),
        in_specs=[pl.BlockSpec((rb, d // 2), lambda i, be, nv: (live(i, be, nv), 0)),
                  pl.BlockSpec((None, d, de), lambda i, be, nv: (be[i], 0, 0)),
                  pl.BlockSpec((None, d, de), lambda i, be, nv: (be[i], 0, 0)),
                  pl.BlockSpec((None, de, d), lambda i, be, nv: (be[i], 0, 0))],
        out_specs=pl.BlockSpec((rb, d // 2), lambda i, be, nv: (live(i, be, nv), 0)),
    )
    return pl.pallas_call(
        functools.partial(_expert_kernel, rb=rb),
        grid_spec=grid_spec,
        out_shape=jax.ShapeDtypeStruct((n_rows, d // 2), jnp.uint32),
        compiler_params=_params("arbitrary"),
        name="moe_experts",
    )(block_expert, block_rows, xs, wg, wu, wd)


def _combine_kernel(dest_ref, wgt_ref, x1_ref, h2_ref, mod_ref, sg_ref, su_ref, sd_ref, ys_ref,
                    o_ref, gbuf, sem, *, tm):
    def body(t, carry):
        for k in range(TOP_K):
            pltpu.make_async_copy(ys_ref.at[pl.ds(dest_ref[k, t], 1), :],
                                  gbuf.at[k, pl.ds(t, 1), :], sem).start()
        return carry

    lax.fori_loop(0, tm, body, 0)
    hb = _unpack_bf16_pairs(h2_ref[...]).astype(BF16)
    hid = _silu(_dot(hb, sg_ref[...])) * _dot(hb, su_ref[...])
    y = _dot(hid.astype(BF16), sd_ref[...])
    wcol = wgt_ref[...].T
    for k in range(TOP_K):
        pltpu.make_async_copy(ys_ref.at[pl.ds(0, tm), :], gbuf.at[k], sem).wait()
    for k in range(TOP_K):
        y = y + _unpack_bf16_pairs(gbuf[k]) * wcol[:, k:k + 1]
    o_ref[...] = x1_ref[...] + mod_ref[5:6, :] * y


def _combine(dest, wgt, x1, h2, mod, sg, su, sd, ys, tm):
    bsz, seq, d = x1.shape
    nt = seq // tm
    row = pl.BlockSpec((None, tm, d), lambda b, i: (b, i, 0))
    return pl.pallas_call(
        functools.partial(_combine_kernel, tm=tm),
        grid=(bsz, nt),
        in_specs=[pl.BlockSpec((8, tm), lambda b, i: (0, b * nt + i), memory_space=pltpu.SMEM),
                  pl.BlockSpec((8, tm), lambda b, i: (0, b * nt + i)),
                  row, pl.BlockSpec((None, tm, d // 2), lambda b, i: (b, i, 0)),
                  pl.BlockSpec((None, N_MOD, d), lambda b, i: (b, 0, 0)),
                  _const_spec(sg.shape), _const_spec(su.shape), _const_spec(sd.shape),
                  pl.BlockSpec(memory_space=pl.ANY)],
        out_specs=row,
        out_shape=jax.ShapeDtypeStruct((bsz, seq, d), F32),
        scratch_shapes=[pltpu.VMEM((TOP_K, tm, d // 2), jnp.uint32), pltpu.SemaphoreType.DMA],
        compiler_params=_params("arbitrary", "arbitrary"),
        name="moe_combine",
    )(dest, wgt, x1, h2, mod, sg, su, sd, ys)


def _dest_kernel(ps_ref, idx_ref, pos_ref, o_ref, *, ne):
    idx = idx_ref[...]
    acc = pos_ref[...]
    for e in range(ne):
        acc = acc + jnp.where(idx == e, ps_ref[e], 0)
    o_ref[...] = acc


def _routing_tables(idx, pos, counts, n_tok, ne, rb):
    n_assign = n_tok * TOP_K
    n_blocks = -(-n_assign // rb) + ne
    padded = (counts + rb - 1) // rb * rb
    pend = jnp.cumsum(padded)
    pstart = (pend - padded).astype(I32)
    tt = min(8192, n_tok)
    col = pl.BlockSpec((8, tt), lambda i: (0, i))
    dest = pl.pallas_call(
        functools.partial(_dest_kernel, ne=ne),
        grid=(n_tok // tt,),
        in_specs=[pl.BlockSpec(memory_space=pltpu.SMEM), col, col],
        out_specs=col,
        out_shape=jax.ShapeDtypeStruct((8, n_tok), I32),
        compiler_params=_params("arbitrary"),
        name="moe_dest",
    )(pstart, idx, pos)
    bstart = jnp.arange(n_blocks, dtype=I32) * rb
    bexp = jnp.minimum(jnp.sum((pend[None, :] <= bstart[:, None]).astype(I32), axis=1), ne - 1)
    brows = jnp.clip(counts[bexp] - (bstart - pstart[bexp]), 0, rb).astype(I32)
    return dest, bexp, brows, n_blocks


def _block_diag_in(bbt, g):
    gw, n = bbt.shape
    p = n // g
    rows = jnp.tile(bbt, (g, 1))
    rg = jnp.arange(g * gw)[:, None] // gw
    cg = jnp.arange(n)[None, :] // p
    return jnp.where(rg == cg, rows, 0.0)


def _block_diag_out(c, sign):
    g, gw, p = c.shape
    cols = jnp.tile(c.transpose(0, 2, 1).reshape(g * p, gw), (1, g))
    rg = jnp.arange(g * p)[:, None] // p
    cg = jnp.arange(g * gw)[None, :] // gw
    return jnp.where(rg == cg, sign * cols, 0.0)


def kernel(x, c, ada_w, ada_b, norm1_g, w_in, ssm_log_dt, ssm_lambda_re, ssm_lambda_im, ssm_b_re,
           ssm_b_im, ssm_c_re, ssm_c_im, ssm_d, ssm_w_glu, hgrn_lb_logits, hgrn_norm_g, hgrn_w_o,
           moba_q_norm_g, moba_k_norm_g, moba_w_o, w_out, norm2_g, router_w, router_bias,
           exp_w_gate, exp_w_up, exp_w_down, shared_w_gate, shared_w_up, shared_w_down):
    bsz, seq, d = x.shape
    depth = w_in.shape[0]
    n_tok = bsz * seq
    g = ssm_lambda_re.shape[1]
    d_ssm = ssm_d.shape[-1]
    d_hk = hgrn_lb_logits.shape[-1]
    d_moba = moba_w_o.shape[1]
    ne = router_w.shape[-1]
    widths = (d_ssm, 4 * d_hk, 3 * d_moba, 3 * d)
    tm = min(256, seq)
    rb = 512

    mod_all = _ada_mod(c, ada_w, ada_b).reshape(depth, bsz, N_MOD, d)
    abar, bbr, bbi, lower = _prep_params(ssm_log_dt, ssm_lambda_re, ssm_lambda_im, ssm_b_re, ssm_b_im,
                                         hgrn_lb_logits)

    for l in range(depth):
        mod = mod_all[l]
        u, hg, mo, gl = _inproj(x, mod, norm1_g[l], w_in[l].astype(BF16), widths, min(512, seq))
        wb = jnp.concatenate([_block_diag_in(bbr[l], g), _block_diag_in(bbi[l], g)], axis=1).astype(BF16)
        wc = jnp.concatenate([_block_diag_out(ssm_c_re[l], 1.0), _block_diag_out(ssm_c_im[l], -1.0)],
                             axis=0).astype(BF16)
        ya = _s5_mixer(u, wb, wc, abar[l], ssm_d[l], ssm_w_glu[l].astype(BF16), min(128, seq))
        hb = _hgrn_mixer(hg, lower[l], hgrn_norm_g[l], min(256, seq))
        mc = _moba_mixer(mo, moba_q_norm_g[l], moba_k_norm_g[l])
        rwt = router_w[l].T
        rwh = rwt.astype(BF16)
        rwl = (rwt - rwh.astype(F32)).astype(BF16)
        x1, h2, idx, wgt, pos, cnt = _merge_router(
            x, ya, hb, mc, gl, mod, norm2_g[l], hgrn_w_o[l].astype(BF16), moba_w_o[l].astype(BF16),
            w_out[l].astype(BF16), rwh, rwl, router_bias[l], tm)
        counts = cnt[:, 0].astype(I32)
        dest, bexp, brows, n_blocks = _routing_tables(idx, pos, counts, n_tok, ne, rb)
        xs = _dispatch(dest, h2.reshape(n_tok, d // 2), n_blocks * rb, tm)
        ys = _experts(bexp, brows, xs, exp_w_gate[l].astype(BF16), exp_w_up[l].astype(BF16),
                      exp_w_down[l].astype(BF16), rb)
        x = _combine(dest, wgt, x1, h2, mod, shared_w_gate[l].astype(BF16),
                     shared_w_up[l].astype(BF16), shared_w_down[l].astype(BF16), ys, tm)
    return x
```

```python
import functools

import jax
import jax.numpy as jnp
from jax import lax
from jax.experimental import pallas as pl
from jax.experimental.pallas import tpu as pltpu
from jax.experimental.pallas import tpu_sc as plsc

F32 = jnp.float32
BF16 = jnp.bfloat16
I32 = jnp.int32
HIGHEST = lax.Precision.HIGHEST

N_MOD = 6
EPS = 1e-6
MOBA_BLOCK = 256
MOBA_TOPK = 3
TOP_K = 6
ROUTED_SCALE = 2.5

LANES = 128
SC_CORES = 2
SC_SUBCORES = 16
SC_GATHER_CHUNK = 64
HGRN_CHUNK = 64
HGRN_SUB = 16
S5_ROW_PAD = 8
MOBA_SUM_ROWS = 16
MASK_BIG = 30000.0
VMEM_LIMIT = 56 * 1024 * 1024


def _nt_dot(a, b, precision=None):
    return lax.dot_general(a, b, (((1,), (1,)), ((), ())), preferred_element_type=F32,
                           precision=precision)


def _tn_dot(a, b, precision=None):
    return lax.dot_general(a, b, (((0,), (0,)), ((), ())), preferred_element_type=F32,
                           precision=precision)


def _dot(a, b, precision=None):
    return jnp.dot(a, b, preferred_element_type=F32, precision=precision)


def _bf16_terms(x, n):
    terms = []
    for _ in range(n):
        t = x.astype(BF16)
        terms.append(t)
        x = x - t.astype(F32)
    return terms


def _dot_exact_lhs(a, b, n):
    ab = a.astype(BF16)
    return functools.reduce(lambda p, q: p + q, [_dot(ab, t) for t in _bf16_terms(b, n)])


def _dot_exact_rhs(a, b, n):
    bb = b.astype(BF16)
    return functools.reduce(lambda p, q: p + q, [_dot(t, bb) for t in _bf16_terms(a, n)])


def _pack_bf16_pairs(y):
    m = y.shape[1] // 2
    lo = lax.bitcast_convert_type(y[:, :m].astype(BF16).astype(F32), jnp.uint32)
    hi = lax.bitcast_convert_type(y[:, m:].astype(BF16).astype(F32), jnp.uint32)
    return hi | (lo >> 16)


def _unpack_bf16_pairs(w):
    lo = lax.bitcast_convert_type(w << 16, F32)
    hi = lax.bitcast_convert_type(w & jnp.uint32(0xFFFF0000), F32)
    return jnp.concatenate([lo, hi], axis=1)


def _sigmoid(x):
    return 1.0 / (1.0 + jnp.exp(-x))


def _silu(x):
    return x * _sigmoid(x)


def _params(*sem):
    return pltpu.CompilerParams(dimension_semantics=sem, vmem_limit_bytes=VMEM_LIMIT)


def _const_spec(shape):
    nd = len(shape)
    return pl.BlockSpec(shape, lambda *_: (0,) * nd, pipeline_mode=pl.Buffered(1))


def _ada_kernel(c_ref, w_ref, b_ref, o_ref):
    c = c_ref[...]
    o_ref[...] = _dot(_silu(c), w_ref[...], HIGHEST) + b_ref[...]


def _ada_mod(c, ada_w, ada_b):
    depth, d, n = ada_w.shape
    bsz = c.shape[0]
    tn = n // 4
    return pl.pallas_call(
        _ada_kernel,
        grid=(depth, n // tn),
        in_specs=[pl.BlockSpec((bsz, d), lambda l, j: (0, 0)),
                  pl.BlockSpec((None, d, tn), lambda l, j: (l, 0, j)),
                  pl.BlockSpec((None, 1, tn), lambda l, j: (l, 0, j))],
        out_specs=pl.BlockSpec((None, bsz, tn), lambda l, j: (l, 0, j)),
        out_shape=jax.ShapeDtypeStruct((depth, bsz, n), F32),
        compiler_params=_params("arbitrary", "arbitrary"),
        name="ada_mod",
    )(c, ada_w, ada_b.reshape(depth, 1, n))


def _prep_kernel(ldt_ref, lr_ref, li_ref, btr_ref, bti_ref, lbl_ref,
                 abar_ref, bbr_ref, bbi_ref, lb_ref):
    depth = ldt_ref.shape[0]
    for l in range(depth):
        dt = jnp.exp(ldt_ref[l])
        lr = lr_ref[l]
        li = li_ref[l]
        mag = jnp.exp(lr * dt)
        ar = mag * jnp.cos(li * dt)
        ai = mag * jnp.sin(li * dt)
        inv = 1.0 / (lr * lr + li * li)
        cr = ((ar - 1.0) * lr + ai * li) * inv
        ci = (ai * lr - (ar - 1.0) * li) * inv
        abar_ref[l, 0:1, :] = ar
        abar_ref[l, 1:2, :] = ai
        btr = btr_ref[l]
        bti = bti_ref[l]
        bbr_ref[l] = cr * btr - ci * bti
        bbi_ref[l] = cr * bti + ci * btr
    rows = [lbl_ref[l:l + 1, :] for l in range(depth)]
    mx = functools.reduce(jnp.maximum, rows)
    ex = [jnp.exp(r - mx) for r in rows]
    den = functools.reduce(lambda a, b: a + b, ex)
    run = ex[0] / den
    first = run
    lb_ref[0:1, :] = run - first
    for l in range(1, depth):
        run = run + ex[l] / den
        lb_ref[l:l + 1, :] = run - first


def _prep_params(log_dt, lam_re, lam_im, b_re, b_im, lb_logits):
    depth, g, p = lam_re.shape
    gw = b_re.shape[-1]
    n = g * p
    hk = lb_logits.shape[-1]
    ldt = jnp.repeat(log_dt, p, axis=1).reshape(depth, 1, n)
    btr = b_re.transpose(0, 3, 1, 2).reshape(depth, gw, n)
    bti = b_im.transpose(0, 3, 1, 2).reshape(depth, gw, n)
    return pl.pallas_call(
        _prep_kernel,
        out_shape=(jax.ShapeDtypeStruct((depth, 2, n), F32),
                   jax.ShapeDtypeStruct((depth, gw, n), F32),
                   jax.ShapeDtypeStruct((depth, gw, n), F32),
                   jax.ShapeDtypeStruct((depth, hk), F32)),
        name="param_prep",
    )(ldt, lam_re.reshape(depth, 1, n), lam_im.reshape(depth, 1, n), btr, bti, lb_logits)


def _inproj_kernel(x_ref, mod_ref, g_ref, w_ref, u_ref, hg_ref, mo_ref, gl_ref, *, widths):
    x = x_ref[...]
    ms = jnp.mean(x * x, axis=-1, keepdims=True)
    h = x * lax.rsqrt(ms + EPS) * g_ref[...]
    h = h * (1.0 + mod_ref[1:2, :]) + mod_ref[0:1, :]
    hb = h.astype(BF16)
    d_ssm, d_hgrn, d_moba, d_gate = widths
    o0 = 0
    u_ref[...] = _dot(hb, w_ref[:, o0:o0 + d_ssm])
    o0 += d_ssm
    hg_ref[...] = _dot(hb, w_ref[:, o0:o0 + d_hgrn])
    o0 += d_hgrn
    mo_ref[...] = _dot(hb, w_ref[:, o0:o0 + d_moba])
    o0 += d_moba
    step = 1024
    for j in range(0, d_gate, step):
        gl_ref[:, j:j + step] = _dot(hb, w_ref[:, o0 + j:o0 + j + step]).astype(BF16)


def _inproj(x, mod, norm_g, w_in_bf, widths, tm):
    bsz, seq, d = x.shape
    d_ssm, d_hgrn, d_moba, d_gate = widths
    d_in = w_in_bf.shape[1]
    row = lambda n: pl.BlockSpec((None, tm, n), lambda b, i: (b, i, 0))
    return pl.pallas_call(
        functools.partial(_inproj_kernel, widths=widths),
        grid=(bsz, seq // tm),
        in_specs=[row(d),
                  pl.BlockSpec((None, N_MOD, d), lambda b, i: (b, 0, 0)),
                  _const_spec((1, d)),
                  _const_spec((d, d_in))],
        out_specs=[row(d_ssm), row(d_hgrn), row(d_moba), row(d_gate)],
        out_shape=(jax.ShapeDtypeStruct((bsz, seq, d_ssm), F32),
                   jax.ShapeDtypeStruct((bsz, seq, d_hgrn), F32),
                   jax.ShapeDtypeStruct((bsz, seq, d_moba), F32),
                   jax.ShapeDtypeStruct((bsz, seq, d_gate), BF16)),
        compiler_params=_params("arbitrary", "arbitrary"),
        name="in_proj",
    )(x, mod, norm_g.reshape(1, d), w_in_bf)


def _s5_kernel(u_ref, wb_ref, wc_ref, abar_ref, d_ref, wglu_ref, o_ref, bu_ref, st_ref,
               *, nb, tl, ns, unroll):
    nc = ns // LANES
    rs = tl + S5_ROW_PAD

    @pl.when(pl.program_id(0) == 0)
    def _():
        st_ref[...] = jnp.zeros_like(st_ref)

    for b in range(nb):
        bu = _dot(u_ref[b].astype(BF16), wb_ref[...])
        for c in range(2 * nc):
            bu_ref[c, b * rs:b * rs + tl, :] = bu[:, c * LANES:(c + 1) * LANES]

    ar = [jnp.broadcast_to(abar_ref[0:1, c * LANES:(c + 1) * LANES], (nb, LANES)) for c in range(nc)]
    ai = [jnp.broadcast_to(abar_ref[1:2, c * LANES:(c + 1) * LANES], (nb, LANES)) for c in range(nc)]

    def outer(j, carry):
        sr, si = list(carry[:nc]), list(carry[nc:])
        for k in range(unroll):
            t = j * unroll + k
            for c in range(nc):
                br = bu_ref[c, pl.ds(t, nb, stride=rs), :]
                bi = bu_ref[nc + c, pl.ds(t, nb, stride=rs), :]
                nr = ar[c] * sr[c] - ai[c] * si[c] + br
                ni = ar[c] * si[c] + ai[c] * sr[c] + bi
                bu_ref[c, pl.ds(t, nb, stride=rs), :] = nr
                bu_ref[nc + c, pl.ds(t, nb, stride=rs), :] = ni
                sr[c], si[c] = nr, ni
        return tuple(sr) + tuple(si)

    init = tuple(st_ref[c] for c in range(2 * nc))
    fin = lax.fori_loop(0, tl // unroll, outer, init)
    for c in range(2 * nc):
        st_ref[c] = fin[c]

    dm = o_ref.shape[-1]
    for b in range(nb):
        s = jnp.concatenate([bu_ref[c, b * rs:b * rs + tl, :] for c in range(2 * nc)], axis=1)
        y = _dot(s.astype(BF16), wc_ref[...]) + d_ref[...] * u_ref[b]
        z = _dot(jax.nn.gelu(y).astype(BF16), wglu_ref[...])
        o_ref[b] = (z[:, :dm] * _sigmoid(z[:, dm:])).astype(o_ref.dtype)


def _s5_mixer(u, wb, wc, abar, d_skip, wglu, tl):
    bsz, seq, d_ssm = u.shape
    ns = abar.shape[-1]
    dm = wglu.shape[1] // 2
    return pl.pallas_call(
        functools.partial(_s5_kernel, nb=bsz, tl=tl, ns=ns, unroll=8),
        grid=(seq // tl,),
        in_specs=[pl.BlockSpec((bsz, tl, d_ssm), lambda i: (0, i, 0)),
                  _const_spec(wb.shape), _const_spec(wc.shape), _const_spec(abar.shape),
                  _const_spec((1, d_ssm)), _const_spec(wglu.shape)],
        out_specs=pl.BlockSpec((bsz, tl, dm), lambda i: (0, i, 0)),
        out_shape=jax.ShapeDtypeStruct((bsz, seq, dm), BF16),
        scratch_shapes=[pltpu.VMEM((2 * ns // LANES, bsz * (tl + S5_ROW_PAD), LANES), F32),
                        pltpu.VMEM((2 * ns // LANES, bsz, LANES), F32)],
        compiler_params=_params("arbitrary"),
        name="s5_mixer",
    )(u, wb, wc, abar, d_skip.reshape(1, d_ssm), wglu)


def _hgrn_kernel(x_ref, lb_ref, ng_ref, o_ref, st_ref, *, tl, hd, dk):
    cs, sb = HGRN_CHUNK, HGRN_SUB

    @pl.when(pl.program_id(1) == 0)
    def _():
        st_ref[...] = jnp.zeros_like(st_ref)

    def iota(shape, dim):
        return lax.broadcasted_iota(I32, shape, dim)

    tril = (iota((cs, cs), 0) >= iota((cs, cs), 1)).astype(F32)
    head_eq = (iota((hd, hd), 0) // dk) == (iota((hd, hd), 1) // dk)
    block_ones = head_eq.astype(BF16)
    block_mean = head_eq.astype(F32) * (1.0 / dk)
    sel = ((iota((sb, sb * sb), 1) // sb) == iota((sb, sb * sb), 0)).astype(BF16)
    s_iota = iota((sb, hd), 0)
    lane_head = iota((1, hd), 1) // dk
    n_heads = hd // dk
    lb = lb_ref[...]
    ng = ng_ref[...]

    def head_copies(a):
        return jnp.concatenate([jnp.where(lane_head == h, a, 0.0) for h in range(n_heads)], axis=0)

    def off_diag_scores(q, key, cum, t0, t1, s0, s1):
        r = cum[s1 - 1:s1, :]
        qt = q[t0:t1] * jnp.exp(cum[t0:t1] - r)
        ks = key[s0:s1] * jnp.exp(r - cum[s0:s1])
        return _nt_dot(qt.astype(BF16), head_copies(ks).astype(BF16))

    nch = tl // cs
    half = cs // 2
    spans = ((half, cs, 0, half), (sb, half, 0, sb), (half + sb, cs, half, half + sb))
    pre = []
    for c in range(nch):
        r0 = c * cs
        q = x_ref[r0:r0 + cs, 0:hd]
        fr = x_ref[r0:r0 + cs, hd:2 * hd]
        iv = x_ref[r0:r0 + cs, 2 * hd:3 * hd]
        forget = lb + (1.0 - lb) * _sigmoid(fr)
        key = 1.0 - forget
        cum = _dot_exact_lhs(tril, jnp.log(forget), 3)
        pre.append((q, key, iv, cum))
    heads_sum = []
    for q, key, iv, cum in pre:
        for a in range(cs // sb):
            lo = a * sb
            cum_a, qa, ka = cum[lo:lo + sb], q[lo:lo + sb], key[lo:lo + sb]
            slabs = []
            for t in range(sb):
                arg = jnp.where(s_iota <= t, cum_a[t:t + 1] - cum_a, -jnp.inf)
                slabs.append(jnp.exp(arg) * (qa[t:t + 1] * ka))
            heads_sum.append(_dot(jnp.concatenate(slabs, axis=0).astype(BF16), block_ones))
    off_scores = [[off_diag_scores(q, key, cum, *sp) for sp in spans] for q, key, iv, cum in pre]
    intras = []
    for c, (q, key, iv, cum) in enumerate(pre):
        parts = []
        for a in range(cs // sb):
            lo = a * sb
            w = heads_sum[c * (cs // sb) + a] * jnp.concatenate([iv[lo:lo + sb]] * sb, axis=0)
            parts.append(_dot(sel, w.astype(BF16)))
        lvl1, lvl2a, lvl2b = [
            _dot(sc.astype(BF16), head_copies(iv[sp[2]:sp[3]]).astype(BF16))
            for sc, sp in zip(off_scores[c], spans)]
        zeros = jnp.zeros((sb, hd), F32)
        intras.append(jnp.concatenate(parts, axis=0)
                      + jnp.concatenate([zeros, lvl2a, lvl1[0:sb], lvl1[sb:half] + lvl2b], axis=0))
    upds, decays, qcs = [], [], []
    for q, key, iv, cum in pre:
        last = cum[cs - 1:cs, :]
        kc = key * jnp.exp(last - cum)
        upds.append(jnp.where(head_eq, _tn_dot(iv.astype(BF16), kc.astype(BF16)), 0.0))
        decays.append(jnp.exp(last))
        qcs.append((q * jnp.exp(cum)).astype(BF16))
    st = st_ref[...]
    for c in range(nch):
        out = intras[c] + _nt_dot(qcs[c], st.astype(BF16))
        st = st * decays[c] + upds[c]
        ms = _dot_exact_rhs(out * out, block_mean, 2)
        g = x_ref[c * cs:(c + 1) * cs, 3 * hd:4 * hd]
        o_ref[c * cs:(c + 1) * cs, :] = (out * lax.rsqrt(ms + EPS) * ng * _silu(g)).astype(o_ref.dtype)
    st_ref[...] = st


def _hgrn_mixer(hg, lb, norm_g, tl):
    bsz, seq, d4 = hg.shape
    hd = d4 // 4
    dk = norm_g.shape[-1]
    ng = jnp.tile(norm_g, hd // dk).reshape(1, hd)
    return pl.pallas_call(
        functools.partial(_hgrn_kernel, tl=tl, hd=hd, dk=dk),
        grid=(bsz, seq // tl),
        in_specs=[pl.BlockSpec((None, tl, d4), lambda b, i: (b, i, 0)),
                  _const_spec((1, hd)), _const_spec((1, hd))],
        out_specs=pl.BlockSpec((None, tl, hd), lambda b, i: (b, i, 0)),
        out_shape=jax.ShapeDtypeStruct((bsz, seq, hd), BF16),
        scratch_shapes=[pltpu.VMEM((hd, hd), F32)],
        compiler_params=_params("arbitrary", "arbitrary"),
        name="hgrn_mixer",
    )(hg, lb.reshape(1, hd), ng)


def _moba_prep_kernel(x_ref, gq_ref, gk_ref, qt_ref, ka_ref, km_ref, vt_ref, *, nh, dh):
    i = pl.program_id(1)
    blk = MOBA_BLOCK
    hd = nh * dh

    @pl.when(i == 0)
    def _():
        km_ref[...] = jnp.zeros_like(km_ref)

    lane = lax.broadcasted_iota(I32, (blk, LANES - dh), 1)
    onehot = (lane == i).astype(F32)
    pad = jnp.zeros((blk, LANES - dh), F32)
    scale = dh ** -0.5
    for h in range(nh):
        qh = x_ref[:, h * dh:(h + 1) * dh]
        kh = x_ref[:, hd + h * dh:hd + (h + 1) * dh]
        vh = x_ref[:, 2 * hd + h * dh:2 * hd + (h + 1) * dh]
        qn = qh * lax.rsqrt(jnp.mean(qh * qh, axis=-1, keepdims=True) + EPS) * gq_ref[...] * scale
        kn = kh * lax.rsqrt(jnp.mean(kh * kh, axis=-1, keepdims=True) + EPS) * gk_ref[...]
        qt_ref[h] = jnp.concatenate([qn, pad], axis=1).T
        ka_ref[h] = jnp.concatenate([kn, onehot], axis=1).astype(BF16)
        kmean = jnp.mean(kn, axis=0, keepdims=True)
        km_ref[h, pl.ds(i, 1), :] = jnp.concatenate([kmean, jnp.zeros((1, LANES - dh), F32)], axis=1)
        vt_ref[h] = jnp.concatenate([vh, pad + 1.0], axis=1).T[0:dh + MOBA_SUM_ROWS].astype(BF16)


def _moba_kernel(qt_ref, ka_ref, km_ref, vt_ref, o_ref, qa_scr, m_scr, acc_scr, *, nh, dh, nbp):
    i = pl.program_id(1)
    blk = MOBA_BLOCK
    r0 = pl.multiple_of(i * blk, blk)
    b_iota = lax.broadcasted_iota(I32, (nbp, blk), 0)
    key_i = lax.broadcasted_iota(I32, (blk, blk), 0)
    qry_i = lax.broadcasted_iota(I32, (blk, blk), 1)

    for h in range(nh):
        qt = qt_ref[h]
        gate = _dot(km_ref[h], qt, HIGHEST)
        work = jnp.where(b_iota < i, gate, -jnp.inf)
        chosen = jnp.zeros((nbp, blk), jnp.bool_)
        for _ in range(MOBA_TOPK):
            mx = jnp.max(work, axis=0, keepdims=True)
            first = jnp.min(jnp.where(work == mx, b_iota, nbp), axis=0, keepdims=True)
            hit = (b_iota == first) & (mx > -jnp.inf)
            chosen = chosen | hit
            work = jnp.where(hit, -jnp.inf, work)
        pen = jnp.where(chosen, 0.0, -MASK_BIG)
        qa_scr[h] = jnp.concatenate(
            [qt[0:dh], pen, jnp.zeros((LANES - dh - nbp, blk), F32)], axis=0).astype(BF16)
        s = _dot(ka_ref[h, pl.ds(r0, blk), :], qt.astype(BF16))
        s = jnp.where(key_i <= qry_i, s, -jnp.inf)
        m0 = jnp.max(s, axis=0, keepdims=True)
        m_scr[h] = m0
        acc_scr[h] = _dot(vt_ref[h, i], jnp.exp(s - m0).astype(BF16))

    def past_blocks(js):
        scores = [[_dot(ka_ref[h, pl.ds(pl.multiple_of(j * blk, blk), blk), :], qa_scr[h]) for j in js]
                  for h in range(nh)]
        probs, alphas = [], []
        for h in range(nh):
            m_old = m_scr[h]
            m_new = m_old
            for s in scores[h]:
                m_new = jnp.maximum(m_new, jnp.max(s, axis=0, keepdims=True))
            probs.append([jnp.exp(s - m_new).astype(BF16) for s in scores[h]])
            alphas.append(jnp.exp(m_old - m_new))
            m_scr[h] = m_new
        for h in range(nh):
            acc = alphas[h] * acc_scr[h]
            for j, p in zip(js, probs[h]):
                acc = acc + _dot(vt_ref[h, j], p)
            acc_scr[h] = acc

    group = 4

    def past_group(jj, carry):
        past_blocks(tuple(group * jj + k for k in range(group)))
        return carry

    lax.fori_loop(0, i // group, past_group, 0)
    done = (i // group) * group
    for width in (2, 1):
        has = (i % (2 * width)) >= width

        @pl.when(has)
        def _(done=done, width=width):
            past_blocks(tuple(done + k for k in range(width)))

        done = done + jnp.where(has, width, 0)

    for h in range(nh):
        acc = acc_scr[h]
        o_ref[h * dh:(h + 1) * dh, :] = (acc[0:dh] / acc[dh:dh + 1]).astype(o_ref.dtype)


def _moba_mixer(mo, gq, gk):
    bsz, seq, d3 = mo.shape
    hd = d3 // 3
    dh = gq.shape[-1]
    nh = hd // dh
    blk = MOBA_BLOCK
    nblk = seq // blk
    nbp = -(-nblk // 8) * 8
    vr = dh + MOBA_SUM_ROWS
    assert seq % blk == 0 and dh + nbp <= LANES
    qt, ka, km, vt = pl.pallas_call(
        functools.partial(_moba_prep_kernel, nh=nh, dh=dh),
        grid=(bsz, nblk),
        in_specs=[pl.BlockSpec((None, blk, d3), lambda b, i: (b, i, 0)),
                  _const_spec((1, dh)), _const_spec((1, dh))],
        out_specs=[pl.BlockSpec((None, nh, LANES, blk), lambda b, i: (b, 0, 0, i)),
                   pl.BlockSpec((None, nh, blk, LANES), lambda b, i: (b, 0, i, 0)),
                   pl.BlockSpec((None, nh, nbp, LANES), lambda b, i: (b, 0, 0, 0)),
                   pl.BlockSpec((None, nh, None, vr, blk), lambda b, i: (b, 0, i, 0, 0))],
        out_shape=(jax.ShapeDtypeStruct((bsz, nh, LANES, seq), F32),
                   jax.ShapeDtypeStruct((bsz, nh, seq, LANES), BF16),
                   jax.ShapeDtypeStruct((bsz, nh, nbp, LANES), F32),
                   jax.ShapeDtypeStruct((bsz, nh, nblk, vr, blk), BF16)),
        compiler_params=_params("arbitrary", "arbitrary"),
        name="moba_prep",
    )(mo, gq.reshape(1, dh), gk.reshape(1, dh))
    return pl.pallas_call(
        functools.partial(_moba_kernel, nh=nh, dh=dh, nbp=nbp),
        grid=(bsz, nblk),
        in_specs=[pl.BlockSpec((None, nh, LANES, blk), lambda b, i: (b, 0, 0, i)),
                  pl.BlockSpec((None, nh, seq, LANES), lambda b, i: (b, 0, 0, 0)),
                  pl.BlockSpec((None, nh, nbp, LANES), lambda b, i: (b, 0, 0, 0)),
                  pl.BlockSpec((None, nh, nblk, vr, blk), lambda b, i: (b, 0, 0, 0, 0))],
        out_specs=pl.BlockSpec((None, hd, blk), lambda b, i: (b, 0, i)),
        out_shape=jax.ShapeDtypeStruct((bsz, hd, seq), BF16),
        scratch_shapes=[pltpu.VMEM((nh, LANES, blk), BF16),
                        pltpu.VMEM((nh, 1, blk), F32),
                        pltpu.VMEM((nh, vr, blk), F32)],
        compiler_params=_params("arbitrary", "arbitrary"),
        name="moba_attn",
    )(qt, ka, km, vt)


def _merge_kernel(x_ref, ya_ref, hb_ref, mc_ref, gl_ref, mod_ref, g2_ref, who_ref, wmo_ref, wout_ref,
                  rwh_ref, rwl_ref, rb_ref,
                  x1_ref, h2_ref, idx_ref, wgt_ref, pos_ref, cnt_ref, *, tm, ne):
    first = (pl.program_id(0) == 0) & (pl.program_id(1) == 0)

    @pl.when(first)
    def _():
        cnt_ref[...] = jnp.zeros_like(cnt_ref)

    d = x_ref.shape[-1]
    yb = _dot(hb_ref[...], who_ref[...])
    yc = _tn_dot(mc_ref[...], wmo_ref[...])
    merged = (_sigmoid(gl_ref[:, 0:d].astype(F32)) * ya_ref[...].astype(F32)
              + _sigmoid(gl_ref[:, d:2 * d].astype(F32)) * yb
              + _sigmoid(gl_ref[:, 2 * d:3 * d].astype(F32)) * yc)
    x1 = x_ref[...] + mod_ref[2:3, :] * _dot(merged.astype(BF16), wout_ref[...])
    x1_ref[...] = x1
    ms = jnp.mean(x1 * x1, axis=-1, keepdims=True)
    h2 = x1 * lax.rsqrt(ms + EPS) * g2_ref[...]
    h2 = h2 * (1.0 + mod_ref[4:5, :]) + mod_ref[3:4, :]
    h2_ref[...] = _pack_bf16_pairs(h2)
    hi = h2.astype(BF16)
    lo = (h2 - hi.astype(F32)).astype(BF16)
    logits = _nt_dot(rwh_ref[...], hi) + _nt_dot(rwh_ref[...], lo) + _nt_dot(rwl_ref[...], hi)
    scores = _sigmoid(logits)
    e_iota = lax.broadcasted_iota(I32, (ne, tm), 0)
    work = scores + rb_ref[...]
    picked = jnp.zeros((ne, tm), F32)
    hits, tops = [], []
    for _ in range(TOP_K):
        mx = jnp.max(work, axis=0, keepdims=True)
        first_e = jnp.min(jnp.where(work == mx, e_iota, ne), axis=0, keepdims=True)
        hit = e_iota == first_e
        hits.append((hit, first_e))
        tops.append(jnp.sum(jnp.where(hit, scores, 0.0), axis=0, keepdims=True))
        work = jnp.where(hit, -jnp.inf, work)
        picked = jnp.where(hit, 1.0, picked)
    total = functools.reduce(lambda a, b: a + b, tops)
    upper = (lax.broadcasted_iota(I32, (tm, tm), 0) < lax.broadcasted_iota(I32, (tm, tm), 1)).astype(BF16)
    rank = _dot(picked.astype(BF16), upper) + cnt_ref[:, 0:1]
    zero_row = jnp.zeros((1, tm), F32)
    idx_rows, w_rows, pos_rows = [], [], []
    for (hit, first_e), top in zip(hits, tops):
        idx_rows.append(first_e)
        w_rows.append(top / total * ROUTED_SCALE)
        pos_rows.append(jnp.sum(jnp.where(hit, rank, 0.0), axis=0, keepdims=True))
    pad = 8 - TOP_K
    idx_ref[...] = jnp.concatenate(idx_rows + [jnp.zeros((1, tm), I32)] * pad, axis=0)
    wgt_ref[...] = jnp.concatenate(w_rows + [zero_row] * pad, axis=0)
    pos_ref[...] = jnp.concatenate(pos_rows + [zero_row] * pad, axis=0).astype(I32)
    cnt_ref[...] = cnt_ref[...] + jnp.sum(picked, axis=1, keepdims=True)


def _merge_router(x, ya, hb, mc, gl, mod, norm2_g, who, wmo, wout, rwh, rwl, rbias, tm):
    bsz, seq, d = x.shape
    ne = rwh.shape[0]
    nt = seq // tm
    n_tok = bsz * seq
    row = lambda n: pl.BlockSpec((None, tm, n), lambda b, i: (b, i, 0))
    col = pl.BlockSpec((8, tm), lambda b, i: (0, b * nt + i))
    return pl.pallas_call(
        functools.partial(_merge_kernel, tm=tm, ne=ne),
        grid=(bsz, nt),
        in_specs=[row(d), row(d), row(hb.shape[-1]),
                  pl.BlockSpec((None, mc.shape[1], tm), lambda b, i: (b, 0, i)), row(gl.shape[-1]),
                  pl.BlockSpec((None, N_MOD, d), lambda b, i: (b, 0, 0)),
                  _const_spec((1, d)), _const_spec(who.shape), _const_spec(wmo.shape),
                  _const_spec(wout.shape), _const_spec(rwh.shape), _const_spec(rwl.shape),
                  _const_spec((ne, 1))],
        out_specs=[row(d), row(d // 2), col, col, col,
                   pl.BlockSpec((ne, LANES), lambda b, i: (0, 0))],
        out_shape=(jax.ShapeDtypeStruct((bsz, seq, d), F32),
                   jax.ShapeDtypeStruct((bsz, seq, d // 2), jnp.uint32),
                   jax.ShapeDtypeStruct((8, n_tok), I32),
                   jax.ShapeDtypeStruct((8, n_tok), F32),
                   jax.ShapeDtypeStruct((8, n_tok), I32),
                   jax.ShapeDtypeStruct((ne, LANES), F32)),
        compiler_params=_params("arbitrary", "arbitrary"),
        name="merge_router",
    )(x, ya, hb, mc, gl, mod, norm2_g.reshape(1, d), who, wmo, wout, rwh, rwl, rbias.reshape(ne, 1))


def _dispatch_kernel(dest_ref, h_ref, xs_ref, sem, *, tm):
    def body(t, carry):
        for k in range(TOP_K):
            pltpu.make_async_copy(h_ref.at[pl.ds(t, 1), :],
                                  xs_ref.at[pl.ds(dest_ref[k, t], 1), :], sem).start()
        return carry

    lax.fori_loop(0, tm, body, 0)
    for k in range(TOP_K):
        pltpu.make_async_copy(h_ref, xs_ref.at[pl.ds(0, tm), :], sem).wait()


def _dispatch(dest, h2, n_rows, tm):
    n_tok, d = h2.shape
    return pl.pallas_call(
        functools.partial(_dispatch_kernel, tm=tm),
        grid=(n_tok // tm,),
        in_specs=[pl.BlockSpec((8, tm), lambda i: (0, i), memory_space=pltpu.SMEM),
                  pl.BlockSpec((tm, d), lambda i: (i, 0))],
        out_specs=pl.BlockSpec(memory_space=pl.ANY),
        out_shape=jax.ShapeDtypeStruct((n_rows, d), h2.dtype),
        scratch_shapes=[pltpu.SemaphoreType.DMA],
        compiler_params=_params("arbitrary"),
        name="moe_dispatch",
    )(dest, h2)


def _expert_kernel(be_ref, nv_ref, x_ref, wg_ref, wu_ref, wd_ref, y_ref, *, rb):
    nv = nv_ref[pl.program_id(0)]

    @pl.when(nv > 0)
    def _():
        rows = lax.broadcasted_iota(I32, (rb, 1), 0)
        x = jnp.where(rows < nv, _unpack_bf16_pairs(x_ref[...]), 0.0).astype(BF16)
        g = _dot(x, wg_ref[...])
        u = _dot(x, wu_ref[...])
        y_ref[...] = _pack_bf16_pairs(_dot((_silu(g) * u).astype(BF16), wd_ref[...]))


def _experts(block_expert, block_rows, xs, wg, wu, wd, rb):
    n_rows = xs.shape[0]
    n_blocks = n_rows // rb
    _, d, de = wg.shape
    live = lambda i, be, nv: jnp.where(nv[i] > 0, i, n_blocks - 1)
    grid_spec = pltpu.PrefetchScalarGridSpec(
        num_scalar_prefetch=2,
        grid=(n_blocks,),
        in_specs=[pl.BlockSpec((rb, d // 2), lambda i, be, nv: (live(i, be, nv), 0)),
                  pl.BlockSpec((None, d, de), lambda i, be, nv: (be[i], 0, 0)),
                  pl.BlockSpec((None, d, de), lambda i, be, nv: (be[i], 0, 0)),
                  pl.BlockSpec((None, de, d), lambda i, be, nv: (be[i], 0, 0))],
        out_specs=pl.BlockSpec((rb, d // 2), lambda i, be, nv: (live(i, be, nv), 0)),
    )
    return pl.pallas_call(
        functools.partial(_expert_kernel, rb=rb),
        grid_spec=grid_spec,
        out_shape=jax.ShapeDtypeStruct((n_rows, d // 2), jnp.uint32),
        compiler_params=_params("arbitrary"),
        name="moe_experts",
    )(block_expert, block_rows, xs, wg, wu, wd)


def _sc_gather_rows(table, idx):
    n_idx = idx.shape[0]
    width = table.shape[1]
    n_workers = SC_CORES * SC_SUBCORES
    per_worker = n_idx // n_workers
    assert n_idx % (n_workers * SC_GATHER_CHUNK) == 0
    mesh = plsc.VectorSubcoreMesh(core_axis_name="c", subcore_axis_name="s")

    def body(table_hbm, idx_hbm, out_hbm, idx_v, rows_v, sem):
        wid = lax.axis_index("s") * SC_CORES + lax.axis_index("c")

        @pl.loop(0, per_worker // SC_GATHER_CHUNK)
        def _(j):
            base = wid * per_worker + j * SC_GATHER_CHUNK
            pltpu.sync_copy(idx_hbm.at[pl.ds(base, SC_GATHER_CHUNK)], idx_v)
            pltpu.async_copy(table_hbm.at[idx_v], rows_v, sem).wait()
            pltpu.sync_copy(rows_v, out_hbm.at[pl.ds(base, SC_GATHER_CHUNK)])

    return pl.kernel(
        body, mesh=mesh,
        out_type=jax.ShapeDtypeStruct((n_idx, width), table.dtype),
        scratch_types=[pltpu.VMEM((SC_GATHER_CHUNK,), I32),
                       pltpu.VMEM((SC_GATHER_CHUNK, width), table.dtype),
                       pltpu.SemaphoreType.DMA],
        name="moe_gather_sc",
    )(table, idx)


def _combine_kernel(wgt_ref, x1_ref, h2_ref, mod_ref, sg_ref, su_ref, sd_ref, g_ref, o_ref):
    hb = _unpack_bf16_pairs(h2_ref[...]).astype(BF16)
    hid = _silu(_dot(hb, sg_ref[...])) * _dot(hb, su_ref[...])
    y = _dot(hid.astype(BF16), sd_ref[...])
    wcol = wgt_ref[...].T
    for k in range(TOP_K):
        y = y + _unpack_bf16_pairs(g_ref[k]) * wcol[:, k:k + 1]
    o_ref[...] = x1_ref[...] + mod_ref[5:6, :] * y


def _combine(wgt, x1, h2, mod, sg, su, sd, gathered, tm):
    bsz, seq, d = x1.shape
    nt = seq // tm
    row = pl.BlockSpec((None, tm, d), lambda b, i: (b, i, 0))
    return pl.pallas_call(
        _combine_kernel,
        grid=(bsz, nt),
        in_specs=[pl.BlockSpec((8, tm), lambda b, i: (0, b * nt + i)),
                  row, pl.BlockSpec((None, tm, d // 2), lambda b, i: (b, i, 0)),
                  pl.BlockSpec((None, N_MOD, d), lambda b, i: (b, 0, 0)),
                  _const_spec(sg.shape), _const_spec(su.shape), _const_spec(sd.shape),
                  pl.BlockSpec((TOP_K, tm, d // 2), lambda b, i: (0, b * nt + i, 0))],
        out_specs=row,
        out_shape=jax.ShapeDtypeStruct((bsz, seq, d), F32),
        compiler_params=_params("arbitrary", "arbitrary"),
        name="moe_combine",
    )(wgt, x1, h2, mod, sg, su, sd, gathered)


def _dest_kernel(ps_ref, idx_ref, pos_ref, o_ref, *, ne):
    idx = idx_ref[...]
    acc = pos_ref[...]
    for e in range(ne):
        acc = acc + jnp.where(idx == e, ps_ref[e], 0)
    o_ref[...] = acc


def _routing_tables(idx, pos, counts, n_tok, ne, rb):
    n_assign = n_tok * TOP_K
    n_blocks = -(-n_assign // rb) + ne
    padded = (counts + rb - 1) // rb * rb
    pend = jnp.cumsum(padded)
    pstart = (pend - padded).astype(I32)
    tt = min(8192, n_tok)
    col = pl.BlockSpec((8, tt), lambda i: (0, i))
    dest = pl.pallas_call(
        functools.partial(_dest_kernel, ne=ne),
        grid=(n_tok // tt,),
        in_specs=[pl.BlockSpec(memory_space=pltpu.SMEM), col, col],
        out_specs=col,
        out_shape=jax.ShapeDtypeStruct((8, n_tok), I32),
        compiler_params=_params("arbitrary"),
        name="moe_dest",
    )(pstart, idx, pos)
    bstart = jnp.arange(n_blocks, dtype=I32) * rb
    bexp = jnp.minimum(jnp.sum((pend[None, :] <= bstart[:, None]).astype(I32), axis=1), ne - 1)
    brows = jnp.clip(counts[bexp] - (bstart - pstart[bexp]), 0, rb).astype(I32)
    return dest, bexp, brows, n_blocks


def _block_diag_in(bbt, g):
    gw, n = bbt.shape
    p = n // g
    rows = jnp.tile(bbt, (g, 1))
    rg = jnp.arange(g * gw)[:, None] // gw
    cg = jnp.arange(n)[None, :] // p
    return jnp.where(rg == cg, rows, 0.0)


def _block_diag_out(c, sign):
    g, gw, p = c.shape
    cols = jnp.tile(c.transpose(0, 2, 1).reshape(g * p, gw), (1, g))
    rg = jnp.arange(g * p)[:, None] // p
    cg = jnp.arange(g * gw)[None, :] // gw
    return jnp.where(rg == cg, sign * cols, 0.0)


def kernel(x, c, ada_w, ada_b, norm1_g, w_in, ssm_log_dt, ssm_lambda_re, ssm_lambda_im, ssm_b_re,
           ssm_b_im, ssm_c_re, ssm_c_im, ssm_d, ssm_w_glu, hgrn_lb_logits, hgrn_norm_g, hgrn_w_o,
           moba_q_norm_g, moba_k_norm_g, moba_w_o, w_out, norm2_g, router_w, router_bias,
           exp_w_gate, exp_w_up, exp_w_down, shared_w_gate, shared_w_up, shared_w_down):
    bsz, seq, d = x.shape
    depth = w_in.shape[0]
    n_tok = bsz * seq
    g = ssm_lambda_re.shape[1]
    d_ssm = ssm_d.shape[-1]
    d_hk = hgrn_lb_logits.shape[-1]
    d_moba = moba_w_o.shape[1]
    ne = router_w.shape[-1]
    widths = (d_ssm, 4 * d_hk, 3 * d_moba, 3 * d)
    tm = min(256, seq)
    rb = 512

    mod_all = _ada_mod(c, ada_w, ada_b).reshape(depth, bsz, N_MOD, d)
    abar, bbr, bbi, lower = _prep_params(ssm_log_dt, ssm_lambda_re, ssm_lambda_im, ssm_b_re, ssm_b_im,
                                         hgrn_lb_logits)

    for l in range(depth):
        mod = mod_all[l]
        u, hg, mo, gl = _inproj(x, mod, norm1_g[l], w_in[l].astype(BF16), widths, min(512, seq))
        wb = jnp.concatenate([_block_diag_in(bbr[l], g), _block_diag_in(bbi[l], g)], axis=1).astype(BF16)
        wc = jnp.concatenate([_block_diag_out(ssm_c_re[l], 1.0), _block_diag_out(ssm_c_im[l], -1.0)],
                             axis=0).astype(BF16)
        ya = _s5_mixer(u, wb, wc, abar[l], ssm_d[l], ssm_w_glu[l].astype(BF16), min(128, seq))
        hb = _hgrn_mixer(hg, lower[l], hgrn_norm_g[l], min(256, seq))
        mc = _moba_mixer(mo, moba_q_norm_g[l], moba_k_norm_g[l])
        rwt = router_w[l].T
        rwh = rwt.astype(BF16)
        rwl = (rwt - rwh.astype(F32)).astype(BF16)
        x1, h2, idx, wgt, pos, cnt = _merge_router(
            x, ya, hb, mc, gl, mod, norm2_g[l], hgrn_w_o[l].astype(BF16), moba_w_o[l].astype(BF16),
            w_out[l].astype(BF16), rwh, rwl, router_bias[l], tm)
        counts = cnt[:, 0].astype(I32)
        dest, bexp, brows, n_blocks = _routing_tables(idx, pos, counts, n_tok, ne, rb)
        xs = _dispatch(dest, h2.reshape(n_tok, d // 2), n_blocks * rb, tm)
        ys = _experts(bexp, brows, xs, exp_w_gate[l].astype(BF16), exp_w_up[l].astype(BF16),
                      exp_w_down[l].astype(BF16), rb)
        gathered = _sc_gather_rows(ys, dest[:TOP_K].reshape(-1)).reshape(TOP_K, n_tok, d // 2)
        x = _combine(wgt, x1, h2, mod, shared_w_gate[l].astype(BF16),
                     shared_w_up[l].astype(BF16), shared_w_down[l].astype(BF16), gathered, tm)
    return x
```

```python
import functools

import jax
import jax.numpy as jnp
from jax import lax
from jax.experimental import pallas as pl
from jax.experimental.pallas import tpu as pltpu
from jax.experimental.pallas import tpu_sc as plsc

F32 = jnp.float32
BF16 = jnp.bfloat16
I32 = jnp.int32
HIGHEST = lax.Precision.HIGHEST

N_MOD = 6
EPS = 1e-6
MOBA_BLOCK = 256
MOBA_TOPK = 3
TOP_K = 6
ROUTED_SCALE = 2.5

LANES = 128
SC_CORES = 2
SC_SUBCORES = 16
SC_GATHER_CHUNK = 64
HGRN_CHUNK = 64
HGRN_SUB = 16
S5_ROW_PAD = 8
MOBA_SUM_ROWS = 16
MASK_BIG = 30000.0
VMEM_LIMIT = 56 * 1024 * 1024


def _nt_dot(a, b, precision=None):
    return lax.dot_general(a, b, (((1,), (1,)), ((), ())), preferred_element_type=F32,
                           precision=precision)


def _tn_dot(a, b, precision=None):
    return lax.dot_general(a, b, (((0,), (0,)), ((), ())), preferred_element_type=F32,
                           precision=precision)


def _dot(a, b, precision=None):
    return jnp.dot(a, b, preferred_element_type=F32, precision=precision)


def _bf16_terms(x, n):
    terms = []
    for _ in range(n):
        t = x.astype(BF16)
        terms.append(t)
        x = x - t.astype(F32)
    return terms


def _dot_exact_lhs(a, b, n):
    ab = a.astype(BF16)
    return functools.reduce(lambda p, q: p + q, [_dot(ab, t) for t in _bf16_terms(b, n)])


def _dot_exact_rhs(a, b, n):
    bb = b.astype(BF16)
    return functools.reduce(lambda p, q: p + q, [_dot(t, bb) for t in _bf16_terms(a, n)])


def _pack_bf16_pairs(y):
    m = y.shape[1] // 2
    lo = lax.bitcast_convert_type(y[:, :m].astype(BF16).astype(F32), jnp.uint32)
    hi = lax.bitcast_convert_type(y[:, m:].astype(BF16).astype(F32), jnp.uint32)
    return hi | (lo >> 16)


def _unpack_bf16_pairs(w):
    lo = lax.bitcast_convert_type(w << 16, F32)
    hi = lax.bitcast_convert_type(w & jnp.uint32(0xFFFF0000), F32)
    return jnp.concatenate([lo, hi], axis=1)


def _sigmoid(x):
    return 1.0 / (1.0 + jnp.exp(-x))


def _silu(x):
    return x * _sigmoid(x)


def _params(*sem):
    return pltpu.CompilerParams(dimension_semantics=sem, vmem_limit_bytes=VMEM_LIMIT)


def _const_spec(shape):
    nd = len(shape)
    return pl.BlockSpec(shape, lambda *_: (0,) * nd, pipeline_mode=pl.Buffered(1))


def _ada_kernel(c_ref, w_ref, b_ref, o_ref):
    c = c_ref[...]
    o_ref[...] = _dot(_silu(c), w_ref[...], HIGHEST) + b_ref[...]


def _ada_mod(c, ada_w, ada_b):
    depth, d, n = ada_w.shape
    bsz = c.shape[0]
    tn = n // 4
    return pl.pallas_call(
        _ada_kernel,
        grid=(depth, n // tn),
        in_specs=[pl.BlockSpec((bsz, d), lambda l, j: (0, 0)),
                  pl.BlockSpec((None, d, tn), lambda l, j: (l, 0, j)),
                  pl.BlockSpec((None, 1, tn), lambda l, j: (l, 0, j))],
        out_specs=pl.BlockSpec((None, bsz, tn), lambda l, j: (l, 0, j)),
        out_shape=jax.ShapeDtypeStruct((depth, bsz, n), F32),
        compiler_params=_params("arbitrary", "arbitrary"),
        name="ada_mod",
    )(c, ada_w, ada_b.reshape(depth, 1, n))


def _prep_kernel(ldt_ref, lr_ref, li_ref, btr_ref, bti_ref, lbl_ref,
                 abar_ref, bbr_ref, bbi_ref, lb_ref):
    depth = ldt_ref.shape[0]
    for l in range(depth):
        dt = jnp.exp(ldt_ref[l])
        lr = lr_ref[l]
        li = li_ref[l]
        mag = jnp.exp(lr * dt)
        ar = mag * jnp.cos(li * dt)
        ai = mag * jnp.sin(li * dt)
        inv = 1.0 / (lr * lr + li * li)
        cr = ((ar - 1.0) * lr + ai * li) * inv
        ci = (ai * lr - (ar - 1.0) * li) * inv
        abar_ref[l, 0:1, :] = ar
        abar_ref[l, 1:2, :] = ai
        btr = btr_ref[l]
        bti = bti_ref[l]
        bbr_ref[l] = cr * btr - ci * bti
        bbi_ref[l] = cr * bti + ci * btr
    rows = [lbl_ref[l:l + 1, :] for l in range(depth)]
    mx = functools.reduce(jnp.maximum, rows)
    ex = [jnp.exp(r - mx) for r in rows]
    den = functools.reduce(lambda a, b: a + b, ex)
    run = ex[0] / den
    first = run
    lb_ref[0:1, :] = run - first
    for l in range(1, depth):
        run = run + ex[l] / den
        lb_ref[l:l + 1, :] = run - first


def _prep_params(log_dt, lam_re, lam_im, b_re, b_im, lb_logits):
    depth, g, p = lam_re.shape
    gw = b_re.shape[-1]
    n = g * p
    hk = lb_logits.shape[-1]
    ldt = jnp.repeat(log_dt, p, axis=1).reshape(depth, 1, n)
    btr = b_re.transpose(0, 3, 1, 2).reshape(depth, gw, n)
    bti = b_im.transpose(0, 3, 1, 2).reshape(depth, gw, n)
    return pl.pallas_call(
        _prep_kernel,
        out_shape=(jax.ShapeDtypeStruct((depth, 2, n), F32),
                   jax.ShapeDtypeStruct((depth, gw, n), F32),
                   jax.ShapeDtypeStruct((depth, gw, n), F32),
                   jax.ShapeDtypeStruct((depth, hk), F32)),
        name="param_prep",
    )(ldt, lam_re.reshape(depth, 1, n), lam_im.reshape(depth, 1, n), btr, bti, lb_logits)


def _inproj_kernel(x_ref, mod_ref, g_ref, w_ref, u_ref, hg_ref, mo_ref, gl_ref, *, widths):
    x = x_ref[...]
    ms = jnp.mean(x * x, axis=-1, keepdims=True)
    h = x * lax.rsqrt(ms + EPS) * g_ref[...]
    h = h * (1.0 + mod_ref[1:2, :]) + mod_ref[0:1, :]
    hb = h.astype(BF16)
    d_ssm, d_hgrn, d_moba, d_gate = widths
    o0 = 0
    u_ref[...] = _dot(hb, w_ref[:, o0:o0 + d_ssm])
    o0 += d_ssm
    hg_ref[...] = _dot(hb, w_ref[:, o0:o0 + d_hgrn])
    o0 += d_hgrn
    mo_ref[...] = _dot(hb, w_ref[:, o0:o0 + d_moba])
    o0 += d_moba
    step = 1024
    for j in range(0, d_gate, step):
        gl_ref[:, j:j + step] = _dot(hb, w_ref[:, o0 + j:o0 + j + step]).astype(BF16)


def _inproj(x, mod, norm_g, w_in_bf, widths, tm):
    bsz, seq, d = x.shape
    d_ssm, d_hgrn, d_moba, d_gate = widths
    d_in = w_in_bf.shape[1]
    row = lambda n: pl.BlockSpec((None, tm, n), lambda b, i: (b, i, 0))
    return pl.pallas_call(
        functools.partial(_inproj_kernel, widths=widths),
        grid=(bsz, seq // tm),
        in_specs=[row(d),
                  pl.BlockSpec((None, N_MOD, d), lambda b, i: (b, 0, 0)),
                  _const_spec((1, d)),
                  _const_spec((d, d_in))],
        out_specs=[row(d_ssm), row(d_hgrn), row(d_moba), row(d_gate)],
        out_shape=(jax.ShapeDtypeStruct((bsz, seq, d_ssm), F32),
                   jax.ShapeDtypeStruct((bsz, seq, d_hgrn), F32),
                   jax.ShapeDtypeStruct((bsz, seq, d_moba), F32),
                   jax.ShapeDtypeStruct((bsz, seq, d_gate), BF16)),
        compiler_params=_params("arbitrary", "arbitrary"),
        name="in_proj",
    )(x, mod, norm_g.reshape(1, d), w_in_bf)


def _s5_kernel(u_ref, wb_ref, wc_ref, abar_ref, d_ref, wglu_ref, o_ref, bu_ref, st_ref,
               *, nb, tl, ns, unroll):
    nc = ns // LANES
    rs = tl + S5_ROW_PAD

    @pl.when(pl.program_id(0) == 0)
    def _():
        st_ref[...] = jnp.zeros_like(st_ref)

    for b in range(nb):
        bu = _dot(u_ref[b].astype(BF16), wb_ref[...])
        for c in range(2 * nc):
            bu_ref[c, b * rs:b * rs + tl, :] = bu[:, c * LANES:(c + 1) * LANES]

    ar = [jnp.broadcast_to(abar_ref[0:1, c * LANES:(c + 1) * LANES], (nb, LANES)) for c in range(nc)]
    ai = [jnp.broadcast_to(abar_ref[1:2, c * LANES:(c + 1) * LANES], (nb, LANES)) for c in range(nc)]

    def outer(j, carry):
        sr, si = list(carry[:nc]), list(carry[nc:])
        for k in range(unroll):
            t = j * unroll + k
            for c in range(nc):
                br = bu_ref[c, pl.ds(t, nb, stride=rs), :]
                bi = bu_ref[nc + c, pl.ds(t, nb, stride=rs), :]
                nr = ar[c] * sr[c] - ai[c] * si[c] + br
                ni = ar[c] * si[c] + ai[c] * sr[c] + bi
                bu_ref[c, pl.ds(t, nb, stride=rs), :] = nr
                bu_ref[nc + c, pl.ds(t, nb, stride=rs), :] = ni
                sr[c], si[c] = nr, ni
        return tuple(sr) + tuple(si)

    init = tuple(st_ref[c] for c in range(2 * nc))
    fin = lax.fori_loop(0, tl // unroll, outer, init)
    for c in range(2 * nc):
        st_ref[c] = fin[c]

    dm = o_ref.shape[-1]
    for b in range(nb):
        s = jnp.concatenate([bu_ref[c, b * rs:b * rs + tl, :] for c in range(2 * nc)], axis=1)
        y = _dot(s.astype(BF16), wc_ref[...]) + d_ref[...] * u_ref[b]
        z = _dot(jax.nn.gelu(y).astype(BF16), wglu_ref[...])
        o_ref[b] = (z[:, :dm] * _sigmoid(z[:, dm:])).astype(o_ref.dtype)


def _s5_mixer(u, wb, wc, abar, d_skip, wglu, tl):
    bsz, seq, d_ssm = u.shape
    ns = abar.shape[-1]
    dm = wglu.shape[1] // 2
    return pl.pallas_call(
        functools.partial(_s5_kernel, nb=bsz, tl=tl, ns=ns, unroll=8),
        grid=(seq // tl,),
        in_specs=[pl.BlockSpec((bsz, tl, d_ssm), lambda i: (0, i, 0)),
                  _const_spec(wb.shape), _const_spec(wc.shape), _const_spec(abar.shape),
                  _const_spec((1, d_ssm)), _const_spec(wglu.shape)],
        out_specs=pl.BlockSpec((bsz, tl, dm), lambda i: (0, i, 0)),
        out_shape=jax.ShapeDtypeStruct((bsz, seq, dm), BF16),
        scratch_shapes=[pltpu.VMEM((2 * ns // LANES, bsz * (tl + S5_ROW_PAD), LANES), F32),
                        pltpu.VMEM((2 * ns // LANES, bsz, LANES), F32)],
        compiler_params=_params("arbitrary"),
        name="s5_mixer",
    )(u, wb, wc, abar, d_skip.reshape(1, d_ssm), wglu)


def _hgrn_kernel(x_ref, lb_ref, ng_ref, o_ref, st_ref, *, tl, hd, dk):
    cs, sb = HGRN_CHUNK, HGRN_SUB

    @pl.when(pl.program_id(1) == 0)
    def _():
        st_ref[...] = jnp.zeros_like(st_ref)

    def iota(shape, dim):
        return lax.broadcasted_iota(I32, shape, dim)

    tril = (iota((cs, cs), 0) >= iota((cs, cs), 1)).astype(F32)
    head_eq = (iota((hd, hd), 0) // dk) == (iota((hd, hd), 1) // dk)
    block_ones = head_eq.astype(BF16)
    block_mean = head_eq.astype(F32) * (1.0 / dk)
    sel = ((iota((sb, sb * sb), 1) // sb) == iota((sb, sb * sb), 0)).astype(BF16)
    s_iota = iota((sb, hd), 0)
    lane_head = iota((1, hd), 1) // dk
    n_heads = hd // dk
    lb = lb_ref[...]
    ng = ng_ref[...]

    def head_copies(a):
        return jnp.concatenate([jnp.where(lane_head == h, a, 0.0) for h in range(n_heads)], axis=0)

    def off_diag_scores(q, key, cum, t0, t1, s0, s1):
        r = cum[s1 - 1:s1, :]
        qt = q[t0:t1] * jnp.exp(cum[t0:t1] - r)
        ks = key[s0:s1] * jnp.exp(r - cum[s0:s1])
        return _nt_dot(qt.astype(BF16), head_copies(ks).astype(BF16))

    nch = tl // cs
    half = cs // 2
    spans = ((half, cs, 0, half), (sb, half, 0, sb), (half + sb, cs, half, half + sb))
    pre = []
    for c in range(nch):
        r0 = c * cs
        q = x_ref[r0:r0 + cs, 0:hd]
        fr = x_ref[r0:r0 + cs, hd:2 * hd]
        iv = x_ref[r0:r0 + cs, 2 * hd:3 * hd]
        forget = lb + (1.0 - lb) * _sigmoid(fr)
        key = 1.0 - forget
        cum = _dot_exact_lhs(tril, jnp.log(forget), 3)
        pre.append((q, key, iv, cum))
    heads_sum = []
    for q, key, iv, cum in pre:
        for a in range(cs // sb):
            lo = a * sb
            cum_a, qa, ka = cum[lo:lo + sb], q[lo:lo + sb], key[lo:lo + sb]
            slabs = []
            for t in range(sb):
                arg = jnp.where(s_iota <= t, cum_a[t:t + 1] - cum_a, -jnp.inf)
                slabs.append(jnp.exp(arg) * (qa[t:t + 1] * ka))
            heads_sum.append(_dot(jnp.concatenate(slabs, axis=0).astype(BF16), block_ones))
    off_scores = [[off_diag_scores(q, key, cum, *sp) for sp in spans] for q, key, iv, cum in pre]
    intras = []
    for c, (q, key, iv, cum) in enumerate(pre):
        parts = []
        for a in range(cs // sb):
            lo = a * sb
            w = heads_sum[c * (cs // sb) + a] * jnp.concatenate([iv[lo:lo + sb]] * sb, axis=0)
            parts.append(_dot(sel, w.astype(BF16)))
        lvl1, lvl2a, lvl2b = [
            _dot(sc.astype(BF16), head_copies(iv[sp[2]:sp[3]]).astype(BF16))
            for sc, sp in zip(off_scores[c], spans)]
        zeros = jnp.zeros((sb, hd), F32)
        intras.append(jnp.concatenate(parts, axis=0)
                      + jnp.concatenate([zeros, lvl2a, lvl1[0:sb], lvl1[sb:half] + lvl2b], axis=0))
    upds, decays, qcs = [], [], []
    for q, key, iv, cum in pre:
        last = cum[cs - 1:cs, :]
        kc = key * jnp.exp(last - cum)
        upds.append(jnp.where(head_eq, _tn_dot(iv.astype(BF16), kc.astype(BF16)), 0.0))
        decays.append(jnp.exp(last))
        qcs.append((q * jnp.exp(cum)).astype(BF16))
    st = st_ref[...]
    for c in range(nch):
        out = intras[c] + _nt_dot(qcs[c], st.astype(BF16))
        st = st * decays[c] + upds[c]
        ms = _dot_exact_rhs(out * out, block_mean, 2)
        g = x_ref[c * cs:(c + 1) * cs, 3 * hd:4 * hd]
        o_ref[c * cs:(c + 1) * cs, :] = (out * lax.rsqrt(ms + EPS) * ng * _silu(g)).astype(o_ref.dtype)
    st_ref[...] = st


def _hgrn_mixer(hg, lb, norm_g, tl):
    bsz, seq, d4 = hg.shape
    hd = d4 // 4
    dk = norm_g.shape[-1]
    ng = jnp.tile(norm_g, hd // dk).reshape(1, hd)
    return pl.pallas_call(
        functools.partial(_hgrn_kernel, tl=tl, hd=hd, dk=dk),
        grid=(bsz, seq // tl),
        in_specs=[pl.BlockSpec((None, tl, d4), lambda b, i: (b, i, 0)),
                  _const_spec((1, hd)), _const_spec((1, hd))],
        out_specs=pl.BlockSpec((None, tl, hd), lambda b, i: (b, i, 0)),
        out_shape=jax.ShapeDtypeStruct((bsz, seq, hd), BF16),
        scratch_shapes=[pltpu.VMEM((hd, hd), F32)],
        compiler_params=_params("arbitrary", "arbitrary"),
        name="hgrn_mixer",
    )(hg, lb.reshape(1, hd), ng)


def _moba_prep_kernel(x_ref, gq_ref, gk_ref, qt_ref, ka_ref, km_ref, vt_ref, *, nh, dh):
    i = pl.program_id(1)
    blk = MOBA_BLOCK
    hd = nh * dh

    @pl.when(i == 0)
    def _():
        km_ref[...] = jnp.zeros_like(km_ref)

    lane = lax.broadcasted_iota(I32, (blk, LANES - dh), 1)
    onehot = (lane == i).astype(F32)
    pad = jnp.zeros((blk, LANES - dh), F32)
    scale = dh ** -0.5
    for h in range(nh):
        qh = x_ref[:, h * dh:(h + 1) * dh]
        kh = x_ref[:, hd + h * dh:hd + (h + 1) * dh]
        vh = x_ref[:, 2 * hd + h * dh:2 * hd + (h + 1) * dh]
        qn = qh * lax.rsqrt(jnp.mean(qh * qh, axis=-1, keepdims=True) + EPS) * gq_ref[...] * scale
        kn = kh * lax.rsqrt(jnp.mean(kh * kh, axis=-1, keepdims=True) + EPS) * gk_ref[...]
        qt_ref[h] = jnp.concatenate([qn, pad], axis=1).T
        ka_ref[h] = jnp.concatenate([kn, onehot], axis=1).astype(BF16)
        kmean = jnp.mean(kn, axis=0, keepdims=True)
        km_ref[h, pl.ds(i, 1), :] = jnp.concatenate([kmean, jnp.zeros((1, LANES - dh), F32)], axis=1)
        vt_ref[h] = jnp.concatenate([vh, pad + 1.0], axis=1).T[0:dh + MOBA_SUM_ROWS].astype(BF16)


def _moba_kernel(qt_ref, ka_ref, km_ref, vt_ref, o_ref, qa_scr, m_scr, acc_scr, *, nh, dh, nbp):
    i = pl.program_id(1)
    blk = MOBA_BLOCK
    r0 = pl.multiple_of(i * blk, blk)
    b_iota = lax.broadcasted_iota(I32, (nbp, blk), 0)
    key_i = lax.broadcasted_iota(I32, (blk, blk), 0)
    qry_i = lax.broadcasted_iota(I32, (blk, blk), 1)

    for h in range(nh):
        qt = qt_ref[h]
        gate = _dot(km_ref[h], qt, HIGHEST)
        work = jnp.where(b_iota < i, gate, -jnp.inf)
        chosen = jnp.zeros((nbp, blk), jnp.bool_)
        for _ in range(MOBA_TOPK):
            mx = jnp.max(work, axis=0, keepdims=True)
            first = jnp.min(jnp.where(work == mx, b_iota, nbp), axis=0, keepdims=True)
            hit = (b_iota == first) & (mx > -jnp.inf)
            chosen = chosen | hit
            work = jnp.where(hit, -jnp.inf, work)
        pen = jnp.where(chosen, 0.0, -MASK_BIG)
        qa_scr[h] = jnp.concatenate(
            [qt[0:dh], pen, jnp.zeros((LANES - dh - nbp, blk), F32)], axis=0).astype(BF16)
        s = _dot(ka_ref[h, pl.ds(r0, blk), :], qt.astype(BF16))
        s = jnp.where(key_i <= qry_i, s, -jnp.inf)
        m0 = jnp.max(s, axis=0, keepdims=True)
        m_scr[h] = m0
        acc_scr[h] = _dot(vt_ref[h, i], jnp.exp(s - m0).astype(BF16))

    def past_blocks(js):
        scores = [[_dot(ka_ref[h, pl.ds(pl.multiple_of(j * blk, blk), blk), :], qa_scr[h]) for j in js]
                  for h in range(nh)]
        probs, alphas = [], []
        for h in range(nh):
            m_old = m_scr[h]
            m_new = m_old
            for s in scores[h]:
                m_new = jnp.maximum(m_new, jnp.max(s, axis=0, keepdims=True))
            probs.append([jnp.exp(s - m_new).astype(BF16) for s in scores[h]])
            alphas.append(jnp.exp(m_old - m_new))
            m_scr[h] = m_new
        for h in range(nh):
            acc = alphas[h] * acc_scr[h]
            for j, p in zip(js, probs[h]):
                acc = acc + _dot(vt_ref[h, j], p)
            acc_scr[h] = acc

    group = 4

    def past_group(jj, carry):
        past_blocks(tuple(group * jj + k for k in range(group)))
        return carry

    lax.fori_loop(0, i // group, past_group, 0)
    done = (i // group) * group
    for width in (2, 1):
        has = (i % (2 * width)) >= width

        @pl.when(has)
        def _(done=done, width=width):
            past_blocks(tuple(done + k for k in range(width)))

        done = done + jnp.where(has, width, 0)

    for h in range(nh):
        acc = acc_scr[h]
        o_ref[h * dh:(h + 1) * dh, :] = (acc[0:dh] / acc[dh:dh + 1]).astype(o_ref.dtype)


def _moba_mixer(mo, gq, gk):
    bsz, seq, d3 = mo.shape
    hd = d3 // 3
    dh = gq.shape[-1]
    nh = hd // dh
    blk = MOBA_BLOCK
    nblk = seq // blk
    nbp = -(-nblk // 8) * 8
    vr = dh + MOBA_SUM_ROWS
    assert seq % blk == 0 and dh + nbp <= LANES
    qt, ka, km, vt = pl.pallas_call(
        functools.partial(_moba_prep_kernel, nh=nh, dh=dh),
        grid=(bsz, nblk),
        in_specs=[pl.BlockSpec((None, blk, d3), lambda b, i: (b, i, 0)),
                  _const_spec((1, dh)), _const_spec((1, dh))],
        out_specs=[pl.BlockSpec((None, nh, LANES, blk), lambda b, i: (b, 0, 0, i)),
                   pl.BlockSpec((None, nh, blk, LANES), lambda b, i: (b, 0, i, 0)),
                   pl.BlockSpec((None, nh, nbp, LANES), lambda b, i: (b, 0, 0, 0)),
                   pl.BlockSpec((None, nh, None, vr, blk), lambda b, i: (b, 0, i, 0, 0))],
        out_shape=(jax.ShapeDtypeStruct((bsz, nh, LANES, seq), F32),
                   jax.ShapeDtypeStruct((bsz, nh, seq, LANES), BF16),
                   jax.ShapeDtypeStruct((bsz, nh, nbp, LANES), F32),
                   jax.ShapeDtypeStruct((bsz, nh, nblk, vr, blk), BF16)),
        compiler_params=_params("arbitrary", "arbitrary"),
        name="moba_prep",
    )(mo, gq.reshape(1, dh), gk.reshape(1, dh))
    return pl.pallas_call(
        functools.partial(_moba_kernel, nh=nh, dh=dh, nbp=nbp),
        grid=(bsz, nblk),
        in_specs=[pl.BlockSpec((None, nh, LANES, blk), lambda b, i: (b, 0, 0, i)),
                  pl.BlockSpec((None, nh, seq, LANES), lambda b, i: (b, 0, 0, 0)),
                  pl.BlockSpec((None, nh, nbp, LANES), lambda b, i: (b, 0, 0, 0)),
                  pl.BlockSpec((None, nh, nblk, vr, blk), lambda b, i: (b, 0, 0, 0, 0))],
        out_specs=pl.BlockSpec((None, hd, blk), lambda b, i: (b, 0, i)),
        out_shape=jax.ShapeDtypeStruct((bsz, hd, seq), BF16),
        scratch_shapes=[pltpu.VMEM((nh, LANES, blk), BF16),
                        pltpu.VMEM((nh, 1, blk), F32),
                        pltpu.VMEM((nh, vr, blk), F32)],
        compiler_params=_params("arbitrary", "arbitrary"),
        name="moba_attn",
    )(qt, ka, km, vt)


def _merge_kernel(x_ref, ya_ref, hb_ref, mc_ref, gl_ref, mod_ref, g2_ref, who_ref, wmo_ref, wout_ref,
                  rwh_ref, rwl_ref, rb_ref,
                  x1_ref, h2_ref, idx_ref, wgt_ref, pos_ref, cnt_ref, *, tm, ne):
    first = (pl.program_id(0) == 0) & (pl.program_id(1) == 0)

    @pl.when(first)
    def _():
        cnt_ref[...] = jnp.zeros_like(cnt_ref)

    d = x_ref.shape[-1]
    yb = _dot(hb_ref[...], who_ref[...])
    yc = _tn_dot(mc_ref[...], wmo_ref[...])
    merged = (_sigmoid(gl_ref[:, 0:d].astype(F32)) * ya_ref[...].astype(F32)
              + _sigmoid(gl_ref[:, d:2 * d].astype(F32)) * yb
              + _sigmoid(gl_ref[:, 2 * d:3 * d].astype(F32)) * yc)
    x1 = x_ref[...] + mod_ref[2:3, :] * _dot(merged.astype(BF16), wout_ref[...])
    x1_ref[...] = x1
    ms = jnp.mean(x1 * x1, axis=-1, keepdims=True)
    h2 = x1 * lax.rsqrt(ms + EPS) * g2_ref[...]
    h2 = h2 * (1.0 + mod_ref[4:5, :]) + mod_ref[3:4, :]
    h2_ref[...] = _pack_bf16_pairs(h2)
    hi = h2.astype(BF16)
    lo = (h2 - hi.astype(F32)).astype(BF16)
    logits = _nt_dot(rwh_ref[...], hi) + _nt_dot(rwh_ref[...], lo) + _nt_dot(rwl_ref[...], hi)
    scores = _sigmoid(logits)
    e_iota = lax.broadcasted_iota(I32, (ne, tm), 0)
    work = scores + rb_ref[...]
    picked = jnp.zeros((ne, tm), F32)
    hits, tops = [], []
    for _ in range(TOP_K):
        mx = jnp.max(work, axis=0, keepdims=True)
        first_e = jnp.min(jnp.where(work == mx, e_iota, ne), axis=0, keepdims=True)
        hit = e_iota == first_e
        hits.append((hit, first_e))
        tops.append(jnp.sum(jnp.where(hit, scores, 0.0), axis=0, keepdims=True))
        work = jnp.where(hit, -jnp.inf, work)
        picked = jnp.where(hit, 1.0, picked)
    total = functools.reduce(lambda a, b: a + b, tops)
    upper = (lax.broadcasted_iota(I32, (tm, tm), 0) < lax.broadcasted_iota(I32, (tm, tm), 1)).astype(BF16)
    rank = _dot(picked.astype(BF16), upper) + cnt_ref[:, 0:1]
    zero_row = jnp.zeros((1, tm), F32)
    idx_rows, w_rows, pos_rows = [], [], []
    for (hit, first_e), top in zip(hits, tops):
        idx_rows.append(first_e)
        w_rows.append(top / total * ROUTED_SCALE)
        pos_rows.append(jnp.sum(jnp.where(hit, rank, 0.0), axis=0, keepdims=True))
    pad = 8 - TOP_K
    idx_ref[...] = jnp.concatenate(idx_rows + [jnp.zeros((1, tm), I32)] * pad, axis=0)
    wgt_ref[...] = jnp.concatenate(w_rows + [zero_row] * pad, axis=0)
    pos_ref[...] = jnp.concatenate(pos_rows + [zero_row] * pad, axis=0).astype(I32)
    cnt_ref[...] = cnt_ref[...] + jnp.sum(picked, axis=1, keepdims=True)


def _merge_router(x, ya, hb, mc, gl, mod, norm2_g, who, wmo, wout, rwh, rwl, rbias, tm):
    bsz, seq, d = x.shape
    ne = rwh.shape[0]
    nt = seq // tm
    n_tok = bsz * seq
    row = lambda n: pl.BlockSpec((None, tm, n), lambda b, i: (b, i, 0))
    col = pl.BlockSpec((8, tm), lambda b, i: (0, b * nt + i))
    return pl.pallas_call(
        functools.partial(_merge_kernel, tm=tm, ne=ne),
        grid=(bsz, nt),
        in_specs=[row(d), row(d), row(hb.shape[-1]),
                  pl.BlockSpec((None, mc.shape[1], tm), lambda b, i: (b, 0, i)), row(gl.shape[-1]),
                  pl.BlockSpec((None, N_MOD, d), lambda b, i: (b, 0, 0)),
                  _const_spec((1, d)), _const_spec(who.shape), _const_spec(wmo.shape),
                  _const_spec(wout.shape), _const_spec(rwh.shape), _const_spec(rwl.shape),
                  _const_spec((ne, 1))],
        out_specs=[row(d), row(d // 2), col, col, col,
                   pl.BlockSpec((ne, LANES), lambda b, i: (0, 0))],
        out_shape=(jax.ShapeDtypeStruct((bsz, seq, d), F32),
                   jax.ShapeDtypeStruct((bsz, seq, d // 2), jnp.uint32),
                   jax.ShapeDtypeStruct((8, n_tok), I32),
                   jax.ShapeDtypeStruct((8, n_tok), F32),
                   jax.ShapeDtypeStruct((8, n_tok), I32),
                   jax.ShapeDtypeStruct((ne, LANES), F32)),
        compiler_params=_params("arbitrary", "arbitrary"),
        name="merge_router",
    )(x, ya, hb, mc, gl, mod, norm2_g.reshape(1, d), who, wmo, wout, rwh, rwl, rbias.reshape(ne, 1))


def _expert_kernel(be_ref, nv_ref, x_ref, wg_ref, wu_ref, wd_ref, y_ref, *, rb):
    nv = nv_ref[pl.program_id(0)]

    @pl.when(nv > 0)
    def _():
        rows = lax.broadcasted_iota(I32, (rb, 1), 0)
        x = jnp.where(rows < nv, _unpack_bf16_pairs(x_ref[...]), 0.0).astype(BF16)
        g = _dot(x, wg_ref[...])
        u = _dot(x, wu_ref[...])
        y_ref[...] = _pack_bf16_pairs(_dot((_silu(g) * u).astype(BF16), wd_ref[...]))


def _experts(block_expert, block_rows, xs, wg, wu, wd, rb):
    n_rows = xs.shape[0]
    n_blocks = n_rows // rb
    _, d, de = wg.shape
    live = lambda i, be, nv: jnp.where(nv[i] > 0, i, n_blocks - 1)
    grid_spec = pltpu.PrefetchScalarGridSpec(
        num_scalar_prefetch=2,
        grid=(n_blocks,),
        in_specs=[pl.BlockSpec((rb, d // 2), lambda i, be, nv: (live(i, be, nv), 0)),
                  pl.BlockSpec((None, d, de), lambda i, be, nv: (be[i], 0, 0)),
                  pl.BlockSpec((None, d, de), lambda i, be, nv: (be[i], 0, 0)),
                  pl.BlockSpec((None, de, d), lambda i, be, nv: (be[i], 0, 0))],
        out_specs=pl.BlockSpec((rb, d // 2), lambda i, be, nv: (live(i, be, nv), 0)),
    )
    return pl.pallas_call(
        functools.partial(_expert_kernel, rb=rb),
        grid_spec=grid_spec,
        out_shape=jax.ShapeDtypeStruct((n_rows, d // 2), jnp.uint32),
        compiler_params=_params("arbitrary"),
        name="moe_experts",
    )(block_expert, block_rows, xs, wg, wu, wd)


def _sc_gather_rows(table, idx):
    n_idx = idx.shape[0]
    width = table.shape[1]
    n_workers = SC_CORES * SC_SUBCORES
    per_worker = n_idx // n_workers
    assert n_idx % (n_workers * SC_GATHER_CHUNK) == 0
    mesh = plsc.VectorSubcoreMesh(core_axis_name="c", subcore_axis_name="s")

    def body(table_hbm, idx_hbm, out_hbm, idx_v, rows_v, sem):
        wid = lax.axis_index("s") * SC_CORES + lax.axis_index("c")

        @pl.loop(0, per_worker // SC_GATHER_CHUNK)
        def _(j):
            base = wid * per_worker + j * SC_GATHER_CHUNK
            pltpu.sync_copy(idx_hbm.at[pl.ds(base, SC_GATHER_CHUNK)], idx_v)
            pltpu.async_copy(table_hbm.at[idx_v], rows_v, sem).wait()
            pltpu.sync_copy(rows_v, out_hbm.at[pl.ds(base, SC_GATHER_CHUNK)])

    return pl.kernel(
        body, mesh=mesh,
        out_type=jax.ShapeDtypeStruct((n_idx, width), table.dtype),
        scratch_types=[pltpu.VMEM((SC_GATHER_CHUNK,), I32),
                       pltpu.VMEM((SC_GATHER_CHUNK, width), table.dtype),
                       pltpu.SemaphoreType.DMA],
        name="moe_gather_sc",
    )(table, idx)


def _sc_scatter_rows(rows, idx, n_out, copies):
    n, width = rows.shape
    n_workers = SC_CORES * SC_SUBCORES
    per_worker = n // n_workers
    assert n % (n_workers * SC_GATHER_CHUNK) == 0
    mesh = plsc.VectorSubcoreMesh(core_axis_name="c", subcore_axis_name="s")

    def body(rows_hbm, idx_hbm, out_hbm, idx_v, rows_v):
        wid = lax.axis_index("s") * SC_CORES + lax.axis_index("c")

        @pl.loop(0, per_worker // SC_GATHER_CHUNK)
        def _(j):
            base = wid * per_worker + j * SC_GATHER_CHUNK
            pltpu.sync_copy(rows_hbm.at[pl.ds(base, SC_GATHER_CHUNK)], rows_v)
            for k in range(copies):
                pltpu.sync_copy(idx_hbm.at[pl.ds(k * n + base, SC_GATHER_CHUNK)], idx_v)
                pltpu.sync_copy(rows_v, out_hbm.at[idx_v])

    return pl.kernel(
        body, mesh=mesh,
        out_type=jax.ShapeDtypeStruct((n_out, width), rows.dtype),
        scratch_types=[pltpu.VMEM((SC_GATHER_CHUNK,), I32),
                       pltpu.VMEM((SC_GATHER_CHUNK, width), rows.dtype)],
        name="moe_dispatch_sc",
    )(rows, idx)


def _combine_kernel(wgt_ref, x1_ref, h2_ref, mod_ref, sg_ref, su_ref, sd_ref, g_ref, o_ref):
    hb = _unpack_bf16_pairs(h2_ref[...]).astype(BF16)
    hid = _silu(_dot(hb, sg_ref[...])) * _dot(hb, su_ref[...])
    y = _dot(hid.astype(BF16), sd_ref[...])
    wcol = wgt_ref[...].T
    for k in range(TOP_K):
        y = y + _unpack_bf16_pairs(g_ref[k]) * wcol[:, k:k + 1]
    o_ref[...] = x1_ref[...] + mod_ref[5:6, :] * y


def _combine(wgt, x1, h2, mod, sg, su, sd, gathered, tm):
    bsz, seq, d = x1.shape
    nt = seq // tm
    row = pl.BlockSpec((None, tm, d), lambda b, i: (b, i, 0))
    return pl.pallas_call(
        _combine_kernel,
        grid=(bsz, nt),
        in_specs=[pl.BlockSpec((8, tm), lambda b, i: (0, b * nt + i)),
                  row, pl.BlockSpec((None, tm, d // 2), lambda b, i: (b, i, 0)),
                  pl.BlockSpec((None, N_MOD, d), lambda b, i: (b, 0, 0)),
                  _const_spec(sg.shape), _const_spec(su.shape), _const_spec(sd.shape),
                  pl.BlockSpec((TOP_K, tm, d // 2), lambda b, i: (0, b * nt + i, 0))],
        out_specs=row,
        out_shape=jax.ShapeDtypeStruct((bsz, seq, d), F32),
        compiler_params=_params("arbitrary", "arbitrary"),
        name="moe_combine",
    )(wgt, x1, h2, mod, sg, su, sd, gathered)


def _dest_kernel(ps_ref, idx_ref, pos_ref, o_ref, *, ne):
    idx = idx_ref[...]
    acc = pos_ref[...]
    for e in range(ne):
        acc = acc + jnp.where(idx == e, ps_ref[e], 0)
    o_ref[...] = acc


def _routing_tables(idx, pos, counts, n_tok, ne, rb):
    n_assign = n_tok * TOP_K
    n_blocks = -(-n_assign // rb) + ne
    padded = (counts + rb - 1) // rb * rb
    pend = jnp.cumsum(padded)
    pstart = (pend - padded).astype(I32)
    tt = min(8192, n_tok)
    col = pl.BlockSpec((8, tt), lambda i: (0, i))
    dest = pl.pallas_call(
        functools.partial(_dest_kernel, ne=ne),
        grid=(n_tok // tt,),
        in_specs=[pl.BlockSpec(memory_space=pltpu.SMEM), col, col],
        out_specs=col,
        out_shape=jax.ShapeDtypeStruct((8, n_tok), I32),
        compiler_params=_params("arbitrary"),
        name="moe_dest",
    )(pstart, idx, pos)
    bstart = jnp.arange(n_blocks, dtype=I32) * rb
    bexp = jnp.minimum(jnp.sum((pend[None, :] <= bstart[:, None]).astype(I32), axis=1), ne - 1)
    brows = jnp.clip(counts[bexp] - (bstart - pstart[bexp]), 0, rb).astype(I32)
    return dest, bexp, brows, n_blocks


def _block_diag_in(bbt, g):
    gw, n = bbt.shape
    p = n // g
    rows = jnp.tile(bbt, (g, 1))
    rg = jnp.arange(g * gw)[:, None] // gw
    cg = jnp.arange(n)[None, :] // p
    return jnp.where(rg == cg, rows, 0.0)


def _block_diag_out(c, sign):
    g, gw, p = c.shape
    cols = jnp.tile(c.transpose(0, 2, 1).reshape(g * p, gw), (1, g))
    rg = jnp.arange(g * p)[:, None] // p
    cg = jnp.arange(g * gw)[None, :] // gw
    return jnp.where(rg == cg, sign * cols, 0.0)


def kernel(x, c, ada_w, ada_b, norm1_g, w_in, ssm_log_dt, ssm_lambda_re, ssm_lambda_im, ssm_b_re,
           ssm_b_im, ssm_c_re, ssm_c_im, ssm_d, ssm_w_glu, hgrn_lb_logits, hgrn_norm_g, hgrn_w_o,
           moba_q_norm_g, moba_k_norm_g, moba_w_o, w_out, norm2_g, router_w, router_bias,
           exp_w_gate, exp_w_up, exp_w_down, shared_w_gate, shared_w_up, shared_w_down):
    bsz, seq, d = x.shape
    depth = w_in.shape[0]
    n_tok = bsz * seq
    g = ssm_lambda_re.shape[1]
    d_ssm = ssm_d.shape[-1]
    d_hk = hgrn_lb_logits.shape[-1]
    d_moba = moba_w_o.shape[1]
    ne = router_w.shape[-1]
    widths = (d_ssm, 4 * d_hk, 3 * d_moba, 3 * d)
    tm = min(256, seq)
    rb = 512

    mod_all = _ada_mod(c, ada_w, ada_b).reshape(depth, bsz, N_MOD, d)
    abar, bbr, bbi, lower = _prep_params(ssm_log_dt, ssm_lambda_re, ssm_lambda_im, ssm_b_re, ssm_b_im,
                                         hgrn_lb_logits)

    for l in range(depth):
        mod = mod_all[l]
        u, hg, mo, gl = _inproj(x, mod, norm1_g[l], w_in[l].astype(BF16), widths, min(512, seq))
        wb = jnp.concatenate([_block_diag_in(bbr[l], g), _block_diag_in(bbi[l], g)], axis=1).astype(BF16)
        wc = jnp.concatenate([_block_diag_out(ssm_c_re[l], 1.0), _block_diag_out(ssm_c_im[l], -1.0)],
                             axis=0).astype(BF16)
        ya = _s5_mixer(u, wb, wc, abar[l], ssm_d[l], ssm_w_glu[l].astype(BF16), min(128, seq))
        hb = _hgrn_mixer(hg, lower[l], hgrn_norm_g[l], min(256, seq))
        mc = _moba_mixer(mo, moba_q_norm_g[l], moba_k_norm_g[l])
        rwt = router_w[l].T
        rwh = rwt.astype(BF16)
        rwl = (rwt - rwh.astype(F32)).astype(BF16)
        x1, h2, idx, wgt, pos, cnt = _merge_router(
            x, ya, hb, mc, gl, mod, norm2_g[l], hgrn_w_o[l].astype(BF16), moba_w_o[l].astype(BF16),
            w_out[l].astype(BF16), rwh, rwl, router_bias[l], tm)
        counts = cnt[:, 0].astype(I32)
        dest, bexp, brows, n_blocks = _routing_tables(idx, pos, counts, n_tok, ne, rb)
        dest_flat = dest[:TOP_K].reshape(-1)
        xs = _sc_scatter_rows(h2.reshape(n_tok, d // 2), dest_flat, n_blocks * rb, TOP_K)
        ys = _experts(bexp, brows, xs, exp_w_gate[l].astype(BF16), exp_w_up[l].astype(BF16),
                      exp_w_down[l].astype(BF16), rb)
        gathered = _sc_gather_rows(ys, dest_flat).reshape(TOP_K, n_tok, d // 2)
        x = _combine(wgt, x1, h2, mod, shared_w_gate[l].astype(BF16),
                     shared_w_up[l].astype(BF16), shared_w_down[l].astype(BF16), gathered, tm)
    return x
```

```python
import functools

import jax
import jax.numpy as jnp
from jax import lax
from jax.experimental import pallas as pl
from jax.experimental.pallas import tpu as pltpu
from jax.experimental.pallas import tpu_sc as plsc

F32 = jnp.float32
BF16 = jnp.bfloat16
I32 = jnp.int32
HIGHEST = lax.Precision.HIGHEST

N_MOD = 6
EPS = 1e-6
MOBA_BLOCK = 256
MOBA_TOPK = 3
TOP_K = 6
ROUTED_SCALE = 2.5

LANES = 128
SC_CORES = 2
SC_SUBCORES = 16
SC_GATHER_CHUNK = 128
HGRN_CHUNK = 64
HGRN_SUB = 16
S5_ROW_PAD = 8
MOBA_SUM_ROWS = 16
MASK_BIG = 30000.0
VMEM_LIMIT = 56 * 1024 * 1024


def _nt_dot(a, b, precision=None):
    return lax.dot_general(a, b, (((1,), (1,)), ((), ())), preferred_element_type=F32,
                           precision=precision)


def _tn_dot(a, b, precision=None):
    return lax.dot_general(a, b, (((0,), (0,)), ((), ())), preferred_element_type=F32,
                           precision=precision)


def _dot(a, b, precision=None):
    return jnp.dot(a, b, preferred_element_type=F32, precision=precision)


def _bf16_terms(x, n):
    terms = []
    for _ in range(n):
        t = x.astype(BF16)
        terms.append(t)
        x = x - t.astype(F32)
    return terms


def _dot_exact_lhs(a, b, n):
    ab = a.astype(BF16)
    return functools.reduce(lambda p, q: p + q, [_dot(ab, t) for t in _bf16_terms(b, n)])


def _dot_exact_rhs(a, b, n):
    bb = b.astype(BF16)
    return functools.reduce(lambda p, q: p + q, [_dot(t, bb) for t in _bf16_terms(a, n)])


def _pack_bf16_pairs(y):
    m = y.shape[1] // 2
    lo = lax.bitcast_convert_type(y[:, :m].astype(BF16).astype(F32), jnp.uint32)
    hi = lax.bitcast_convert_type(y[:, m:].astype(BF16).astype(F32), jnp.uint32)
    return hi | (lo >> 16)


def _unpack_bf16_pairs(w):
    lo = lax.bitcast_convert_type(w << 16, F32)
    hi = lax.bitcast_convert_type(w & jnp.uint32(0xFFFF0000), F32)
    return jnp.concatenate([lo, hi], axis=1)


def _sigmoid(x):
    return 1.0 / (1.0 + jnp.exp(-x))


def _silu(x):
    return x * _sigmoid(x)


def _params(*sem):
    return pltpu.CompilerParams(dimension_semantics=sem, vmem_limit_bytes=VMEM_LIMIT)


def _const_spec(shape):
    nd = len(shape)
    return pl.BlockSpec(shape, lambda *_: (0,) * nd, pipeline_mode=pl.Buffered(1))


def _ada_kernel(c_ref, w_ref, b_ref, o_ref):
    c = c_ref[...]
    o_ref[...] = _dot(_silu(c), w_ref[...], HIGHEST) + b_ref[...]


def _ada_mod(c, ada_w, ada_b):
    depth, d, n = ada_w.shape
    bsz = c.shape[0]
    tn = n // 4
    return pl.pallas_call(
        _ada_kernel,
        grid=(depth, n // tn),
        in_specs=[pl.BlockSpec((bsz, d), lambda l, j: (0, 0)),
                  pl.BlockSpec((None, d, tn), lambda l, j: (l, 0, j)),
                  pl.BlockSpec((None, 1, tn), lambda l, j: (l, 0, j))],
        out_specs=pl.BlockSpec((None, bsz, tn), lambda l, j: (l, 0, j)),
        out_shape=jax.ShapeDtypeStruct((depth, bsz, n), F32),
        compiler_params=_params("arbitrary", "arbitrary"),
        name="ada_mod",
    )(c, ada_w, ada_b.reshape(depth, 1, n))


def _prep_kernel(ldt_ref, lr_ref, li_ref, btr_ref, bti_ref, lbl_ref,
                 abar_ref, bbr_ref, bbi_ref, lb_ref):
    depth = ldt_ref.shape[0]
    for l in range(depth):
        dt = jnp.exp(ldt_ref[l])
        lr = lr_ref[l]
        li = li_ref[l]
        mag = jnp.exp(lr * dt)
        ar = mag * jnp.cos(li * dt)
        ai = mag * jnp.sin(li * dt)
        inv = 1.0 / (lr * lr + li * li)
        cr = ((ar - 1.0) * lr + ai * li) * inv
        ci = (ai * lr - (ar - 1.0) * li) * inv
        abar_ref[l, 0:1, :] = ar
        abar_ref[l, 1:2, :] = ai
        btr = btr_ref[l]
        bti = bti_ref[l]
        bbr_ref[l] = cr * btr - ci * bti
        bbi_ref[l] = cr * bti + ci * btr
    rows = [lbl_ref[l:l + 1, :] for l in range(depth)]
    mx = functools.reduce(jnp.maximum, rows)
    ex = [jnp.exp(r - mx) for r in rows]
    den = functools.reduce(lambda a, b: a + b, ex)
    run = ex[0] / den
    first = run
    lb_ref[0:1, :] = run - first
    for l in range(1, depth):
        run = run + ex[l] / den
        lb_ref[l:l + 1, :] = run - first


def _prep_params(log_dt, lam_re, lam_im, b_re, b_im, lb_logits):
    depth, g, p = lam_re.shape
    gw = b_re.shape[-1]
    n = g * p
    hk = lb_logits.shape[-1]
    ldt = jnp.repeat(log_dt, p, axis=1).reshape(depth, 1, n)
    btr = b_re.transpose(0, 3, 1, 2).reshape(depth, gw, n)
    bti = b_im.transpose(0, 3, 1, 2).reshape(depth, gw, n)
    return pl.pallas_call(
        _prep_kernel,
        out_shape=(jax.ShapeDtypeStruct((depth, 2, n), F32),
                   jax.ShapeDtypeStruct((depth, gw, n), F32),
                   jax.ShapeDtypeStruct((depth, gw, n), F32),
                   jax.ShapeDtypeStruct((depth, hk), F32)),
        name="param_prep",
    )(ldt, lam_re.reshape(depth, 1, n), lam_im.reshape(depth, 1, n), btr, bti, lb_logits)


def _inproj_kernel(x_ref, mod_ref, g_ref, w_ref, u_ref, hg_ref, mo_ref, gl_ref, *, widths):
    x = x_ref[...]
    ms = jnp.mean(x * x, axis=-1, keepdims=True)
    h = x * lax.rsqrt(ms + EPS) * g_ref[...]
    h = h * (1.0 + mod_ref[1:2, :]) + mod_ref[0:1, :]
    hb = h.astype(BF16)
    d_ssm, d_hgrn, d_moba, d_gate = widths
    o0 = 0
    u_ref[...] = _dot(hb, w_ref[:, o0:o0 + d_ssm])
    o0 += d_ssm
    hg_ref[...] = _dot(hb, w_ref[:, o0:o0 + d_hgrn])
    o0 += d_hgrn
    mo_ref[...] = _dot(hb, w_ref[:, o0:o0 + d_moba])
    o0 += d_moba
    step = 1024
    for j in range(0, d_gate, step):
        gl_ref[:, j:j + step] = _dot(hb, w_ref[:, o0 + j:o0 + j + step]).astype(BF16)


def _inproj(x, mod, norm_g, w_in_bf, widths, tm):
    bsz, seq, d = x.shape
    d_ssm, d_hgrn, d_moba, d_gate = widths
    d_in = w_in_bf.shape[1]
    row = lambda n: pl.BlockSpec((None, tm, n), lambda b, i: (b, i, 0))
    return pl.pallas_call(
        functools.partial(_inproj_kernel, widths=widths),
        grid=(bsz, seq // tm),
        in_specs=[row(d),
                  pl.BlockSpec((None, N_MOD, d), lambda b, i: (b, 0, 0)),
                  _const_spec((1, d)),
                  _const_spec((d, d_in))],
        out_specs=[row(d_ssm), row(d_hgrn), row(d_moba), row(d_gate)],
        out_shape=(jax.ShapeDtypeStruct((bsz, seq, d_ssm), F32),
                   jax.ShapeDtypeStruct((bsz, seq, d_hgrn), F32),
                   jax.ShapeDtypeStruct((bsz, seq, d_moba), F32),
                   jax.ShapeDtypeStruct((bsz, seq, d_gate), BF16)),
        compiler_params=_params("arbitrary", "arbitrary"),
        name="in_proj",
    )(x, mod, norm_g.reshape(1, d), w_in_bf)


def _s5_kernel(u_ref, wb_ref, wc_ref, abar_ref, d_ref, wglu_ref, o_ref, bu_ref, st_ref,
               *, nb, tl, ns, unroll):
    nc = ns // LANES
    rs = tl + S5_ROW_PAD

    @pl.when(pl.program_id(0) == 0)
    def _():
        st_ref[...] = jnp.zeros_like(st_ref)

    for b in range(nb):
        bu = _dot(u_ref[b].astype(BF16), wb_ref[...])
        for c in range(2 * nc):
            bu_ref[c, b * rs:b * rs + tl, :] = bu[:, c * LANES:(c + 1) * LANES]

    ar = [jnp.broadcast_to(abar_ref[0:1, c * LANES:(c + 1) * LANES], (nb, LANES)) for c in range(nc)]
    ai = [jnp.broadcast_to(abar_ref[1:2, c * LANES:(c + 1) * LANES], (nb, LANES)) for c in range(nc)]

    def outer(j, carry):
        sr, si = list(carry[:nc]), list(carry[nc:])
        for k in range(unroll):
            t = j * unroll + k
            for c in range(nc):
                br = bu_ref[c, pl.ds(t, nb, stride=rs), :]
                bi = bu_ref[nc + c, pl.ds(t, nb, stride=rs), :]
                nr = ar[c] * sr[c] - ai[c] * si[c] + br
                ni = ar[c] * si[c] + ai[c] * sr[c] + bi
                bu_ref[c, pl.ds(t, nb, stride=rs), :] = nr
                bu_ref[nc + c, pl.ds(t, nb, stride=rs), :] = ni
                sr[c], si[c] = nr, ni
        return tuple(sr) + tuple(si)

    init = tuple(st_ref[c] for c in range(2 * nc))
    fin = lax.fori_loop(0, tl // unroll, outer, init)
    for c in range(2 * nc):
        st_ref[c] = fin[c]

    dm = o_ref.shape[-1]
    for b in range(nb):
        s = jnp.concatenate([bu_ref[c, b * rs:b * rs + tl, :] for c in range(2 * nc)], axis=1)
        y = _dot(s.astype(BF16), wc_ref[...]) + d_ref[...] * u_ref[b]
        z = _dot(jax.nn.gelu(y).astype(BF16), wglu_ref[...])
        o_ref[b] = (z[:, :dm] * _sigmoid(z[:, dm:])).astype(o_ref.dtype)


def _s5_mixer(u, wb, wc, abar, d_skip, wglu, tl):
    bsz, seq, d_ssm = u.shape
    ns = abar.shape[-1]
    dm = wglu.shape[1] // 2
    return pl.pallas_call(
        functools.partial(_s5_kernel, nb=bsz, tl=tl, ns=ns, unroll=8),
        grid=(seq // tl,),
        in_specs=[pl.BlockSpec((bsz, tl, d_ssm), lambda i: (0, i, 0)),
                  _const_spec(wb.shape), _const_spec(wc.shape), _const_spec(abar.shape),
                  _const_spec((1, d_ssm)), _const_spec(wglu.shape)],
        out_specs=pl.BlockSpec((bsz, tl, dm), lambda i: (0, i, 0)),
        out_shape=jax.ShapeDtypeStruct((bsz, seq, dm), BF16),
        scratch_shapes=[pltpu.VMEM((2 * ns // LANES, bsz * (tl + S5_ROW_PAD), LANES), F32),
                        pltpu.VMEM((2 * ns // LANES, bsz, LANES), F32)],
        compiler_params=_params("arbitrary"),
        name="s5_mixer",
    )(u, wb, wc, abar, d_skip.reshape(1, d_ssm), wglu)


def _hgrn_kernel(x_ref, lb_ref, ng_ref, o_ref, st_ref, *, tl, hd, dk):
    cs, sb = HGRN_CHUNK, HGRN_SUB

    @pl.when(pl.program_id(1) == 0)
    def _():
        st_ref[...] = jnp.zeros_like(st_ref)

    def iota(shape, dim):
        return lax.broadcasted_iota(I32, shape, dim)

    tril = (iota((cs, cs), 0) >= iota((cs, cs), 1)).astype(F32)
    head_eq = (iota((hd, hd), 0) // dk) == (iota((hd, hd), 1) // dk)
    block_ones = head_eq.astype(BF16)
    block_mean = head_eq.astype(F32) * (1.0 / dk)
    sel = ((iota((sb, sb * sb), 1) // sb) == iota((sb, sb * sb), 0)).astype(BF16)
    s_iota = iota((sb, hd), 0)
    lane_head = iota((1, hd), 1) // dk
    n_heads = hd // dk
    lb = lb_ref[...]
    ng = ng_ref[...]

    def head_copies(a):
        return jnp.concatenate([jnp.where(lane_head == h, a, 0.0) for h in range(n_heads)], axis=0)

    def off_diag_scores(q, key, cum, t0, t1, s0, s1):
        r = cum[s1 - 1:s1, :]
        qt = q[t0:t1] * jnp.exp(cum[t0:t1] - r)
        ks = key[s0:s1] * jnp.exp(r - cum[s0:s1])
        return _nt_dot(qt.astype(BF16), head_copies(ks).astype(BF16))

    nch = tl // cs
    half = cs // 2
    spans = ((half, cs, 0, half), (sb, half, 0, sb), (half + sb, cs, half, half + sb))
    pre = []
    for c in range(nch):
        r0 = c * cs
        q = x_ref[r0:r0 + cs, 0:hd]
        fr = x_ref[r0:r0 + cs, hd:2 * hd]
        iv = x_ref[r0:r0 + cs, 2 * hd:3 * hd]
        forget = lb + (1.0 - lb) * _sigmoid(fr)
        key = 1.0 - forget
        cum = _dot_exact_lhs(tril, jnp.log(forget), 3)
        pre.append((q, key, iv, cum))
    heads_sum = []
    for q, key, iv, cum in pre:
        for a in range(cs // sb):
            lo = a * sb
            cum_a, qa, ka = cum[lo:lo + sb], q[lo:lo + sb], key[lo:lo + sb]
            slabs = []
            for t in range(sb):
                arg = jnp.where(s_iota <= t, cum_a[t:t + 1] - cum_a, -jnp.inf)
                slabs.append(jnp.exp(arg) * (qa[t:t + 1] * ka))
            heads_sum.append(_dot(jnp.concatenate(slabs, axis=0).astype(BF16), block_ones))
    off_scores = [[off_diag_scores(q, key, cum, *sp) for sp in spans] for q, key, iv, cum in pre]
    intras = []
    for c, (q, key, iv, cum) in enumerate(pre):
        parts = []
        for a in range(cs // sb):
            lo = a * sb
            w = heads_sum[c * (cs // sb) + a] * jnp.concatenate([iv[lo:lo + sb]] * sb, axis=0)
            parts.append(_dot(sel, w.astype(BF16)))
        lvl1, lvl2a, lvl2b = [
            _dot(sc.astype(BF16), head_copies(iv[sp[2]:sp[3]]).astype(BF16))
            for sc, sp in zip(off_scores[c], spans)]
        zeros = jnp.zeros((sb, hd), F32)
        intras.append(jnp.concatenate(parts, axis=0)
                      + jnp.concatenate([zeros, lvl2a, lvl1[0:sb], lvl1[sb:half] + lvl2b], axis=0))
    upds, decays, qcs = [], [], []
    for q, key, iv, cum in pre:
        last = cum[cs - 1:cs, :]
        kc = key * jnp.exp(last - cum)
        upds.append(jnp.where(head_eq, _tn_dot(iv.astype(BF16), kc.astype(BF16)), 0.0))
        decays.append(jnp.exp(last))
        qcs.append((q * jnp.exp(cum)).astype(BF16))
    st = st_ref[...]
    for c in range(nch):
        out = intras[c] + _nt_dot(qcs[c], st.astype(BF16))
        st = st * decays[c] + upds[c]
        ms = _dot_exact_rhs(out * out, block_mean, 2)
        g = x_ref[c * cs:(c + 1) * cs, 3 * hd:4 * hd]
        o_ref[c * cs:(c + 1) * cs, :] = (out * lax.rsqrt(ms + EPS) * ng * _silu(g)).astype(o_ref.dtype)
    st_ref[...] = st


def _hgrn_mixer(hg, lb, norm_g, tl):
    bsz, seq, d4 = hg.shape
    hd = d4 // 4
    dk = norm_g.shape[-1]
    ng = jnp.tile(norm_g, hd // dk).reshape(1, hd)
    return pl.pallas_call(
        functools.partial(_hgrn_kernel, tl=tl, hd=hd, dk=dk),
        grid=(bsz, seq // tl),
        in_specs=[pl.BlockSpec((None, tl, d4), lambda b, i: (b, i, 0)),
                  _const_spec((1, hd)), _const_spec((1, hd))],
        out_specs=pl.BlockSpec((None, tl, hd), lambda b, i: (b, i, 0)),
        out_shape=jax.ShapeDtypeStruct((bsz, seq, hd), BF16),
        scratch_shapes=[pltpu.VMEM((hd, hd), F32)],
        compiler_params=_params("arbitrary", "arbitrary"),
        name="hgrn_mixer",
    )(hg, lb.reshape(1, hd), ng)


def _moba_prep_kernel(x_ref, gq_ref, gk_ref, qt_ref, ka_ref, km_ref, vt_ref, *, nh, dh):
    i = pl.program_id(1)
    blk = MOBA_BLOCK
    hd = nh * dh

    @pl.when(i == 0)
    def _():
        km_ref[...] = jnp.zeros_like(km_ref)

    lane = lax.broadcasted_iota(I32, (blk, LANES - dh), 1)
    onehot = (lane == i).astype(F32)
    pad = jnp.zeros((blk, LANES - dh), F32)
    scale = dh ** -0.5
    for h in range(nh):
        qh = x_ref[:, h * dh:(h + 1) * dh]
        kh = x_ref[:, hd + h * dh:hd + (h + 1) * dh]
        vh = x_ref[:, 2 * hd + h * dh:2 * hd + (h + 1) * dh]
        qn = qh * lax.rsqrt(jnp.mean(qh * qh, axis=-1, keepdims=True) + EPS) * gq_ref[...] * scale
        kn = kh * lax.rsqrt(jnp.mean(kh * kh, axis=-1, keepdims=True) + EPS) * gk_ref[...]
        qt_ref[h] = jnp.concatenate([qn, pad], axis=1).T
        ka_ref[h] = jnp.concatenate([kn, onehot], axis=1).astype(BF16)
        kmean = jnp.mean(kn, axis=0, keepdims=True)
        km_ref[h, pl.ds(i, 1), :] = jnp.concatenate([kmean, jnp.zeros((1, LANES - dh), F32)], axis=1)
        vt_ref[h] = jnp.concatenate([vh, pad + 1.0], axis=1).T[0:dh + MOBA_SUM_ROWS].astype(BF16)


def _moba_kernel(qt_ref, ka_ref, km_ref, vt_ref, o_ref, qa_scr, m_scr, acc_scr, *, nh, dh, nbp):
    i = pl.program_id(1)
    blk = MOBA_BLOCK
    r0 = pl.multiple_of(i * blk, blk)
    b_iota = lax.broadcasted_iota(I32, (nbp, blk), 0)
    key_i = lax.broadcasted_iota(I32, (blk, blk), 0)
    qry_i = lax.broadcasted_iota(I32, (blk, blk), 1)

    for h in range(nh):
        qt = qt_ref[h]
        gate = _dot(km_ref[h], qt, HIGHEST)
        work = jnp.where(b_iota < i, gate, -jnp.inf)
        chosen = jnp.zeros((nbp, blk), jnp.bool_)
        for _ in range(MOBA_TOPK):
            mx = jnp.max(work, axis=0, keepdims=True)
            first = jnp.min(jnp.where(work == mx, b_iota, nbp), axis=0, keepdims=True)
            hit = (b_iota == first) & (mx > -jnp.inf)
            chosen = chosen | hit
            work = jnp.where(hit, -jnp.inf, work)
        pen = jnp.where(chosen, 0.0, -MASK_BIG)
        qa_scr[h] = jnp.concatenate(
            [qt[0:dh], pen, jnp.zeros((LANES - dh - nbp, blk), F32)], axis=0).astype(BF16)
        s = _dot(ka_ref[h, pl.ds(r0, blk), :], qt.astype(BF16))
        s = jnp.where(key_i <= qry_i, s, -jnp.inf)
        m0 = jnp.max(s, axis=0, keepdims=True)
        m_scr[h] = m0
        acc_scr[h] = _dot(vt_ref[h, i], jnp.exp(s - m0).astype(BF16))

    def past_blocks(js):
        scores = [[_dot(ka_ref[h, pl.ds(pl.multiple_of(j * blk, blk), blk), :], qa_scr[h]) for j in js]
                  for h in range(nh)]
        probs, alphas = [], []
        for h in range(nh):
            m_old = m_scr[h]
            m_new = m_old
            for s in scores[h]:
                m_new = jnp.maximum(m_new, jnp.max(s, axis=0, keepdims=True))
            probs.append([jnp.exp(s - m_new).astype(BF16) for s in scores[h]])
            alphas.append(jnp.exp(m_old - m_new))
            m_scr[h] = m_new
        for h in range(nh):
            acc = alphas[h] * acc_scr[h]
            for j, p in zip(js, probs[h]):
                acc = acc + _dot(vt_ref[h, j], p)
            acc_scr[h] = acc

    group = 4

    def past_group(jj, carry):
        past_blocks(tuple(group * jj + k for k in range(group)))
        return carry

    lax.fori_loop(0, i // group, past_group, 0)
    done = (i // group) * group
    for width in (2, 1):
        has = (i % (2 * width)) >= width

        @pl.when(has)
        def _(done=done, width=width):
            past_blocks(tuple(done + k for k in range(width)))

        done = done + jnp.where(has, width, 0)

    for h in range(nh):
        acc = acc_scr[h]
        o_ref[h * dh:(h + 1) * dh, :] = (acc[0:dh] / acc[dh:dh + 1]).astype(o_ref.dtype)


def _moba_mixer(mo, gq, gk):
    bsz, seq, d3 = mo.shape
    hd = d3 // 3
    dh = gq.shape[-1]
    nh = hd // dh
    blk = MOBA_BLOCK
    nblk = seq // blk
    nbp = -(-nblk // 8) * 8
    vr = dh + MOBA_SUM_ROWS
    assert seq % blk == 0 and dh + nbp <= LANES
    qt, ka, km, vt = pl.pallas_call(
        functools.partial(_moba_prep_kernel, nh=nh, dh=dh),
        grid=(bsz, nblk),
        in_specs=[pl.BlockSpec((None, blk, d3), lambda b, i: (b, i, 0)),
                  _const_spec((1, dh)), _const_spec((1, dh))],
        out_specs=[pl.BlockSpec((None, nh, LANES, blk), lambda b, i: (b, 0, 0, i)),
                   pl.BlockSpec((None, nh, blk, LANES), lambda b, i: (b, 0, i, 0)),
                   pl.BlockSpec((None, nh, nbp, LANES), lambda b, i: (b, 0, 0, 0)),
                   pl.BlockSpec((None, nh, None, vr, blk), lambda b, i: (b, 0, i, 0, 0))],
        out_shape=(jax.ShapeDtypeStruct((bsz, nh, LANES, seq), F32),
                   jax.ShapeDtypeStruct((bsz, nh, seq, LANES), BF16),
                   jax.ShapeDtypeStruct((bsz, nh, nbp, LANES), F32),
                   jax.ShapeDtypeStruct((bsz, nh, nblk, vr, blk), BF16)),
        compiler_params=_params("arbitrary", "arbitrary"),
        name="moba_prep",
    )(mo, gq.reshape(1, dh), gk.reshape(1, dh))
    return pl.pallas_call(
        functools.partial(_moba_kernel, nh=nh, dh=dh, nbp=nbp),
        grid=(bsz, nblk),
        in_specs=[pl.BlockSpec((None, nh, LANES, blk), lambda b, i: (b, 0, 0, i)),
                  pl.BlockSpec((None, nh, seq, LANES), lambda b, i: (b, 0, 0, 0)),
                  pl.BlockSpec((None, nh, nbp, LANES), lambda b, i: (b, 0, 0, 0)),
                  pl.BlockSpec((None, nh, nblk, vr, blk), lambda b, i: (b, 0, 0, 0, 0))],
        out_specs=pl.BlockSpec((None, hd, blk), lambda b, i: (b, 0, i)),
        out_shape=jax.ShapeDtypeStruct((bsz, hd, seq), BF16),
        scratch_shapes=[pltpu.VMEM((nh, LANES, blk), BF16),
                        pltpu.VMEM((nh, 1, blk), F32),
                        pltpu.VMEM((nh, vr, blk), F32)],
        compiler_params=_params("arbitrary", "arbitrary"),
        name="moba_attn",
    )(qt, ka, km, vt)


def _merge_kernel(x_ref, ya_ref, hb_ref, mc_ref, gl_ref, mod_ref, g2_ref, who_ref, wmo_ref, wout_ref,
                  rwh_ref, rwl_ref, rb_ref,
                  x1_ref, h2_ref, idx_ref, wgt_ref, pos_ref, cnt_ref, *, tm, ne):
    first = (pl.program_id(0) == 0) & (pl.program_id(1) == 0)

    @pl.when(first)
    def _():
        cnt_ref[...] = jnp.zeros_like(cnt_ref)

    d = x_ref.shape[-1]
    yb = _dot(hb_ref[...], who_ref[...])
    yc = _tn_dot(mc_ref[...], wmo_ref[...])
    merged = (_sigmoid(gl_ref[:, 0:d].astype(F32)) * ya_ref[...].astype(F32)
              + _sigmoid(gl_ref[:, d:2 * d].astype(F32)) * yb
              + _sigmoid(gl_ref[:, 2 * d:3 * d].astype(F32)) * yc)
    x1 = x_ref[...] + mod_ref[2:3, :] * _dot(merged.astype(BF16), wout_ref[...])
    x1_ref[...] = x1
    ms = jnp.mean(x1 * x1, axis=-1, keepdims=True)
    h2 = x1 * lax.rsqrt(ms + EPS) * g2_ref[...]
    h2 = h2 * (1.0 + mod_ref[4:5, :]) + mod_ref[3:4, :]
    h2_ref[...] = _pack_bf16_pairs(h2)
    hi = h2.astype(BF16)
    lo = (h2 - hi.astype(F32)).astype(BF16)
    logits = _nt_dot(rwh_ref[...], hi) + _nt_dot(rwh_ref[...], lo) + _nt_dot(rwl_ref[...], hi)
    scores = _sigmoid(logits)
    e_iota = lax.broadcasted_iota(I32, (ne, tm), 0)
    work = scores + rb_ref[...]
    picked = jnp.zeros((ne, tm), F32)
    hits, tops = [], []
    for _ in range(TOP_K):
        mx = jnp.max(work, axis=0, keepdims=True)
        first_e = jnp.min(jnp.where(work == mx, e_iota, ne), axis=0, keepdims=True)
        hit = e_iota == first_e
        hits.append((hit, first_e))
        tops.append(jnp.sum(jnp.where(hit, scores, 0.0), axis=0, keepdims=True))
        work = jnp.where(hit, -jnp.inf, work)
        picked = jnp.where(hit, 1.0, picked)
    total = functools.reduce(lambda a, b: a + b, tops)
    upper = (lax.broadcasted_iota(I32, (tm, tm), 0) < lax.broadcasted_iota(I32, (tm, tm), 1)).astype(BF16)
    rank = _dot(picked.astype(BF16), upper) + cnt_ref[:, 0:1]
    zero_row = jnp.zeros((1, tm), F32)
    idx_rows, w_rows, pos_rows = [], [], []
    for (hit, first_e), top in zip(hits, tops):
        idx_rows.append(first_e)
        w_rows.append(top / total * ROUTED_SCALE)
        pos_rows.append(jnp.sum(jnp.where(hit, rank, 0.0), axis=0, keepdims=True))
    pad = 8 - TOP_K
    idx_ref[...] = jnp.concatenate(idx_rows + [jnp.zeros((1, tm), I32)] * pad, axis=0)
    wgt_ref[...] = jnp.concatenate(w_rows + [zero_row] * pad, axis=0)
    pos_ref[...] = jnp.concatenate(pos_rows + [zero_row] * pad, axis=0).astype(I32)
    cnt_ref[...] = cnt_ref[...] + jnp.sum(picked, axis=1, keepdims=True)


def _merge_router(x, ya, hb, mc, gl, mod, norm2_g, who, wmo, wout, rwh, rwl, rbias, tm):
    bsz, seq, d = x.shape
    ne = rwh.shape[0]
    nt = seq // tm
    n_tok = bsz * seq
    row = lambda n: pl.BlockSpec((None, tm, n), lambda b, i: (b, i, 0))
    col = pl.BlockSpec((8, tm), lambda b, i: (0, b * nt + i))
    return pl.pallas_call(
        functools.partial(_merge_kernel, tm=tm, ne=ne),
        grid=(bsz, nt),
        in_specs=[row(d), row(d), row(hb.shape[-1]),
                  pl.BlockSpec((None, mc.shape[1], tm), lambda b, i: (b, 0, i)), row(gl.shape[-1]),
                  pl.BlockSpec((None, N_MOD, d), lambda b, i: (b, 0, 0)),
                  _const_spec((1, d)), _const_spec(who.shape), _const_spec(wmo.shape),
                  _const_spec(wout.shape), _const_spec(rwh.shape), _const_spec(rwl.shape),
                  _const_spec((ne, 1))],
        out_specs=[row(d), row(d // 2), col, col, col,
                   pl.BlockSpec((ne, LANES), lambda b, i: (0, 0))],
        out_shape=(jax.ShapeDtypeStruct((bsz, seq, d), F32),
                   jax.ShapeDtypeStruct((bsz, seq, d // 2), jnp.uint32),
                   jax.ShapeDtypeStruct((8, n_tok), I32),
                   jax.ShapeDtypeStruct((8, n_tok), F32),
                   jax.ShapeDtypeStruct((8, n_tok), I32),
                   jax.ShapeDtypeStruct((ne, LANES), F32)),
        compiler_params=_params("arbitrary", "arbitrary"),
        name="merge_router",
    )(x, ya, hb, mc, gl, mod, norm2_g.reshape(1, d), who, wmo, wout, rwh, rwl, rbias.reshape(ne, 1))


def _expert_kernel(be_ref, nv_ref, x_ref, wg_ref, wu_ref, wd_ref, y_ref, *, rb):
    nv = nv_ref[pl.program_id(0)]

    @pl.when(nv > 0)
    def _():
        rows = lax.broadcasted_iota(I32, (rb, 1), 0)
        x = jnp.where(rows < nv, _unpack_bf16_pairs(x_ref[...]), 0.0).astype(BF16)
        g = _dot(x, wg_ref[...])
        u = _dot(x, wu_ref[...])
        y_ref[...] = _pack_bf16_pairs(_dot((_silu(g) * u).astype(BF16), wd_ref[...]))


def _experts(block_expert, block_rows, xs, wg, wu, wd, rb):
    n_rows = xs.shape[0]
    n_blocks = n_rows // rb
    _, d, de = wg.shape
    live = lambda i, be, nv: jnp.where(nv[i] > 0, i, n_blocks - 1)
    grid_spec = pltpu.PrefetchScalarGridSpec(
        num_scalar_prefetch=2,
        grid=(n_blocks,),
        in_specs=[pl.BlockSpec((rb, d // 2), lambda i, be, nv: (live(i, be, nv), 0)),
                  pl.BlockSpec((None, d, de), lambda i, be, nv: (be[i], 0, 0)),
                  pl.BlockSpec((None, d, de), lambda i, be, nv: (be[i], 0, 0)),
                  pl.BlockSpec((None, de, d), lambda i, be, nv: (be[i], 0, 0))],
        out_specs=pl.BlockSpec((rb, d // 2), lambda i, be, nv: (live(i, be, nv), 0)),
    )
    return pl.pallas_call(
        functools.partial(_expert_kernel, rb=rb),
        grid_spec=grid_spec,
        out_shape=jax.ShapeDtypeStruct((n_rows, d // 2), jnp.uint32),
        compiler_params=_params("arbitrary"),
        name="moe_experts",
    )(block_expert, block_rows, xs, wg, wu, wd)


def _sc_gather_rows(table, idx):
    n_idx = idx.shape[0]
    width = table.shape[1]
    n_workers = SC_CORES * SC_SUBCORES
    per_worker = n_idx // n_workers
    assert n_idx % (n_workers * SC_GATHER_CHUNK) == 0
    mesh = plsc.VectorSubcoreMesh(core_axis_name="c", subcore_axis_name="s")

    def body(table_hbm, idx_hbm, out_hbm, idx_v, rows_v, sem):
        wid = lax.axis_index("s") * SC_CORES + lax.axis_index("c")

        @pl.loop(0, per_worker // SC_GATHER_CHUNK)
        def _(j):
            base = wid * per_worker + j * SC_GATHER_CHUNK
            pltpu.sync_copy(idx_hbm.at[pl.ds(base, SC_GATHER_CHUNK)], idx_v)
            pltpu.async_copy(table_hbm.at[idx_v], rows_v, sem).wait()
            pltpu.sync_copy(rows_v, out_hbm.at[pl.ds(base, SC_GATHER_CHUNK)])

    return pl.kernel(
        body, mesh=mesh,
        out_type=jax.ShapeDtypeStruct((n_idx, width), table.dtype),
        scratch_types=[pltpu.VMEM((SC_GATHER_CHUNK,), I32),
                       pltpu.VMEM((SC_GATHER_CHUNK, width), table.dtype),
                       pltpu.SemaphoreType.DMA],
        name="moe_gather_sc",
    )(table, idx)


def _sc_scatter_rows(rows, idx, n_out, copies):
    n, width = rows.shape
    n_workers = SC_CORES * SC_SUBCORES
    per_worker = n // n_workers
    assert n % (n_workers * SC_GATHER_CHUNK) == 0
    mesh = plsc.VectorSubcoreMesh(core_axis_name="c", subcore_axis_name="s")

    def body(rows_hbm, idx_hbm, out_hbm, idx_v, rows_v):
        wid = lax.axis_index("s") * SC_CORES + lax.axis_index("c")

        @pl.loop(0, per_worker // SC_GATHER_CHUNK)
        def _(j):
            base = wid * per_worker + j * SC_GATHER_CHUNK
            pltpu.sync_copy(rows_hbm.at[pl.ds(base, SC_GATHER_CHUNK)], rows_v)
            for k in range(copies):
                pltpu.sync_copy(idx_hbm.at[pl.ds(k * n + base, SC_GATHER_CHUNK)], idx_v)
                pltpu.sync_copy(rows_v, out_hbm.at[idx_v])

    return pl.kernel(
        body, mesh=mesh,
        out_type=jax.ShapeDtypeStruct((n_out, width), rows.dtype),
        scratch_types=[pltpu.VMEM((SC_GATHER_CHUNK,), I32),
                       pltpu.VMEM((SC_GATHER_CHUNK, width), rows.dtype)],
        name="moe_dispatch_sc",
    )(rows, idx)


def _combine_kernel(wgt_ref, x1_ref, h2_ref, mod_ref, sg_ref, su_ref, sd_ref, g_ref, o_ref):
    hb = _unpack_bf16_pairs(h2_ref[...]).astype(BF16)
    hid = _silu(_dot(hb, sg_ref[...])) * _dot(hb, su_ref[...])
    y = _dot(hid.astype(BF16), sd_ref[...])
    wcol = wgt_ref[...].T
    for k in range(TOP_K):
        y = y + _unpack_bf16_pairs(g_ref[k]) * wcol[:, k:k + 1]
    o_ref[...] = x1_ref[...] + mod_ref[5:6, :] * y


def _combine(wgt, x1, h2, mod, sg, su, sd, gathered, tm):
    bsz, seq, d = x1.shape
    nt = seq // tm
    row = pl.BlockSpec((None, tm, d), lambda b, i: (b, i, 0))
    return pl.pallas_call(
        _combine_kernel,
        grid=(bsz, nt),
        in_specs=[pl.BlockSpec((8, tm), lambda b, i: (0, b * nt + i)),
                  row, pl.BlockSpec((None, tm, d // 2), lambda b, i: (b, i, 0)),
                  pl.BlockSpec((None, N_MOD, d), lambda b, i: (b, 0, 0)),
                  _const_spec(sg.shape), _const_spec(su.shape), _const_spec(sd.shape),
                  pl.BlockSpec((TOP_K, tm, d // 2), lambda b, i: (0, b * nt + i, 0))],
        out_specs=row,
        out_shape=jax.ShapeDtypeStruct((bsz, seq, d), F32),
        compiler_params=_params("arbitrary", "arbitrary"),
        name="moe_combine",
    )(wgt, x1, h2, mod, sg, su, sd, gathered)


def _dest_kernel(ps_ref, idx_ref, pos_ref, o_ref, *, ne):
    idx = idx_ref[...]
    acc = pos_ref[...]
    for e in range(ne):
        acc = acc + jnp.where(idx == e, ps_ref[e], 0)
    o_ref[...] = acc


def _routing_tables(idx, pos, counts, n_tok, ne, rb):
    n_assign = n_tok * TOP_K
    n_blocks = -(-n_assign // rb) + ne
    padded = (counts + rb - 1) // rb * rb
    pend = jnp.cumsum(padded)
    pstart = (pend - padded).astype(I32)
    tt = min(8192, n_tok)
    col = pl.BlockSpec((8, tt), lambda i: (0, i))
    dest = pl.pallas_call(
        functools.partial(_dest_kernel, ne=ne),
        grid=(n_tok // tt,),
        in_specs=[pl.BlockSpec(memory_space=pltpu.SMEM), col, col],
        out_specs=col,
        out_shape=jax.ShapeDtypeStruct((8, n_tok), I32),
        compiler_params=_params("arbitrary"),
        name="moe_dest",
    )(pstart, idx, pos)
    bstart = jnp.arange(n_blocks, dtype=I32) * rb
    bexp = jnp.minimum(jnp.sum((pend[None, :] <= bstart[:, None]).astype(I32), axis=1), ne - 1)
    brows = jnp.clip(counts[bexp] - (bstart - pstart[bexp]), 0, rb).astype(I32)
    return dest, bexp, brows, n_blocks


def _block_diag_in(bbt, g):
    gw, n = bbt.shape
    p = n // g
    rows = jnp.tile(bbt, (g, 1))
    rg = jnp.arange(g * gw)[:, None] // gw
    cg = jnp.arange(n)[None, :] // p
    return jnp.where(rg == cg, rows, 0.0)


def _block_diag_out(c, sign):
    g, gw, p = c.shape
    cols = jnp.tile(c.transpose(0, 2, 1).reshape(g * p, gw), (1, g))
    rg = jnp.arange(g * p)[:, None] // p
    cg = jnp.arange(g * gw)[None, :] // gw
    return jnp.where(rg == cg, sign * cols, 0.0)


def kernel(x, c, ada_w, ada_b, norm1_g, w_in, ssm_log_dt, ssm_lambda_re, ssm_lambda_im, ssm_b_re,
           ssm_b_im, ssm_c_re, ssm_c_im, ssm_d, ssm_w_glu, hgrn_lb_logits, hgrn_norm_g, hgrn_w_o,
           moba_q_norm_g, moba_k_norm_g, moba_w_o, w_out, norm2_g, router_w, router_bias,
           exp_w_gate, exp_w_up, exp_w_down, shared_w_gate, shared_w_up, shared_w_down):
    bsz, seq, d = x.shape
    depth = w_in.shape[0]
    n_tok = bsz * seq
    g = ssm_lambda_re.shape[1]
    d_ssm = ssm_d.shape[-1]
    d_hk = hgrn_lb_logits.shape[-1]
    d_moba = moba_w_o.shape[1]
    ne = router_w.shape[-1]
    widths = (d_ssm, 4 * d_hk, 3 * d_moba, 3 * d)
    tm = min(256, seq)
    rb = 512

    mod_all = _ada_mod(c, ada_w, ada_b).reshape(depth, bsz, N_MOD, d)
    abar, bbr, bbi, lower = _prep_params(ssm_log_dt, ssm_lambda_re, ssm_lambda_im, ssm_b_re, ssm_b_im,
                                         hgrn_lb_logits)

    for l in range(depth):
        mod = mod_all[l]
        u, hg, mo, gl = _inproj(x, mod, norm1_g[l], w_in[l].astype(BF16), widths, min(512, seq))
        wb = jnp.concatenate([_block_diag_in(bbr[l], g), _block_diag_in(bbi[l], g)], axis=1).astype(BF16)
        wc = jnp.concatenate([_block_diag_out(ssm_c_re[l], 1.0), _block_diag_out(ssm_c_im[l], -1.0)],
                             axis=0).astype(BF16)
        ya = _s5_mixer(u, wb, wc, abar[l], ssm_d[l], ssm_w_glu[l].astype(BF16), min(128, seq))
        hb = _hgrn_mixer(hg, lower[l], hgrn_norm_g[l], min(256, seq))
        mc = _moba_mixer(mo, moba_q_norm_g[l], moba_k_norm_g[l])
        rwt = router_w[l].T
        rwh = rwt.astype(BF16)
        rwl = (rwt - rwh.astype(F32)).astype(BF16)
        x1, h2, idx, wgt, pos, cnt = _merge_router(
            x, ya, hb, mc, gl, mod, norm2_g[l], hgrn_w_o[l].astype(BF16), moba_w_o[l].astype(BF16),
            w_out[l].astype(BF16), rwh, rwl, router_bias[l], tm)
        counts = cnt[:, 0].astype(I32)
        dest, bexp, brows, n_blocks = _routing_tables(idx, pos, counts, n_tok, ne, rb)
        dest_flat = dest[:TOP_K].reshape(-1)
        xs = _sc_scatter_rows(h2.reshape(n_tok, d // 2), dest_flat, n_blocks * rb, TOP_K)
        ys = _experts(bexp, brows, xs, exp_w_gate[l].astype(BF16), exp_w_up[l].astype(BF16),
                      exp_w_down[l].astype(BF16), rb)
        gathered = _sc_gather_rows(ys, dest_flat).reshape(TOP_K, n_tok, d // 2)
        x = _combine(wgt, x1, h2, mod, shared_w_gate[l].astype(BF16),
                     shared_w_up[l].astype(BF16), shared_w_down[l].astype(BF16), gathered, tm)
    return x
```

```python
import functools

import jax
import jax.numpy as jnp
from jax import lax
from jax.experimental import pallas as pl
from jax.experimental.pallas import tpu as pltpu
from jax.experimental.pallas import tpu_sc as plsc

F32 = jnp.float32
BF16 = jnp.bfloat16
I32 = jnp.int32
HIGHEST = lax.Precision.HIGHEST

N_MOD = 6
EPS = 1e-6
MOBA_BLOCK = 256
MOBA_TOPK = 3
TOP_K = 6
ROUTED_SCALE = 2.5

LANES = 128
SC_CORES = 2
SC_SUBCORES = 16
SC_PIPE_CHUNK = 64
SC_GATHER_CHUNK = 128
HGRN_CHUNK = 64
HGRN_SUB = 16
S5_ROW_PAD = 8
MOBA_SUM_ROWS = 16
MASK_BIG = 30000.0
VMEM_LIMIT = 56 * 1024 * 1024


def _nt_dot(a, b, precision=None):
    return lax.dot_general(a, b, (((1,), (1,)), ((), ())), preferred_element_type=F32,
                           precision=precision)


def _tn_dot(a, b, precision=None):
    return lax.dot_general(a, b, (((0,), (0,)), ((), ())), preferred_element_type=F32,
                           precision=precision)


def _dot(a, b, precision=None):
    return jnp.dot(a, b, preferred_element_type=F32, precision=precision)


def _bf16_terms(x, n):
    terms = []
    for _ in range(n):
        t = x.astype(BF16)
        terms.append(t)
        x = x - t.astype(F32)
    return terms


def _dot_exact_lhs(a, b, n):
    ab = a.astype(BF16)
    return functools.reduce(lambda p, q: p + q, [_dot(ab, t) for t in _bf16_terms(b, n)])


def _dot_exact_rhs(a, b, n):
    bb = b.astype(BF16)
    return functools.reduce(lambda p, q: p + q, [_dot(t, bb) for t in _bf16_terms(a, n)])


def _pack_bf16_pairs(y):
    m = y.shape[1] // 2
    lo = lax.bitcast_convert_type(y[:, :m].astype(BF16).astype(F32), jnp.uint32)
    hi = lax.bitcast_convert_type(y[:, m:].astype(BF16).astype(F32), jnp.uint32)
    return hi | (lo >> 16)


def _unpack_bf16_pairs(w):
    lo = lax.bitcast_convert_type(w << 16, F32)
    hi = lax.bitcast_convert_type(w & jnp.uint32(0xFFFF0000), F32)
    return jnp.concatenate([lo, hi], axis=1)


def _sigmoid(x):
    return 1.0 / (1.0 + jnp.exp(-x))


def _silu(x):
    return x * _sigmoid(x)


def _params(*sem):
    return pltpu.CompilerParams(dimension_semantics=sem, vmem_limit_bytes=VMEM_LIMIT)


def _const_spec(shape):
    nd = len(shape)
    return pl.BlockSpec(shape, lambda *_: (0,) * nd, pipeline_mode=pl.Buffered(1))


def _ada_kernel(c_ref, w_ref, b_ref, o_ref):
    c = c_ref[...]
    o_ref[...] = _dot(_silu(c), w_ref[...], HIGHEST) + b_ref[...]


def _ada_mod(c, ada_w, ada_b):
    depth, d, n = ada_w.shape
    bsz = c.shape[0]
    tn = n // 4
    return pl.pallas_call(
        _ada_kernel,
        grid=(depth, n // tn),
        in_specs=[pl.BlockSpec((bsz, d), lambda l, j: (0, 0)),
                  pl.BlockSpec((None, d, tn), lambda l, j: (l, 0, j)),
                  pl.BlockSpec((None, 1, tn), lambda l, j: (l, 0, j))],
        out_specs=pl.BlockSpec((None, bsz, tn), lambda l, j: (l, 0, j)),
        out_shape=jax.ShapeDtypeStruct((depth, bsz, n), F32),
        compiler_params=_params("arbitrary", "arbitrary"),
        name="ada_mod",
    )(c, ada_w, ada_b.reshape(depth, 1, n))


def _prep_kernel(ldt_ref, lr_ref, li_ref, btr_ref, bti_ref, lbl_ref,
                 abar_ref, bbr_ref, bbi_ref, lb_ref):
    depth = ldt_ref.shape[0]
    for l in range(depth):
        dt = jnp.exp(ldt_ref[l])
        lr = lr_ref[l]
        li = li_ref[l]
        mag = jnp.exp(lr * dt)
        ar = mag * jnp.cos(li * dt)
        ai = mag * jnp.sin(li * dt)
        inv = 1.0 / (lr * lr + li * li)
        cr = ((ar - 1.0) * lr + ai * li) * inv
        ci = (ai * lr - (ar - 1.0) * li) * inv
        abar_ref[l, 0:1, :] = ar
        abar_ref[l, 1:2, :] = ai
        btr = btr_ref[l]
        bti = bti_ref[l]
        bbr_ref[l] = cr * btr - ci * bti
        bbi_ref[l] = cr * bti + ci * btr
    rows = [lbl_ref[l:l + 1, :] for l in range(depth)]
    mx = functools.reduce(jnp.maximum, rows)
    ex = [jnp.exp(r - mx) for r in rows]
    den = functools.reduce(lambda a, b: a + b, ex)
    run = ex[0] / den
    first = run
    lb_ref[0:1, :] = run - first
    for l in range(1, depth):
        run = run + ex[l] / den
        lb_ref[l:l + 1, :] = run - first


def _prep_params(log_dt, lam_re, lam_im, b_re, b_im, lb_logits):
    depth, g, p = lam_re.shape
    gw = b_re.shape[-1]
    n = g * p
    hk = lb_logits.shape[-1]
    ldt = jnp.repeat(log_dt, p, axis=1).reshape(depth, 1, n)
    btr = b_re.transpose(0, 3, 1, 2).reshape(depth, gw, n)
    bti = b_im.transpose(0, 3, 1, 2).reshape(depth, gw, n)
    return pl.pallas_call(
        _prep_kernel,
        out_shape=(jax.ShapeDtypeStruct((depth, 2, n), F32),
                   jax.ShapeDtypeStruct((depth, gw, n), F32),
                   jax.ShapeDtypeStruct((depth, gw, n), F32),
                   jax.ShapeDtypeStruct((depth, hk), F32)),
        name="param_prep",
    )(ldt, lam_re.reshape(depth, 1, n), lam_im.reshape(depth, 1, n), btr, bti, lb_logits)


def _inproj_kernel(x_ref, mod_ref, g_ref, w_ref, u_ref, hg_ref, mo_ref, gl_ref, *, widths):
    x = x_ref[...]
    ms = jnp.mean(x * x, axis=-1, keepdims=True)
    h = x * lax.rsqrt(ms + EPS) * g_ref[...]
    h = h * (1.0 + mod_ref[1:2, :]) + mod_ref[0:1, :]
    hb = h.astype(BF16)
    d_ssm, d_hgrn, d_moba, d_gate = widths
    o0 = 0
    u_ref[...] = _dot(hb, w_ref[:, o0:o0 + d_ssm])
    o0 += d_ssm
    hg_ref[...] = _dot(hb, w_ref[:, o0:o0 + d_hgrn])
    o0 += d_hgrn
    mo_ref[...] = _dot(hb, w_ref[:, o0:o0 + d_moba])
    o0 += d_moba
    step = 1024
    for j in range(0, d_gate, step):
        gl_ref[:, j:j + step] = _dot(hb, w_ref[:, o0 + j:o0 + j + step]).astype(BF16)


def _inproj(x, mod, norm_g, w_in_bf, widths, tm):
    bsz, seq, d = x.shape
    d_ssm, d_hgrn, d_moba, d_gate = widths
    d_in = w_in_bf.shape[1]
    row = lambda n: pl.BlockSpec((None, tm, n), lambda b, i: (b, i, 0))
    return pl.pallas_call(
        functools.partial(_inproj_kernel, widths=widths),
        grid=(bsz, seq // tm),
        in_specs=[row(d),
                  pl.BlockSpec((None, N_MOD, d), lambda b, i: (b, 0, 0)),
                  _const_spec((1, d)),
                  _const_spec((d, d_in))],
        out_specs=[row(d_ssm), row(d_hgrn), row(d_moba), row(d_gate)],
        out_shape=(jax.ShapeDtypeStruct((bsz, seq, d_ssm), F32),
                   jax.ShapeDtypeStruct((bsz, seq, d_hgrn), F32),
                   jax.ShapeDtypeStruct((bsz, seq, d_moba), F32),
                   jax.ShapeDtypeStruct((bsz, seq, d_gate), BF16)),
        compiler_params=_params("arbitrary", "arbitrary"),
        name="in_proj",
    )(x, mod, norm_g.reshape(1, d), w_in_bf)


def _s5_kernel(u_ref, wb_ref, wc_ref, abar_ref, d_ref, wglu_ref, o_ref, bu_ref, st_ref,
               *, nb, tl, ns, unroll):
    nc = ns // LANES
    rs = tl + S5_ROW_PAD

    @pl.when(pl.program_id(0) == 0)
    def _():
        st_ref[...] = jnp.zeros_like(st_ref)

    for b in range(nb):
        bu = _dot(u_ref[b].astype(BF16), wb_ref[...])
        for c in range(2 * nc):
            bu_ref[c, b * rs:b * rs + tl, :] = bu[:, c * LANES:(c + 1) * LANES]

    ar = [jnp.broadcast_to(abar_ref[0:1, c * LANES:(c + 1) * LANES], (nb, LANES)) for c in range(nc)]
    ai = [jnp.broadcast_to(abar_ref[1:2, c * LANES:(c + 1) * LANES], (nb, LANES)) for c in range(nc)]

    def outer(j, carry):
        sr, si = list(carry[:nc]), list(carry[nc:])
        for k in range(unroll):
            t = j * unroll + k
            for c in range(nc):
                br = bu_ref[c, pl.ds(t, nb, stride=rs), :]
                bi = bu_ref[nc + c, pl.ds(t, nb, stride=rs), :]
                nr = ar[c] * sr[c] - ai[c] * si[c] + br
                ni = ar[c] * si[c] + ai[c] * sr[c] + bi
                bu_ref[c, pl.ds(t, nb, stride=rs), :] = nr
                bu_ref[nc + c, pl.ds(t, nb, stride=rs), :] = ni
                sr[c], si[c] = nr, ni
        return tuple(sr) + tuple(si)

    init = tuple(st_ref[c] for c in range(2 * nc))
    fin = lax.fori_loop(0, tl // unroll, outer, init)
    for c in range(2 * nc):
        st_ref[c] = fin[c]

    dm = o_ref.shape[-1]
    for b in range(nb):
        s = jnp.concatenate([bu_ref[c, b * rs:b * rs + tl, :] for c in range(2 * nc)], axis=1)
        y = _dot(s.astype(BF16), wc_ref[...]) + d_ref[...] * u_ref[b]
        z = _dot(jax.nn.gelu(y).astype(BF16), wglu_ref[...])
        o_ref[b] = (z[:, :dm] * _sigmoid(z[:, dm:])).astype(o_ref.dtype)


def _s5_mixer(u, wb, wc, abar, d_skip, wglu, tl):
    bsz, seq, d_ssm = u.shape
    ns = abar.shape[-1]
    dm = wglu.shape[1] // 2
    return pl.pallas_call(
        functools.partial(_s5_kernel, nb=bsz, tl=tl, ns=ns, unroll=8),
        grid=(seq // tl,),
        in_specs=[pl.BlockSpec((bsz, tl, d_ssm), lambda i: (0, i, 0)),
                  _const_spec(wb.shape), _const_spec(wc.shape), _const_spec(abar.shape),
                  _const_spec((1, d_ssm)), _const_spec(wglu.shape)],
        out_specs=pl.BlockSpec((bsz, tl, dm), lambda i: (0, i, 0)),
        out_shape=jax.ShapeDtypeStruct((bsz, seq, dm), BF16),
        scratch_shapes=[pltpu.VMEM((2 * ns // LANES, bsz * (tl + S5_ROW_PAD), LANES), F32),
                        pltpu.VMEM((2 * ns // LANES, bsz, LANES), F32)],
        compiler_params=_params("arbitrary"),
        name="s5_mixer",
    )(u, wb, wc, abar, d_skip.reshape(1, d_ssm), wglu)


def _hgrn_kernel(x_ref, lb_ref, ng_ref, o_ref, st_ref, *, tl, hd, dk):
    cs, sb = HGRN_CHUNK, HGRN_SUB

    @pl.when(pl.program_id(1) == 0)
    def _():
        st_ref[...] = jnp.zeros_like(st_ref)

    def iota(shape, dim):
        return lax.broadcasted_iota(I32, shape, dim)

    tril = (iota((cs, cs), 0) >= iota((cs, cs), 1)).astype(F32)
    head_eq = (iota((hd, hd), 0) // dk) == (iota((hd, hd), 1) // dk)
    block_ones = head_eq.astype(BF16)
    block_mean = head_eq.astype(F32) * (1.0 / dk)
    sel = ((iota((sb, sb * sb), 1) // sb) == iota((sb, sb * sb), 0)).astype(BF16)
    s_iota = iota((sb, hd), 0)
    lane_head = iota((1, hd), 1) // dk
    n_heads = hd // dk
    lb = lb_ref[...]
    ng = ng_ref[...]

    def head_copies(a):
        return jnp.concatenate([jnp.where(lane_head == h, a, 0.0) for h in range(n_heads)], axis=0)

    def off_diag_scores(q, key, cum, t0, t1, s0, s1):
        r = cum[s1 - 1:s1, :]
        qt = q[t0:t1] * jnp.exp(cum[t0:t1] - r)
        ks = key[s0:s1] * jnp.exp(r - cum[s0:s1])
        return _nt_dot(qt.astype(BF16), head_copies(ks).astype(BF16))

    nch = tl // cs
    half = cs // 2
    spans = ((half, cs, 0, half), (sb, half, 0, sb), (half + sb, cs, half, half + sb))
    pre = []
    for c in range(nch):
        r0 = c * cs
        q = x_ref[r0:r0 + cs, 0:hd]
        fr = x_ref[r0:r0 + cs, hd:2 * hd]
        iv = x_ref[r0:r0 + cs, 2 * hd:3 * hd]
        forget = lb + (1.0 - lb) * _sigmoid(fr)
        key = 1.0 - forget
        cum = _dot_exact_lhs(tril, jnp.log(forget), 3)
        pre.append((q, key, iv, cum))
    heads_sum = []
    for q, key, iv, cum in pre:
        for a in range(cs // sb):
            lo = a * sb
            cum_a, qa, ka = cum[lo:lo + sb], q[lo:lo + sb], key[lo:lo + sb]
            slabs = []
            for t in range(sb):
                arg = jnp.where(s_iota <= t, cum_a[t:t + 1] - cum_a, -jnp.inf)
                slabs.append(jnp.exp(arg) * (qa[t:t + 1] * ka))
            heads_sum.append(_dot(jnp.concatenate(slabs, axis=0).astype(BF16), block_ones))
    off_scores = [[off_diag_scores(q, key, cum, *sp) for sp in spans] for q, key, iv, cum in pre]
    intras = []
    for c, (q, key, iv, cum) in enumerate(pre):
        parts = []
        for a in range(cs // sb):
            lo = a * sb
            w = heads_sum[c * (cs // sb) + a] * jnp.concatenate([iv[lo:lo + sb]] * sb, axis=0)
            parts.append(_dot(sel, w.astype(BF16)))
        lvl1, lvl2a, lvl2b = [
            _dot(sc.astype(BF16), head_copies(iv[sp[2]:sp[3]]).astype(BF16))
            for sc, sp in zip(off_scores[c], spans)]
        zeros = jnp.zeros((sb, hd), F32)
        intras.append(jnp.concatenate(parts, axis=0)
                      + jnp.concatenate([zeros, lvl2a, lvl1[0:sb], lvl1[sb:half] + lvl2b], axis=0))
    upds, decays, qcs = [], [], []
    for q, key, iv, cum in pre:
        last = cum[cs - 1:cs, :]
        kc = key * jnp.exp(last - cum)
        upds.append(jnp.where(head_eq, _tn_dot(iv.astype(BF16), kc.astype(BF16)), 0.0))
        decays.append(jnp.exp(last))
        qcs.append((q * jnp.exp(cum)).astype(BF16))
    st = st_ref[...]
    for c in range(nch):
        out = intras[c] + _nt_dot(qcs[c], st.astype(BF16))
        st = st * decays[c] + upds[c]
        ms = _dot_exact_rhs(out * out, block_mean, 2)
        g = x_ref[c * cs:(c + 1) * cs, 3 * hd:4 * hd]
        o_ref[c * cs:(c + 1) * cs, :] = (out * lax.rsqrt(ms + EPS) * ng * _silu(g)).astype(o_ref.dtype)
    st_ref[...] = st


def _hgrn_mixer(hg, lb, norm_g, tl):
    bsz, seq, d4 = hg.shape
    hd = d4 // 4
    dk = norm_g.shape[-1]
    ng = jnp.tile(norm_g, hd // dk).reshape(1, hd)
    return pl.pallas_call(
        functools.partial(_hgrn_kernel, tl=tl, hd=hd, dk=dk),
        grid=(bsz, seq // tl),
        in_specs=[pl.BlockSpec((None, tl, d4), lambda b, i: (b, i, 0)),
                  _const_spec((1, hd)), _const_spec((1, hd))],
        out_specs=pl.BlockSpec((None, tl, hd), lambda b, i: (b, i, 0)),
        out_shape=jax.ShapeDtypeStruct((bsz, seq, hd), BF16),
        scratch_shapes=[pltpu.VMEM((hd, hd), F32)],
        compiler_params=_params("arbitrary", "arbitrary"),
        name="hgrn_mixer",
    )(hg, lb.reshape(1, hd), ng)


def _moba_prep_kernel(x_ref, gq_ref, gk_ref, qt_ref, ka_ref, km_ref, vt_ref, *, nh, dh):
    i = pl.program_id(1)
    blk = MOBA_BLOCK
    hd = nh * dh

    @pl.when(i == 0)
    def _():
        km_ref[...] = jnp.zeros_like(km_ref)

    lane = lax.broadcasted_iota(I32, (blk, LANES - dh), 1)
    onehot = (lane == i).astype(F32)
    pad = jnp.zeros((blk, LANES - dh), F32)
    scale = dh ** -0.5
    for h in range(nh):
        qh = x_ref[:, h * dh:(h + 1) * dh]
        kh = x_ref[:, hd + h * dh:hd + (h + 1) * dh]
        vh = x_ref[:, 2 * hd + h * dh:2 * hd + (h + 1) * dh]
        qn = qh * lax.rsqrt(jnp.mean(qh * qh, axis=-1, keepdims=True) + EPS) * gq_ref[...] * scale
        kn = kh * lax.rsqrt(jnp.mean(kh * kh, axis=-1, keepdims=True) + EPS) * gk_ref[...]
        qt_ref[h] = jnp.concatenate([qn, pad], axis=1).T
        ka_ref[h] = jnp.concatenate([kn, onehot], axis=1).astype(BF16)
        kmean = jnp.mean(kn, axis=0, keepdims=True)
        km_ref[h, pl.ds(i, 1), :] = jnp.concatenate([kmean, jnp.zeros((1, LANES - dh), F32)], axis=1)
        vt_ref[h] = jnp.concatenate([vh, pad + 1.0], axis=1).T[0:dh + MOBA_SUM_ROWS].astype(BF16)


def _moba_kernel(qt_ref, ka_ref, km_ref, vt_ref, o_ref, qa_scr, m_scr, acc_scr, *, nh, dh, nbp):
    i = pl.program_id(1)
    blk = MOBA_BLOCK
    r0 = pl.multiple_of(i * blk, blk)
    b_iota = lax.broadcasted_iota(I32, (nbp, blk), 0)
    key_i = lax.broadcasted_iota(I32, (blk, blk), 0)
    qry_i = lax.broadcasted_iota(I32, (blk, blk), 1)

    for h in range(nh):
        qt = qt_ref[h]
        gate = _dot(km_ref[h], qt, HIGHEST)
        work = jnp.where(b_iota < i, gate, -jnp.inf)
        chosen = jnp.zeros((nbp, blk), jnp.bool_)
        for _ in range(MOBA_TOPK):
            mx = jnp.max(work, axis=0, keepdims=True)
            first = jnp.min(jnp.where(work == mx, b_iota, nbp), axis=0, keepdims=True)
            hit = (b_iota == first) & (mx > -jnp.inf)
            chosen = chosen | hit
            work = jnp.where(hit, -jnp.inf, work)
        pen = jnp.where(chosen, 0.0, -MASK_BIG)
        qa_scr[h] = jnp.concatenate(
            [qt[0:dh], pen, jnp.zeros((LANES - dh - nbp, blk), F32)], axis=0).astype(BF16)
        s = _dot(ka_ref[h, pl.ds(r0, blk), :], qt.astype(BF16))
        s = jnp.where(key_i <= qry_i, s, -jnp.inf)
        m0 = jnp.max(s, axis=0, keepdims=True)
        m_scr[h] = m0
        acc_scr[h] = _dot(vt_ref[h, i], jnp.exp(s - m0).astype(BF16))

    def past_blocks(js):
        scores = [[_dot(ka_ref[h, pl.ds(pl.multiple_of(j * blk, blk), blk), :], qa_scr[h]) for j in js]
                  for h in range(nh)]
        probs, alphas = [], []
        for h in range(nh):
            m_old = m_scr[h]
            m_new = m_old
            for s in scores[h]:
                m_new = jnp.maximum(m_new, jnp.max(s, axis=0, keepdims=True))
            probs.append([jnp.exp(s - m_new).astype(BF16) for s in scores[h]])
            alphas.append(jnp.exp(m_old - m_new))
            m_scr[h] = m_new
        for h in range(nh):
            acc = alphas[h] * acc_scr[h]
            for j, p in zip(js, probs[h]):
                acc = acc + _dot(vt_ref[h, j], p)
            acc_scr[h] = acc

    group = 4

    def past_group(jj, carry):
        past_blocks(tuple(group * jj + k for k in range(group)))
        return carry

    lax.fori_loop(0, i // group, past_group, 0)
    done = (i // group) * group
    for width in (2, 1):
        has = (i % (2 * width)) >= width

        @pl.when(has)
        def _(done=done, width=width):
            past_blocks(tuple(done + k for k in range(width)))

        done = done + jnp.where(has, width, 0)

    for h in range(nh):
        acc = acc_scr[h]
        o_ref[h * dh:(h + 1) * dh, :] = (acc[0:dh] / acc[dh:dh + 1]).astype(o_ref.dtype)


def _moba_mixer(mo, gq, gk):
    bsz, seq, d3 = mo.shape
    hd = d3 // 3
    dh = gq.shape[-1]
    nh = hd // dh
    blk = MOBA_BLOCK
    nblk = seq // blk
    nbp = -(-nblk // 8) * 8
    vr = dh + MOBA_SUM_ROWS
    assert seq % blk == 0 and dh + nbp <= LANES
    qt, ka, km, vt = pl.pallas_call(
        functools.partial(_moba_prep_kernel, nh=nh, dh=dh),
        grid=(bsz, nblk),
        in_specs=[pl.BlockSpec((None, blk, d3), lambda b, i: (b, i, 0)),
                  _const_spec((1, dh)), _const_spec((1, dh))],
        out_specs=[pl.BlockSpec((None, nh, LANES, blk), lambda b, i: (b, 0, 0, i)),
                   pl.BlockSpec((None, nh, blk, LANES), lambda b, i: (b, 0, i, 0)),
                   pl.BlockSpec((None, nh, nbp, LANES), lambda b, i: (b, 0, 0, 0)),
                   pl.BlockSpec((None, nh, None, vr, blk), lambda b, i: (b, 0, i, 0, 0))],
        out_shape=(jax.ShapeDtypeStruct((bsz, nh, LANES, seq), F32),
                   jax.ShapeDtypeStruct((bsz, nh, seq, LANES), BF16),
                   jax.ShapeDtypeStruct((bsz, nh, nbp, LANES), F32),
                   jax.ShapeDtypeStruct((bsz, nh, nblk, vr, blk), BF16)),
        compiler_params=_params("arbitrary", "arbitrary"),
        name="moba_prep",
    )(mo, gq.reshape(1, dh), gk.reshape(1, dh))
    return pl.pallas_call(
        functools.partial(_moba_kernel, nh=nh, dh=dh, nbp=nbp),
        grid=(bsz, nblk),
        in_specs=[pl.BlockSpec((None, nh, LANES, blk), lambda b, i: (b, 0, 0, i)),
                  pl.BlockSpec((None, nh, seq, LANES), lambda b, i: (b, 0, 0, 0)),
                  pl.BlockSpec((None, nh, nbp, LANES), lambda b, i: (b, 0, 0, 0)),
                  pl.BlockSpec((None, nh, nblk, vr, blk), lambda b, i: (b, 0, 0, 0, 0))],
        out_specs=pl.BlockSpec((None, hd, blk), lambda b, i: (b, 0, i)),
        out_shape=jax.ShapeDtypeStruct((bsz, hd, seq), BF16),
        scratch_shapes=[pltpu.VMEM((nh, LANES, blk), BF16),
                        pltpu.VMEM((nh, 1, blk), F32),
                        pltpu.VMEM((nh, vr, blk), F32)],
        compiler_params=_params("arbitrary", "arbitrary"),
        name="moba_attn",
    )(qt, ka, km, vt)


def _merge_kernel(x_ref, ya_ref, hb_ref, mc_ref, gl_ref, mod_ref, g2_ref, who_ref, wmo_ref, wout_ref,
                  rwh_ref, rwl_ref, rb_ref,
                  x1_ref, h2_ref, idx_ref, wgt_ref, pos_ref, cnt_ref, *, tm, ne):
    first = (pl.program_id(0) == 0) & (pl.program_id(1) == 0)

    @pl.when(first)
    def _():
        cnt_ref[...] = jnp.zeros_like(cnt_ref)

    d = x_ref.shape[-1]
    yb = _dot(hb_ref[...], who_ref[...])
    yc = _tn_dot(mc_ref[...], wmo_ref[...])
    merged = (_sigmoid(gl_ref[:, 0:d].astype(F32)) * ya_ref[...].astype(F32)
              + _sigmoid(gl_ref[:, d:2 * d].astype(F32)) * yb
              + _sigmoid(gl_ref[:, 2 * d:3 * d].astype(F32)) * yc)
    x1 = x_ref[...] + mod_ref[2:3, :] * _dot(merged.astype(BF16), wout_ref[...])
    x1_ref[...] = x1
    ms = jnp.mean(x1 * x1, axis=-1, keepdims=True)
    h2 = x1 * lax.rsqrt(ms + EPS) * g2_ref[...]
    h2 = h2 * (1.0 + mod_ref[4:5, :]) + mod_ref[3:4, :]
    h2_ref[...] = _pack_bf16_pairs(h2)
    hi = h2.astype(BF16)
    lo = (h2 - hi.astype(F32)).astype(BF16)
    logits = _nt_dot(rwh_ref[...], hi) + _nt_dot(rwh_ref[...], lo) + _nt_dot(rwl_ref[...], hi)
    scores = _sigmoid(logits)
    e_iota = lax.broadcasted_iota(I32, (ne, tm), 0)
    work = scores + rb_ref[...]
    picked = jnp.zeros((ne, tm), F32)
    hits, tops = [], []
    for _ in range(TOP_K):
        mx = jnp.max(work, axis=0, keepdims=True)
        first_e = jnp.min(jnp.where(work == mx, e_iota, ne), axis=0, keepdims=True)
        hit = e_iota == first_e
        hits.append((hit, first_e))
        tops.append(jnp.sum(jnp.where(hit, scores, 0.0), axis=0, keepdims=True))
        work = jnp.where(hit, -jnp.inf, work)
        picked = jnp.where(hit, 1.0, picked)
    total = functools.reduce(lambda a, b: a + b, tops)
    upper = (lax.broadcasted_iota(I32, (tm, tm), 0) < lax.broadcasted_iota(I32, (tm, tm), 1)).astype(BF16)
    rank = _dot(picked.astype(BF16), upper) + cnt_ref[:, 0:1]
    zero_row = jnp.zeros((1, tm), F32)
    idx_rows, w_rows, pos_rows = [], [], []
    for (hit, first_e), top in zip(hits, tops):
        idx_rows.append(first_e)
        w_rows.append(top / total * ROUTED_SCALE)
        pos_rows.append(jnp.sum(jnp.where(hit, rank, 0.0), axis=0, keepdims=True))
    pad = 8 - TOP_K
    idx_ref[...] = jnp.concatenate(idx_rows + [jnp.zeros((1, tm), I32)] * pad, axis=0)
    wgt_ref[...] = jnp.concatenate(w_rows + [zero_row] * pad, axis=0)
    pos_ref[...] = jnp.concatenate(pos_rows + [zero_row] * pad, axis=0).astype(I32)
    cnt_ref[...] = cnt_ref[...] + jnp.sum(picked, axis=1, keepdims=True)


def _merge_router(x, ya, hb, mc, gl, mod, norm2_g, who, wmo, wout, rwh, rwl, rbias, tm):
    bsz, seq, d = x.shape
    ne = rwh.shape[0]
    nt = seq // tm
    n_tok = bsz * seq
    row = lambda n: pl.BlockSpec((None, tm, n), lambda b, i: (b, i, 0))
    col = pl.BlockSpec((8, tm), lambda b, i: (0, b * nt + i))
    return pl.pallas_call(
        functools.partial(_merge_kernel, tm=tm, ne=ne),
        grid=(bsz, nt),
        in_specs=[row(d), row(d), row(hb.shape[-1]),
                  pl.BlockSpec((None, mc.shape[1], tm), lambda b, i: (b, 0, i)), row(gl.shape[-1]),
                  pl.BlockSpec((None, N_MOD, d), lambda b, i: (b, 0, 0)),
                  _const_spec((1, d)), _const_spec(who.shape), _const_spec(wmo.shape),
                  _const_spec(wout.shape), _const_spec(rwh.shape), _const_spec(rwl.shape),
                  _const_spec((ne, 1))],
        out_specs=[row(d), row(d // 2), col, col, col,
                   pl.BlockSpec((ne, LANES), lambda b, i: (0, 0))],
        out_shape=(jax.ShapeDtypeStruct((bsz, seq, d), F32),
                   jax.ShapeDtypeStruct((bsz, seq, d // 2), jnp.uint32),
                   jax.ShapeDtypeStruct((8, n_tok), I32),
                   jax.ShapeDtypeStruct((8, n_tok), F32),
                   jax.ShapeDtypeStruct((8, n_tok), I32),
                   jax.ShapeDtypeStruct((ne, LANES), F32)),
        compiler_params=_params("arbitrary", "arbitrary"),
        name="merge_router",
    )(x, ya, hb, mc, gl, mod, norm2_g.reshape(1, d), who, wmo, wout, rwh, rwl, rbias.reshape(ne, 1))


def _expert_kernel(be_ref, nv_ref, x_ref, wg_ref, wu_ref, wd_ref, y_ref, *, rb):
    nv = nv_ref[pl.program_id(0)]

    @pl.when(nv > 0)
    def _():
        rows = lax.broadcasted_iota(I32, (rb, 1), 0)
        x = jnp.where(rows < nv, _unpack_bf16_pairs(x_ref[...]), 0.0).astype(BF16)
        g = _dot(x, wg_ref[...])
        u = _dot(x, wu_ref[...])
        y_ref[...] = _pack_bf16_pairs(_dot((_silu(g) * u).astype(BF16), wd_ref[...]))


def _experts(block_expert, block_rows, xs, wg, wu, wd, rb):
    n_rows = xs.shape[0]
    n_blocks = n_rows // rb
    _, d, de = wg.shape
    live = lambda i, be, nv: jnp.where(nv[i] > 0, i, n_blocks - 1)
    grid_spec = pltpu.PrefetchScalarGridSpec(
        num_scalar_prefetch=2,
        grid=(n_blocks,),
        in_specs=[pl.BlockSpec((rb, d // 2), lambda i, be, nv: (live(i, be, nv), 0)),
                  pl.BlockSpec((None, d, de), lambda i, be, nv: (be[i], 0, 0)),
                  pl.BlockSpec((None, d, de), lambda i, be, nv: (be[i], 0, 0)),
                  pl.BlockSpec((None, de, d), lambda i, be, nv: (be[i], 0, 0))],
        out_specs=pl.BlockSpec((rb, d // 2), lambda i, be, nv: (live(i, be, nv), 0)),
    )
    return pl.pallas_call(
        functools.partial(_expert_kernel, rb=rb),
        grid_spec=grid_spec,
        out_shape=jax.ShapeDtypeStruct((n_rows, d // 2), jnp.uint32),
        compiler_params=_params("arbitrary"),
        name="moe_experts",
    )(block_expert, block_rows, xs, wg, wu, wd)


def _sc_gather_rows(table, idx):
    n_idx = idx.shape[0]
    width = table.shape[1]
    n_workers = SC_CORES * SC_SUBCORES
    per_worker = n_idx // n_workers
    ch = SC_PIPE_CHUNK
    n_ch = per_worker // ch
    assert n_idx % (n_workers * ch * 2) == 0
    mesh = plsc.VectorSubcoreMesh(core_axis_name="c", subcore_axis_name="s")

    def body(table_hbm, idx_hbm, out_hbm, idx_v, buf0, buf1, sem0, sem1):
        wid = lax.axis_index("s") * SC_CORES + lax.axis_index("c")
        pltpu.sync_copy(idx_hbm.at[pl.ds(wid * n_ch, n_ch)], idx_v)

        def gather(j, buf, sem):
            return pltpu.make_async_copy(table_hbm.at[idx_v.at[j]], buf, sem)

        def write(j, buf):
            pltpu.sync_copy(buf, out_hbm.at[pl.ds(wid * per_worker + j * ch, ch)])

        gather(0, buf0, sem0).start()

        @pl.loop(0, n_ch // 2)
        def _(jj):
            j0 = 2 * jj
            gather(j0 + 1, buf1, sem1).start()
            gather(j0, buf0, sem0).wait()
            write(j0, buf0)
            nxt = jnp.minimum(j0 + 2, n_ch - 1)
            gather(nxt, buf0, sem0).start()
            gather(j0 + 1, buf1, sem1).wait()
            write(j0 + 1, buf1)

        gather(n_ch - 1, buf0, sem0).wait()

    return pl.kernel(
        body, mesh=mesh,
        out_type=jax.ShapeDtypeStruct((n_idx, width), table.dtype),
        scratch_types=[pltpu.VMEM((n_ch, ch), I32),
                       pltpu.VMEM((ch, width), table.dtype),
                       pltpu.VMEM((ch, width), table.dtype),
                       pltpu.SemaphoreType.DMA, pltpu.SemaphoreType.DMA],
        name="moe_gather_sc",
    )(table, idx.reshape(n_idx // ch, ch))


def _sc_scatter_rows(rows, idx, n_out, copies):
    n, width = rows.shape
    n_workers = SC_CORES * SC_SUBCORES
    per_worker = n // n_workers
    assert n % (n_workers * SC_GATHER_CHUNK) == 0
    mesh = plsc.VectorSubcoreMesh(core_axis_name="c", subcore_axis_name="s")

    def body(rows_hbm, idx_hbm, out_hbm, idx_v, rows_v):
        wid = lax.axis_index("s") * SC_CORES + lax.axis_index("c")

        @pl.loop(0, per_worker // SC_GATHER_CHUNK)
        def _(j):
            base = wid * per_worker + j * SC_GATHER_CHUNK
            pltpu.sync_copy(rows_hbm.at[pl.ds(base, SC_GATHER_CHUNK)], rows_v)
            for k in range(copies):
                pltpu.sync_copy(idx_hbm.at[pl.ds(k * n + base, SC_GATHER_CHUNK)], idx_v)
                pltpu.sync_copy(rows_v, out_hbm.at[idx_v])

    return pl.kernel(
        body, mesh=mesh,
        out_type=jax.ShapeDtypeStruct((n_out, width), rows.dtype),
        scratch_types=[pltpu.VMEM((SC_GATHER_CHUNK,), I32),
                       pltpu.VMEM((SC_GATHER_CHUNK, width), rows.dtype)],
        name="moe_dispatch_sc",
    )(rows, idx)


def _combine_kernel(wgt_ref, x1_ref, h2_ref, mod_ref, sg_ref, su_ref, sd_ref, g_ref, o_ref):
    hb = _unpack_bf16_pairs(h2_ref[...]).astype(BF16)
    hid = _silu(_dot(hb, sg_ref[...])) * _dot(hb, su_ref[...])
    y = _dot(hid.astype(BF16), sd_ref[...])
    wcol = wgt_ref[...].T
    for k in range(TOP_K):
        y = y + _unpack_bf16_pairs(g_ref[k]) * wcol[:, k:k + 1]
    o_ref[...] = x1_ref[...] + mod_ref[5:6, :] * y


def _combine(wgt, x1, h2, mod, sg, su, sd, gathered, tm):
    bsz, seq, d = x1.shape
    nt = seq // tm
    row = pl.BlockSpec((None, tm, d), lambda b, i: (b, i, 0))
    return pl.pallas_call(
        _combine_kernel,
        grid=(bsz, nt),
        in_specs=[pl.BlockSpec((8, tm), lambda b, i: (0, b * nt + i)),
                  row, pl.BlockSpec((None, tm, d // 2), lambda b, i: (b, i, 0)),
                  pl.BlockSpec((None, N_MOD, d), lambda b, i: (b, 0, 0)),
                  _const_spec(sg.shape), _const_spec(su.shape), _const_spec(sd.shape),
                  pl.BlockSpec((TOP_K, tm, d // 2), lambda b, i: (0, b * nt + i, 0))],
        out_specs=row,
        out_shape=jax.ShapeDtypeStruct((bsz, seq, d), F32),
        compiler_params=_params("arbitrary", "arbitrary"),
        name="moe_combine",
    )(wgt, x1, h2, mod, sg, su, sd, gathered)


def _dest_kernel(ps_ref, idx_ref, pos_ref, o_ref, *, ne):
    idx = idx_ref[...]
    acc = pos_ref[...]
    for e in range(ne):
        acc = acc + jnp.where(idx == e, ps_ref[e], 0)
    o_ref[...] = acc


def _routing_tables(idx, pos, counts, n_tok, ne, rb):
    n_assign = n_tok * TOP_K
    n_blocks = -(-n_assign // rb) + ne
    padded = (counts + rb - 1) // rb * rb
    pend = jnp.cumsum(padded)
    pstart = (pend - padded).astype(I32)
    tt = min(8192, n_tok)
    col = pl.BlockSpec((8, tt), lambda i: (0, i))
    dest = pl.pallas_call(
        functools.partial(_dest_kernel, ne=ne),
        grid=(n_tok // tt,),
        in_specs=[pl.BlockSpec(memory_space=pltpu.SMEM), col, col],
        out_specs=col,
        out_shape=jax.ShapeDtypeStruct((8, n_tok), I32),
        compiler_params=_params("arbitrary"),
        name="moe_dest",
    )(pstart, idx, pos)
    bstart = jnp.arange(n_blocks, dtype=I32) * rb
    bexp = jnp.minimum(jnp.sum((pend[None, :] <= bstart[:, None]).astype(I32), axis=1), ne - 1)
    brows = jnp.clip(counts[bexp] - (bstart - pstart[bexp]), 0, rb).astype(I32)
    return dest, bexp, brows, n_blocks


def _block_diag_in(bbt, g):
    gw, n = bbt.shape
    p = n // g
    rows = jnp.tile(bbt, (g, 1))
    rg = jnp.arange(g * gw)[:, None] // gw
    cg = jnp.arange(n)[None, :] // p
    return jnp.where(rg == cg, rows, 0.0)


def _block_diag_out(c, sign):
    g, gw, p = c.shape
    cols = jnp.tile(c.transpose(0, 2, 1).reshape(g * p, gw), (1, g))
    rg = jnp.arange(g * p)[:, None] // p
    cg = jnp.arange(g * gw)[None, :] // gw
    return jnp.where(rg == cg, sign * cols, 0.0)


def kernel(x, c, ada_w, ada_b, norm1_g, w_in, ssm_log_dt, ssm_lambda_re, ssm_lambda_im, ssm_b_re,
           ssm_b_im, ssm_c_re, ssm_c_im, ssm_d, ssm_w_glu, hgrn_lb_logits, hgrn_norm_g, hgrn_w_o,
           moba_q_norm_g, moba_k_norm_g, moba_w_o, w_out, norm2_g, router_w, router_bias,
           exp_w_gate, exp_w_up, exp_w_down, shared_w_gate, shared_w_up, shared_w_down):
    bsz, seq, d = x.shape
    depth = w_in.shape[0]
    n_tok = bsz * seq
    g = ssm_lambda_re.shape[1]
    d_ssm = ssm_d.shape[-1]
    d_hk = hgrn_lb_logits.shape[-1]
    d_moba = moba_w_o.shape[1]
    ne = router_w.shape[-1]
    widths = (d_ssm, 4 * d_hk, 3 * d_moba, 3 * d)
    tm = min(256, seq)
    rb = 512

    mod_all = _ada_mod(c, ada_w, ada_b).reshape(depth, bsz, N_MOD, d)
    abar, bbr, bbi, lower = _prep_params(ssm_log_dt, ssm_lambda_re, ssm_lambda_im, ssm_b_re, ssm_b_im,
                                         hgrn_lb_logits)

    for l in range(depth):
        mod = mod_all[l]
        u, hg, mo, gl = _inproj(x, mod, norm1_g[l], w_in[l].astype(BF16), widths, min(512, seq))
        wb = jnp.concatenate([_block_diag_in(bbr[l], g), _block_diag_in(bbi[l], g)], axis=1).astype(BF16)
        wc = jnp.concatenate([_block_diag_out(ssm_c_re[l], 1.0), _block_diag_out(ssm_c_im[l], -1.0)],
                             axis=0).astype(BF16)
        ya = _s5_mixer(u, wb, wc, abar[l], ssm_d[l], ssm_w_glu[l].astype(BF16), min(128, seq))
        hb = _hgrn_mixer(hg, lower[l], hgrn_norm_g[l], min(256, seq))
        mc = _moba_mixer(mo, moba_q_norm_g[l], moba_k_norm_g[l])
        rwt = router_w[l].T
        rwh = rwt.astype(BF16)
        rwl = (rwt - rwh.astype(F32)).astype(BF16)
        x1, h2, idx, wgt, pos, cnt = _merge_router(
            x, ya, hb, mc, gl, mod, norm2_g[l], hgrn_w_o[l].astype(BF16), moba_w_o[l].astype(BF16),
            w_out[l].astype(BF16), rwh, rwl, router_bias[l], tm)
        counts = cnt[:, 0].astype(I32)
        dest, bexp, brows, n_blocks = _routing_tables(idx, pos, counts, n_tok, ne, rb)
        dest_flat = dest[:TOP_K].reshape(-1)
        xs = _sc_scatter_rows(h2.reshape(n_tok, d // 2), dest_flat, n_blocks * rb, TOP_K)
        ys = _experts(bexp, brows, xs, exp_w_gate[l].astype(BF16), exp_w_up[l].astype(BF16),
                      exp_w_down[l].astype(BF16), rb)
        gathered = _sc_gather_rows(ys, dest_flat).reshape(TOP_K, n_tok, d // 2)
        x = _combine(wgt, x1, h2, mod, shared_w_gate[l].astype(BF16),
                     shared_w_up[l].astype(BF16), shared_w_down[l].astype(BF16), gathered, tm)
    return x
```
